```python
import math
import jax
import jax.numpy as jnp
from jax import lax
import numpy as np

D_MODEL = 1024
BATCH = 8
SEQ = 2048
DEPTH = 2

GRID_W = 64
CTX_LEN = 256
NORM_EPS = 1e-6

RG_WIDTH = 512
RG_HEADS = 8
RG_HEAD_DIM = RG_WIDTH // RG_HEADS
RG_CONV = 4
RG_C = 8.0

HY_WIDTH = 512
HY_ORDER = 2
HY_SHORT_CONV = 3
HY_EMB_DIM = 33
HY_FILTER_HIDDEN = 64
HY_DECAY_TARGET = 1e-2
HY_FAST_DECAY_PCT = 0.3
HY_SLOW_DECAY_PCT = 1.5

EVEN_IN = 2 * RG_WIDTH + (HY_ORDER + 1) * HY_WIDTH
EVEN_OUT = RG_WIDTH + HY_WIDTH

MLA_HEADS = 8
Q_LORA = 512
KV_LORA = 256
QK_NOPE = 128
QK_ROPE = 64
V_HEAD = 128
ODD_IN = Q_LORA + KV_LORA + QK_ROPE
ROPE_THETA = 10000.0
Q_BLOCK = 128

D_FF = 2816
FFN_CONV = 3

N_EVEN = (DEPTH + 1) // 2
N_ODD = DEPTH // 2

kernel_name = 'hybrid_rglru_hyena_mla_prefix_dit'


def rmsnorm(x, g):
    xf = x.astype(jnp.float32)
    y = xf * lax.rsqrt(jnp.mean(xf * xf, axis=-1, keepdims=True) + NORM_EPS)
    return (y * g.astype(jnp.float32)).astype(x.dtype)


def dwconv(x, w, b):
    k, ch = w.shape
    left = k // 2
    y = lax.conv_general_dilated(x, w[:, None, :].astype(x.dtype), window_strides=(1,),
                                 padding=[(left, k - 1 - left)],
                                 dimension_numbers=('NWC', 'WIO', 'NWC'),
                                 feature_group_count=ch)
    return y + b


def linear_scan(a, b, h0, reverse):
    if reverse:
        a = jnp.flip(a, 1)
        b = jnp.flip(b, 1)

    def combine(e1, e2):
        a1, b1 = e1
        a2, b2 = e2
        return a1 * a2, a2 * b1 + b2

    a_cum, b_cum = lax.associative_scan(combine, (a, b), axis=1)
    h = a_cum * h0[:, None, :] + b_cum
    return jnp.flip(h, 1) if reverse else h


def rglru_coeffs(xc, wa, ba, wx, bx, lam):
    bsz, n, w = xc.shape
    xh = xc.reshape(bsz, n, RG_HEADS, RG_HEAD_DIM)
    r = jax.nn.sigmoid(jnp.einsum('blhi,hij->blhj', xh, wa).reshape(bsz, n, w) + ba)
    i = jax.nn.sigmoid(jnp.einsum('blhi,hij->blhj', xh, wx).reshape(bsz, n, w) + bx)
    log_a = -RG_C * r.astype(jnp.float32) * jax.nn.softplus(-lam.astype(jnp.float32))
    a = jnp.exp(log_a)
    b = jnp.sqrt(-jnp.expm1(2.0 * log_a)) * (i * xc).astype(jnp.float32)
    return a, b


def rglru_bidir(x_lat, x_ctx, conv_w, conv_b, wa, ba, wx, bx, lam, need_ctx):
    x_lat = dwconv(x_lat, conv_w, conv_b)
    x_ctx = dwconv(x_ctx, conv_w, conv_b)
    ys_lat = []
    ys_ctx = []
    for d in range(2):
        rev = d == 1
        a_c, b_c = rglru_coeffs(x_ctx, wa[d], ba[d], wx[d], bx[d], lam[d])
        h_c = linear_scan(a_c, b_c, jnp.zeros_like(b_c[:, 0]), rev)
        h0 = h_c[:, 0] if rev else h_c[:, -1]
        a_l, b_l = rglru_coeffs(x_lat, wa[d], ba[d], wx[d], bx[d], lam[d])
        ys_lat.append(linear_scan(a_l, b_l, h0, rev))
        ys_ctx.append(h_c)
    y_lat = (ys_lat[0] + ys_lat[1]).astype(x_lat.dtype)
    y_ctx = (ys_ctx[0] + ys_ctx[1]).astype(x_ctx.dtype) if need_ctx else None
    return y_lat, y_ctx


def hyena_filters(n, f_w1, f_b1, f_freq, f_w2, f_b2, f_w3):
    f32 = jnp.float32
    pos = jnp.arange(n, dtype=f32)
    t = jnp.linspace(0.0, 1.0, n, dtype=f32)[:, None]
    bands = (HY_EMB_DIM - 1) // 2
    w = 2.0 * math.pi * pos / n
    f = jnp.linspace(1e-4, bands - 1, bands, dtype=f32)
    ang = w[:, None] * f[None, :]
    z = jnp.concatenate([t, jnp.cos(ang), -jnp.sin(ang)], axis=-1)
    freq = f_freq.astype(f32)
    h = jnp.sin(freq * (z @ f_w1.astype(f32) + f_b1.astype(f32)))
    h = jnp.sin(freq * (h @ f_w2.astype(f32) + f_b2.astype(f32)))
    h = (h @ f_w3.astype(f32)).reshape(n, HY_ORDER, 2, HY_WIDTH)
    max_decay = math.log(HY_DECAY_TARGET) / HY_FAST_DECAY_PCT
    min_decay = math.log(HY_DECAY_TARGET) / HY_SLOW_DECAY_PCT
    deltas = jnp.linspace(min_decay, max_decay, HY_WIDTH, dtype=f32)
    decay = jnp.exp(-t * jnp.abs(deltas)[None, :])
    h = h * decay[:, None, None, :]
    k = jnp.concatenate([h[:, :, 0],
                         jnp.zeros((1, HY_ORDER, HY_WIDTH), f32),
                         jnp.flip(h[1:, :, 1], 0)], axis=0)
    return k / jnp.sum(jnp.abs(k), axis=0, keepdims=True)


def hyena(z, conv_w, conv_b, f_w1, f_b1, f_freq, f_w2, f_b2, f_w3, bias):
    bsz, n, _ = z.shape
    z = dwconv(z, conv_w, conv_b)
    v, x1, x2 = jnp.split(z, 3, axis=-1)
    k_f = jnp.fft.rfft(hyena_filters(n, f_w1, f_b1, f_freq, f_w2, f_b2, f_w3), axis=0)
    u = v
    for o, gate in enumerate((x1, x2)):
        u_f = jnp.fft.rfft(u.astype(jnp.float32), n=2 * n, axis=1)
        y = jnp.fft.irfft(u_f * k_f[None, :, o], n=2 * n, axis=1)[:, :n].astype(z.dtype)
        u = gate * (y + u * bias[o])
    return u


def axial_rope(n):
    f32 = jnp.float32
    rows = n // GRID_W
    row = jnp.repeat(jnp.arange(rows, dtype=f32), GRID_W)
    col = jnp.tile(jnp.arange(GRID_W, dtype=f32), rows)
    n_freq = QK_ROPE // 4
    inv = ROPE_THETA ** (-jnp.arange(n_freq, dtype=f32) / n_freq)
    ang = jnp.concatenate([row[:, None] * inv[None, :], col[:, None] * inv[None, :]], axis=-1)
    return jnp.cos(ang), jnp.sin(ang)


def apply_rope(x, cos, sin):
    xr = x.reshape(x.shape[:-1] + (x.shape[-1] // 2, 2))
    x1 = xr[..., 0]
    x2 = xr[..., 1]
    out = jnp.stack([x1 * cos - x2 * sin, x1 * sin + x2 * cos], axis=-1)
    return out.reshape(x.shape).astype(x.dtype)


def attend(qn, qr, kn, kr, v):
    scale = (QK_NOPE + QK_ROPE) ** -0.5
    s = jnp.einsum('bqhd,bkhd->bhqk', qn, kn) + jnp.einsum('bqhr,bkr->bhqk', qr, kr)
    p = jax.nn.softmax(s.astype(jnp.float32) * scale, axis=-1)
    return jnp.einsum('bhqk,bkhd->bqhd', p.astype(v.dtype), v)


def mla_kv(zkv, kv_g, w_ukv):
    bsz, n, _ = zkv.shape
    ckv = rmsnorm(zkv[..., :KV_LORA], kv_g)
    kv = (ckv @ w_ukv).reshape(bsz, n, MLA_HEADS, QK_NOPE + V_HEAD)
    return kv[..., :QK_NOPE], zkv[..., KV_LORA:], kv[..., QK_NOPE:]


def mla_q(zq, q_g, w_uq):
    bsz, n, _ = zq.shape
    q = (rmsnorm(zq, q_g) @ w_uq).reshape(bsz, n, MLA_HEADS, QK_NOPE + QK_ROPE)
    return q[..., :QK_NOPE], q[..., QK_NOPE:]


def mla_mixer(h_l, h_c, w_in, q_g, kv_g, w_uq, w_ukv, w_o, cos, sin, need_ctx):
    bsz, n, _ = h_l.shape
    z_l = h_l @ w_in
    kn_l, kr_l, v_l = mla_kv(z_l[..., Q_LORA:], kv_g, w_ukv)
    kr_l = apply_rope(kr_l, cos[None], sin[None])
    kn_c, kr_c, v_c = mla_kv(h_c @ w_in[:, Q_LORA:], kv_g, w_ukv)
    kn = jnp.concatenate([kn_l, kn_c], axis=1)
    kr = jnp.concatenate([kr_l, kr_c], axis=1)
    v = jnp.concatenate([v_l, v_c], axis=1)
    qn_l, qr_l = mla_q(z_l[..., :Q_LORA], q_g, w_uq)
    qr_l = apply_rope(qr_l, cos[None, :, None], sin[None, :, None])
    nb = n // Q_BLOCK

    def blocks(t):
        return jnp.swapaxes(t.reshape((bsz, nb, Q_BLOCK) + t.shape[2:]), 0, 1)

    o = lax.map(lambda q: attend(q[0], q[1], kn, kr, v), (blocks(qn_l), blocks(qr_l)))
    out_l = jnp.swapaxes(o, 0, 1).reshape(bsz, n, MLA_HEADS * V_HEAD) @ w_o
    out_c = None
    if need_ctx:
        qn_c, qr_c = mla_q(h_c @ w_in[:, :Q_LORA], q_g, w_uq)
        o_c = attend(qn_c, qr_c, kn_c, kr_c, v_c)
        out_c = o_c.reshape(bsz, o_c.shape[1], MLA_HEADS * V_HEAD) @ w_o
    return out_l, out_c


def even_mixer(h_l, h_c, w_in, rg_conv_w, rg_conv_b, rg_wa, rg_ba, rg_wx, rg_bx, rg_lambda,
               hy_conv_w, hy_conv_b, f_w1, f_b1, f_freq, f_w2, f_b2, f_w3, hy_bias, w_out, need_ctx):
    z_l = h_l @ w_in
    zc_rg = h_c @ w_in[:, :RG_WIDTH]
    y_rg_l, y_rg_c = rglru_bidir(z_l[..., :RG_WIDTH], zc_rg, rg_conv_w, rg_conv_b,
                                 rg_wa, rg_ba, rg_wx, rg_bx, rg_lambda, need_ctx)
    a_l = y_rg_l * jax.nn.gelu(z_l[..., RG_WIDTH:2 * RG_WIDTH])
    b_l = hyena(z_l[..., 2 * RG_WIDTH:], hy_conv_w, hy_conv_b, f_w1, f_b1, f_freq, f_w2, f_b2, f_w3, hy_bias)
    out_l = jnp.concatenate([a_l, b_l], axis=-1) @ w_out
    out_c = None
    if need_ctx:
        zc = h_c @ w_in[:, RG_WIDTH:]
        a_c = y_rg_c * jax.nn.gelu(zc[..., :RG_WIDTH])
        b_c = hyena(zc[..., RG_WIDTH:], hy_conv_w, hy_conv_b, f_w1, f_b1, f_freq, f_w2, f_b2, f_w3, hy_bias)
        out_c = jnp.concatenate([a_c, b_c], axis=-1) @ w_out
    return out_l, out_c


def conv_ffn(h, w_up, conv_w, conv_b, w_down):
    u = dwconv(h @ w_up, conv_w, conv_b)
    g, v = jnp.split(u, 2, axis=-1)
    return (jax.nn.silu(g) * v) @ w_down


def setup_inputs(seed: int = 0) -> dict:
    key = jax.random.key(seed)
    ks = iter(jax.random.split(key, 64))
    f32 = jnp.float32
    D = D_MODEL
    F2 = 2 * D_FF

    def nrm(shape, scale):
        return jax.random.normal(next(ks), shape, f32) * scale

    def gain(shape):
        return 1.0 + nrm(shape, 0.05)

    u = jax.random.uniform(next(ks), (N_EVEN, 2, RG_WIDTH), f32, 0.9, 0.999)
    a = u ** (1.0 / RG_C)
    lam = jnp.log(a) - jnp.log1p(-a)
    return {
        'x': nrm((BATCH, SEQ, D), 1.0),
        'c': nrm((BATCH, D), 1.0),
        'ctx': nrm((BATCH, CTX_LEN, D), 1.0),
        'c_ctx': nrm((D,), 1.0),
        'ada_w': nrm((DEPTH, D, 6 * D), 0.5 * D ** -0.5),
        'ada_b': nrm((DEPTH, 6 * D), 0.02),
        'norm1_g': gain((DEPTH, D)),
        'norm2_g': gain((DEPTH, D)),
        'ev_w_in': nrm((N_EVEN, D, EVEN_IN), D ** -0.5),
        'ev_rg_conv_w': nrm((N_EVEN, RG_CONV, RG_WIDTH), RG_CONV ** -0.5),
        'ev_rg_conv_b': nrm((N_EVEN, RG_WIDTH), 0.02),
        'ev_rg_wa': nrm((N_EVEN, 2, RG_HEADS, RG_HEAD_DIM, RG_HEAD_DIM), RG_HEAD_DIM ** -0.5),
        'ev_rg_ba': nrm((N_EVEN, 2, RG_WIDTH), 0.02),
        'ev_rg_wx': nrm((N_EVEN, 2, RG_HEADS, RG_HEAD_DIM, RG_HEAD_DIM), RG_HEAD_DIM ** -0.5),
        'ev_rg_bx': nrm((N_EVEN, 2, RG_WIDTH), 0.02),
        'ev_rg_lambda': lam,
        'ev_hy_conv_w': nrm((N_EVEN, HY_SHORT_CONV, (HY_ORDER + 1) * HY_WIDTH), HY_SHORT_CONV ** -0.5),
        'ev_hy_conv_b': nrm((N_EVEN, (HY_ORDER + 1) * HY_WIDTH), 0.02),
        'ev_hy_f_w1': nrm((N_EVEN, HY_EMB_DIM, HY_FILTER_HIDDEN), HY_EMB_DIM ** -0.5),
        'ev_hy_f_b1': nrm((N_EVEN, HY_FILTER_HIDDEN), 0.1),
        'ev_hy_f_freq': 1.0 + nrm((N_EVEN, HY_FILTER_HIDDEN), 0.1),
        'ev_hy_f_w2': nrm((N_EVEN, HY_FILTER_HIDDEN, HY_FILTER_HIDDEN), HY_FILTER_HIDDEN ** -0.5),
        'ev_hy_f_b2': nrm((N_EVEN, HY_FILTER_HIDDEN), 0.1),
        'ev_hy_f_w3': nrm((N_EVEN, HY_FILTER_HIDDEN, HY_ORDER * 2 * HY_WIDTH), HY_FILTER_HIDDEN ** -0.5),
        'ev_hy_bias': nrm((N_EVEN, HY_ORDER, HY_WIDTH), 0.5),
        'ev_w_out': nrm((N_EVEN, EVEN_OUT, D), EVEN_OUT ** -0.5),
        'od_w_in': nrm((N_ODD, D, ODD_IN), D ** -0.5),
        'od_q_norm_g': gain((N_ODD, Q_LORA)),
        'od_kv_norm_g': gain((N_ODD, KV_LORA)),
        'od_w_uq': nrm((N_ODD, Q_LORA, MLA_HEADS * (QK_NOPE + QK_ROPE)), Q_LORA ** -0.5),
        'od_w_ukv': nrm((N_ODD, KV_LORA, MLA_HEADS * (QK_NOPE + V_HEAD)), KV_LORA ** -0.5),
        'od_w_o': nrm((N_ODD, MLA_HEADS * V_HEAD, D), (MLA_HEADS * V_HEAD) ** -0.5),
        'ffn_w_up': nrm((DEPTH, D, F2), D ** -0.5),
        'ffn_conv_w': nrm((DEPTH, FFN_CONV, F2), FFN_CONV ** -0.5),
        'ffn_conv_b': nrm((DEPTH, F2), 0.02),
        'ffn_w_down': nrm((DEPTH, D_FF, D), D_FF ** -0.5),
        'final_g': gain((D,)),
    }


def reference(x, c, ctx, c_ctx, ada_w, ada_b, norm1_g, norm2_g,
              ev_w_in, ev_rg_conv_w, ev_rg_conv_b, ev_rg_wa, ev_rg_ba, ev_rg_wx, ev_rg_bx, ev_rg_lambda,
              ev_hy_conv_w, ev_hy_conv_b, ev_hy_f_w1, ev_hy_f_b1, ev_hy_f_freq, ev_hy_f_w2, ev_hy_f_b2,
              ev_hy_f_w3, ev_hy_bias, ev_w_out,
              od_w_in, od_q_norm_g, od_kv_norm_g, od_w_uq, od_w_ukv, od_w_o,
              ffn_w_up, ffn_conv_w, ffn_conv_b, ffn_w_down, final_g):
    n = x.shape[1]
    cos, sin = axial_rope(n)
    s_c = jax.nn.silu(c)
    s_cc = jax.nn.silu(c_ctx)
    xc = ctx
    for layer in range(DEPTH):
        need_ctx = layer < DEPTH - 1
        mod_l = (s_c @ ada_w[layer] + ada_b[layer])[:, None, :]
        mod_c = (s_cc @ ada_w[layer] + ada_b[layer])[None, None, :]
        sh1_l, sc1_l, g1_l, sh2_l, sc2_l, g2_l = jnp.split(mod_l, 6, axis=-1)
        sh1_c, sc1_c, g1_c, sh2_c, sc2_c, g2_c = jnp.split(mod_c, 6, axis=-1)
        h_l = rmsnorm(x, norm1_g[layer]) * (1.0 + sc1_l) + sh1_l
        h_c = rmsnorm(xc, norm1_g[layer]) * (1.0 + sc1_c) + sh1_c
        e = layer // 2
        if layer % 2 == 0:
            out_l, out_c = even_mixer(h_l, h_c, ev_w_in[e], ev_rg_conv_w[e], ev_rg_conv_b[e],
                                      ev_rg_wa[e], ev_rg_ba[e], ev_rg_wx[e], ev_rg_bx[e], ev_rg_lambda[e],
                                      ev_hy_conv_w[e], ev_hy_conv_b[e], ev_hy_f_w1[e], ev_hy_f_b1[e],
                                      ev_hy_f_freq[e], ev_hy_f_w2[e], ev_hy_f_b2[e], ev_hy_f_w3[e],
                                      ev_hy_bias[e], ev_w_out[e], need_ctx)
        else:
            out_l, out_c = mla_mixer(h_l, h_c, od_w_in[e], od_q_norm_g[e], od_kv_norm_g[e],
                                     od_w_uq[e], od_w_ukv[e], od_w_o[e], cos, sin, need_ctx)
        x = x + g1_l * out_l
        h2_l = rmsnorm(x, norm2_g[layer]) * (1.0 + sc2_l) + sh2_l
        x = x + g2_l * conv_ffn(h2_l, ffn_w_up[layer], ffn_conv_w[layer], ffn_conv_b[layer], ffn_w_down[layer])
        if need_ctx:
            xc = xc + g1_c * out_c
            h2_c = rmsnorm(xc, norm2_g[layer]) * (1.0 + sc2_c) + sh2_c
            xc = xc + g2_c * conv_ffn(h2_c, ffn_w_up[layer], ffn_conv_w[layer], ffn_conv_b[layer], ffn_w_down[layer])
    return rmsnorm(x, final_g)
```

```python
import functools
import math

import numpy as np
import jax
import jax.numpy as jnp
from jax import lax
from jax.experimental import pallas as pl
from jax.experimental.pallas import tpu as pltpu

F32 = jnp.float32
BF16 = jnp.bfloat16

D = 1024
B = 8
S = 2048
C = 256
T = S + C
EPS = 1e-6

RG_W = 512
RG_HEADS = 8
RG_HD = 64
RG_C = 8.0
HY_W = 512
HY_EMB = 33
HY_HID = 64
EVEN_IN = 2560

HEADS = 8
Q_LORA = 512
KV_LORA = 256
NOPE = 128
ROPE = 64
VH = 128
HQ = 256
ODD_IN_P = 896
D_FF = 2816

VMEM_LIMIT = 52 * 1024 * 1024


def _cp(sem):
    return pltpu.CompilerParams(dimension_semantics=sem, vmem_limit_bytes=VMEM_LIMIT)


def _dot(a, b):
    return jnp.dot(a, b, preferred_element_type=F32)


def _split(x):
    hi = x.astype(BF16)
    lo = (x - hi.astype(F32)).astype(BF16)
    return hi, lo


def _dot3(ah, al, bh, bl):
    return _dot(ah, bh) + (_dot(ah, bl) + _dot(al, bh))


def _dot3f(a, b):
    ah, al = _split(a)
    bh, bl = _split(b)
    return _dot3(ah, al, bh, bl)


def _sigmoid(x):
    return 1.0 / (1.0 + jnp.exp(-x))


def _mod_rows(m_ref, k, row, ctx_start):
    lat = m_ref[0, k:k + 1, :]
    if ctx_start is None:
        return lat
    return jnp.where(row >= ctx_start, m_ref[1, k:k + 1, :], lat)


def _mod_kernel(c_ref, w_ref, b_ref, o_ref):
    c = c_ref[...]
    s = (c * _sigmoid(c)).astype(BF16)
    o_ref[...] = _dot(s, w_ref[...].astype(BF16)) + b_ref[...]


def modulation(cvec, ada_w, ada_b):
    depth = ada_w.shape[0]
    tn = 1024
    return pl.pallas_call(
        _mod_kernel,
        grid=(depth, 6 * D // tn),
        in_specs=[pl.BlockSpec((16, D), lambda l, j: (0, 0)),
                  pl.BlockSpec((None, D, tn), lambda l, j: (l, 0, j)),
                  pl.BlockSpec((None, 1, tn), lambda l, j: (l, 0, j))],
        out_specs=pl.BlockSpec((None, 16, tn), lambda l, j: (l, 0, j)),
        out_shape=jax.ShapeDtypeStruct((depth, 16, 6 * D), F32),
        compiler_params=_cp(("parallel", "parallel")),
        name="modulation",
    )(cvec, ada_w, ada_b.reshape(depth, 1, 6 * D))


def _nmm_kernel(x_ref, g_ref, m_ref, w_ref, o_ref, h_ref, *, tm, ish, isc, ctx_start, nchunk):
    x = x_ref[...]
    ms = jnp.mean(x * x, axis=-1, keepdims=True)
    y = x * lax.rsqrt(ms + EPS) * g_ref[...]
    row = pl.program_id(1) * tm + lax.broadcasted_iota(jnp.int32, (tm, 1), 0)
    sc = _mod_rows(m_ref, isc, row, ctx_start)
    sh = _mod_rows(m_ref, ish, row, ctx_start)
    h_ref[...] = (y * (1.0 + sc) + sh).astype(BF16)
    n = o_ref.shape[-1]
    for n0 in range(0, n, nchunk):
        n1 = min(n, n0 + nchunk)
        o_ref[:, n0:n1] = _dot(h_ref[...], w_ref[:, n0:n1]).astype(o_ref.dtype)


def norm_mod_matmul(xs, g, mod, ish, isc, w, out_dtype, tm, ctx_start, name):
    bsz, t, _ = xs.shape
    n = w.shape[1]
    kern = functools.partial(_nmm_kernel, tm=tm, ish=ish, isc=isc, ctx_start=ctx_start, nchunk=512)
    return pl.pallas_call(
        kern,
        grid=(bsz, t // tm),
        in_specs=[pl.BlockSpec((None, tm, D), lambda b, i: (b, i, 0)),
                  pl.BlockSpec((1, D), lambda b, i: (0, 0)),
                  pl.BlockSpec((None, 2, 6, D), lambda b, i: (b, 0, 0, 0)),
                  pl.BlockSpec((D, n), lambda b, i: (0, 0))],
        out_specs=pl.BlockSpec((None, tm, n), lambda b, i: (b, i, 0)),
        out_shape=jax.ShapeDtypeStruct((bsz, t, n), out_dtype),
        scratch_shapes=[pltpu.VMEM((tm, D), BF16)],
        compiler_params=_cp(("parallel", "parallel")),
        name=name,
    )(xs, g.reshape(1, D), mod, w)


def _mmres_kernel(*refs, nparts, tm, ig, ctx_start):
    parts = refs[:nparts]
    w_ref, x_ref, m_ref, o_ref = refs[nparts:nparts + 4]
    acc = None
    k0 = 0
    for p in parts:
        kp = p.shape[-1]
        d = _dot(p[...], w_ref[k0:k0 + kp, :])
        acc = d if acc is None else acc + d
        k0 += kp
    row = pl.program_id(1) * tm + lax.broadcasted_iota(jnp.int32, (tm, 1), 0)
    gate = _mod_rows(m_ref, ig, row, ctx_start)
    o_ref[...] = x_ref[...] + gate * acc


def matmul_residual(parts, w, xs, mod, ig, t_out, tm, ctx_start, name):
    bsz = xs.shape[0]
    kern = functools.partial(_mmres_kernel, nparts=len(parts), tm=tm, ig=ig, ctx_start=ctx_start)
    in_specs = [pl.BlockSpec((None, tm, p.shape[-1]), lambda b, i: (b, i, 0)) for p in parts]
    in_specs += [pl.BlockSpec(w.shape, lambda b, i: (0, 0)),
                 pl.BlockSpec((None, tm, D), lambda b, i: (b, i, 0)),
                 pl.BlockSpec((None, 2, 6, D), lambda b, i: (b, 0, 0, 0))]
    return pl.pallas_call(
        kern,
        grid=(bsz, t_out // tm),
        in_specs=in_specs,
        out_specs=pl.BlockSpec((None, tm, D), lambda b, i: (b, i, 0)),
        out_shape=jax.ShapeDtypeStruct((bsz, t_out, D), F32),
        compiler_params=_cp(("parallel", "parallel")),
        name=name,
    )(*parts, w, xs, mod)


def _ffn2_kernel(u_ref, up_ref, un_ref, cw_ref, cb_ref, wd_ref, x_ref, m_ref, fg_ref, o_ref, acc_ref,
                 *, tm, t, seg_starts, ctx_start, final, ck):
    i = pl.program_id(1)
    lrow = lax.broadcasted_iota(jnp.int32, (tm, 1), 0)
    row = i * tm + lrow
    no_prev = row == seg_starts[0]
    no_next = row == t - 1
    for s0 in seg_starts[1:]:
        no_prev = no_prev | (row == s0)
        no_next = no_next | (row == s0 - 1)
    first = lrow == 0
    last = lrow == tm - 1

    def conv(c0):
        cur = u_ref[:, c0:c0 + ck].astype(F32)
        pr = up_ref[15:16, c0:c0 + ck].astype(F32)
        nx = un_ref[0:1, c0:c0 + ck].astype(F32)
        prev = jnp.where(first, pr, pltpu.roll(cur, 1, 0))
        prev = jnp.where(no_prev, 0.0, prev)
        nxt = jnp.where(last, nx, pltpu.roll(cur, tm - 1, 0))
        nxt = jnp.where(no_next, 0.0, nxt)
        w = cw_ref[:, c0:c0 + ck]
        return prev * w[0:1] + cur * w[1:2] + nxt * w[2:3] + cb_ref[:, c0:c0 + ck]

    for k, c0 in enumerate(range(0, D_FF, ck)):
        g = conv(c0)
        v = conv(D_FF + c0)
        a = (g * _sigmoid(g) * v).astype(BF16)
        d = _dot(a, wd_ref[c0:c0 + ck, :])
        if k == 0:
            acc_ref[...] = d
        else:
            acc_ref[...] += d
    gate = _mod_rows(m_ref, 5, row, ctx_start)
    out = x_ref[...] + gate * acc_ref[...]
    if final:
        ms = jnp.mean(out * out, axis=-1, keepdims=True)
        out = out * lax.rsqrt(ms + EPS) * fg_ref[...]
    o_ref[...] = out


def ffn_second_half(u, conv_w, conv_b, w_down, xs, mod, final_g, tm, seg_starts, ctx_start, final, name):
    bsz, t, f2 = u.shape
    hb = 16
    nhb = t // hb
    r = tm // hb
    kern = functools.partial(_ffn2_kernel, tm=tm, t=t, seg_starts=seg_starts, ctx_start=ctx_start,
                             final=final, ck=256)
    return pl.pallas_call(
        kern,
        grid=(bsz, t // tm),
        in_specs=[pl.BlockSpec((None, tm, f2), lambda b, i: (b, i, 0)),
                  pl.BlockSpec((None, hb, f2), lambda b, i: (b, jnp.maximum(i * r - 1, 0), 0)),
                  pl.BlockSpec((None, hb, f2), lambda b, i: (b, jnp.minimum((i + 1) * r, nhb - 1), 0)),
                  pl.BlockSpec((3, f2), lambda b, i: (0, 0)),
                  pl.BlockSpec((1, f2), lambda b, i: (0, 0)),
                  pl.BlockSpec((D_FF, D), lambda b, i: (0, 0)),
                  pl.BlockSpec((None, tm, D), lambda b, i: (b, i, 0)),
                  pl.BlockSpec((None, 2, 6, D), lambda b, i: (b, 0, 0, 0)),
                  pl.BlockSpec((1, D), lambda b, i: (0, 0))],
        out_specs=pl.BlockSpec((None, tm, D), lambda b, i: (b, i, 0)),
        out_shape=jax.ShapeDtypeStruct((bsz, t, D), F32),
        scratch_shapes=[pltpu.VMEM((tm, D), F32)],
        compiler_params=_cp(("parallel", "parallel")),
        name=name,
    )(u, u, u, conv_w, conv_b.reshape(1, f2), w_down, xs, mod, final_g.reshape(1, D))


PAD = 8


def _fill_padded(xp_ref, x_ref, width):
    z = jnp.zeros((PAD, width), F32)
    xp_ref[0:PAD, :] = z
    xp_ref[PAD:PAD + S, :] = x_ref[0:S, :].astype(F32)
    xp_ref[PAD + S:2 * PAD + S, :] = z
    xp_ref[2 * PAD + S:2 * PAD + T, :] = x_ref[S:T, :].astype(F32)
    xp_ref[2 * PAD + T:3 * PAD + T, :] = z


def _conv_tile(xp_ref, w, bias, r0, rows, ktaps):
    base = PAD + r0 if r0 < S else 2 * PAD + r0
    left = ktaps // 2
    acc = bias
    for j in range(ktaps):
        acc = acc + xp_ref[base + j - left:base + j - left + rows, :] * w[j:j + 1, :]
    return acc


def _hconv_kernel(z_ref, w_ref, b_ref, o_ref, xp_ref):
    _fill_padded(xp_ref, z_ref, 512)
    w = w_ref[...]
    bias = b_ref[...]
    for r0 in range(0, T, 256):
        o_ref[r0:r0 + 256, :] = _conv_tile(xp_ref, w, bias, r0, 256, 3)


def hyena_short_conv(z, conv_w, conv_b):
    return pl.pallas_call(
        _hconv_kernel,
        grid=(B, 3),
        in_specs=[pl.BlockSpec((None, T, 512), lambda b, j: (b, 0, 2 + j)),
                  pl.BlockSpec((3, 512), lambda b, j: (0, j)),
                  pl.BlockSpec((1, 512), lambda b, j: (0, j))],
        out_specs=pl.BlockSpec((None, T, 512), lambda b, j: (b, 0, j)),
        out_shape=jax.ShapeDtypeStruct((B, T, 3 * HY_W), F32),
        scratch_shapes=[pltpu.VMEM((T + 3 * PAD, 512), F32)],
        compiler_params=_cp(("parallel", "parallel")),
        name="hyena_short_conv",
    )(z, conv_w, conv_b.reshape(1, 3 * HY_W))


RGH = 256


def _rg_kernel(x_ref, gt_ref, cw_ref, cb_ref, wbd_ref, bias_ref, sp_ref, o_ref,
               xp_ref, af_ref, bf_ref, ar_ref, br_ref):
    _fill_padded(xp_ref, x_ref, RGH)
    cw = cw_ref[...]
    cb = cb_ref[...]
    spf = sp_ref[0:1, :]
    spr = sp_ref[1:2, :]
    for r0 in range(0, T, 256):
        xc = _conv_tile(xp_ref, cw, cb, r0, 256, 4)
        g = _dot(xc.astype(BF16), wbd_ref[...])
        for d, (a_ref, b_ref, sp) in enumerate(((af_ref, bf_ref, spf), (ar_ref, br_ref, spr))):
            rg = _sigmoid(g[:, (2 * d) * RGH:(2 * d + 1) * RGH] + bias_ref[2 * d:2 * d + 1, :])
            ig = _sigmoid(g[:, (2 * d + 1) * RGH:(2 * d + 2) * RGH] + bias_ref[2 * d + 1:2 * d + 2, :])
            log_a = -RG_C * rg * sp
            a = jnp.exp(log_a)
            a_ref[r0:r0 + 256, :] = a
            b_ref[r0:r0 + 256, :] = jnp.sqrt(1.0 - a * a) * (ig * xc)

    def make_body(f0, r_last):
        def body(i, carry):
            hf, hr = carry
            tf = f0 + i
            tr = r_last - i
            hf = af_ref[pl.ds(tf, 1), :] * hf + bf_ref[pl.ds(tf, 1), :]
            bf_ref[pl.ds(tf, 1), :] = hf
            hr = ar_ref[pl.ds(tr, 1), :] * hr + br_ref[pl.ds(tr, 1), :]
            br_ref[pl.ds(tr, 1), :] = hr
            return hf, hr
        return body

    zero = jnp.zeros((1, RGH), F32)
    carry = lax.fori_loop(0, C, make_body(S, T - 1), (zero, zero), unroll=8)
    lax.fori_loop(0, S, make_body(0, S - 1), carry, unroll=8)

    c0 = math.sqrt(2.0 / math.pi)
    for r0 in range(0, T, 256):
        y = bf_ref[r0:r0 + 256, :] + br_ref[r0:r0 + 256, :]
        gt = gt_ref[r0:r0 + 256, :].astype(F32)
        gelu = 0.5 * gt * (1.0 + jnp.tanh(c0 * (gt + 0.044715 * (gt * gt * gt))))
        o_ref[r0:r0 + 256, :] = (y * gelu).astype(o_ref.dtype)


def rglru(z, conv_w, conv_b, wbd, bias4, softplus_neg_lam):
    nh = RG_W // RGH
    return pl.pallas_call(
        _rg_kernel,
        grid=(B, nh),
        in_specs=[pl.BlockSpec((None, T, RGH), lambda b, j: (b, 0, j)),
                  pl.BlockSpec((None, T, RGH), lambda b, j: (b, 0, nh + j)),
                  pl.BlockSpec((4, RGH), lambda b, j: (0, j)),
                  pl.BlockSpec((1, RGH), lambda b, j: (0, j)),
                  pl.BlockSpec((None, RGH, 4 * RGH), lambda b, j: (j, 0, 0)),
                  pl.BlockSpec((4, RGH), lambda b, j: (0, j)),
                  pl.BlockSpec((2, RGH), lambda b, j: (0, j))],
        out_specs=pl.BlockSpec((None, T, RGH), lambda b, j: (b, 0, j)),
        out_shape=jax.ShapeDtypeStruct((B, T, RG_W), BF16),
        scratch_shapes=[pltpu.VMEM((T + 3 * PAD, RGH), F32)] + [pltpu.VMEM((T, RGH), F32)] * 4,
        compiler_params=_cp(("parallel", "parallel")),
        name="rglru",
    )(z, z, conv_w, conv_b.reshape(1, RG_W), wbd, bias4, softplus_neg_lam)


def _softplus_kernel(x_ref, o_ref):
    y = -x_ref[...]
    o_ref[...] = jnp.maximum(y, 0.0) + jnp.log(1.0 + jnp.exp(-jnp.abs(y)))


def softplus_neg(lam):
    return pl.pallas_call(
        _softplus_kernel,
        out_shape=jax.ShapeDtypeStruct(lam.shape, F32),
        name="softplus_neg",
    )(lam)


def _hfilt_kernel(z_ref, w1_ref, b1_ref, fr_ref, w2_ref, b2_ref, w3_ref, dec_ref, o_ref, *, n, tr):
    fr = fr_ref[...]
    w1h, w1l = _split(w1_ref[...])
    w2h, w2l = _split(w2_ref[...])
    w3h, w3l = _split(w3_ref[...])
    tot = jnp.zeros((1, HY_W), F32)
    for r0 in range(0, n, tr):
        zh, zl = _split(z_ref[r0:r0 + tr, :])
        h = jnp.sin(fr * (_dot3(zh, zl, w1h, w1l) + b1_ref[...]))
        hh, hl = _split(h)
        h = jnp.sin(fr * (_dot3(hh, hl, w2h, w2l) + b2_ref[...]))
        hh, hl = _split(h)
        h = _dot3(hh, hl, w3h, w3l)
        dec = dec_ref[r0:r0 + tr, :]
        h0 = h[:, :HY_W] * dec
        h1 = h[:, HY_W:] * dec
        if r0 == 0:
            h1 = jnp.where(lax.broadcasted_iota(jnp.int32, (tr, 1), 0) == 0, 0.0, h1)
        tot = tot + jnp.sum(jnp.abs(h0) + jnp.abs(h1), axis=0, keepdims=True)
        o_ref[r0:r0 + tr, :HY_W] = h0 + h1
        o_ref[r0:r0 + tr, HY_W:] = h0 - h1
    inv = 1.0 / tot
    inv2 = jnp.concatenate([inv, inv], axis=-1)
    for r0 in range(0, n, tr):
        o_ref[r0:r0 + tr, :] = o_ref[r0:r0 + tr, :] * inv2


def hyena_filter_halves(n, zemb, w1p, b1, freq, w2, b2, w3, decay):
    kern = functools.partial(_hfilt_kernel, n=n, tr=256)
    full = lambda shape: pl.BlockSpec(shape, lambda o: (0,) * len(shape))
    return pl.pallas_call(
        kern,
        grid=(2,),
        in_specs=[full((n, 128)), full((128, HY_HID)), full((1, HY_HID)), full((1, HY_HID)),
                  full((HY_HID, HY_HID)), full((1, HY_HID)),
                  pl.BlockSpec((HY_HID, 2 * HY_W), lambda o: (0, o)),
                  full((n, HY_W))],
        out_specs=pl.BlockSpec((n, 2 * HY_W), lambda o: (0, o)),
        out_shape=jax.ShapeDtypeStruct((n, 4 * HY_W), F32),
        compiler_params=_cp(("parallel",)),
        name=f"hyena_filter_{n}",
    )(zemb, w1p, b1.reshape(1, HY_HID), freq.reshape(1, HY_HID), w2, b2.reshape(1, HY_HID), w3, decay)


def _cmm_kernel(ah_ref, al_ref, x_ref, o_ref, xh_ref, xl_ref):
    @pl.when(pl.program_id(2) == 0)
    def _():
        xh, xl = _split(x_ref[...])
        xh_ref[...] = xh
        xl_ref[...] = xl

    o_ref[...] = _dot3(ah_ref[...], al_ref[...], xh_ref[...], xl_ref[...])


def const_matmul(ah, al, x, k, x_row_block, ncols, tn, tmf, name):
    m = ah.shape[0]
    bx = x.shape[0]
    return pl.pallas_call(
        _cmm_kernel,
        grid=(bx, ncols // tn, m // tmf),
        in_specs=[pl.BlockSpec((tmf, k), lambda b, j, i: (i, 0)),
                  pl.BlockSpec((tmf, k), lambda b, j, i: (i, 0)),
                  pl.BlockSpec((None, k, tn), lambda b, j, i: (b, x_row_block, j))],
        out_specs=pl.BlockSpec((None, tmf, tn), lambda b, j, i: (b, i, j)),
        out_shape=jax.ShapeDtypeStruct((bx, m, ncols), F32),
        scratch_shapes=[pltpu.VMEM((k, tn), BF16), pltpu.VMEM((k, tn), BF16)],
        compiler_params=_cp(("parallel", "parallel", "arbitrary")),
        name=name,
    )(ah, al, x)


def _hinv_kernel(u_ref, kt_ref, kb_ref, kn_ref, ah_ref, al_ref, up_ref, gt_ref, bias_ref, o_ref,
                 yh_ref, yl_ref, *, n, tc):
    @pl.when(pl.program_id(2) == 0)
    def _():
        for r0 in range(0, n, tc):
            ur = u_ref[r0:r0 + tc, :]
            ui = u_ref[n + r0:n + r0 + tc, :]
            kr = kt_ref[r0:r0 + tc, :]
            ki = kb_ref[r0:r0 + tc, :]
            yr = ur * kr - ui * ki
            yi = ur * ki + ui * kr
            if r0 == 0:
                first = lax.broadcasted_iota(jnp.int32, (tc, 1), 0) == 0
                yr = jnp.where(first, ur * kr, yr)
                yi = jnp.where(first, ui * kn_ref[0:1, :], yi)
            h, l = _split(yr)
            yh_ref[r0:r0 + tc, :] = h
            yl_ref[r0:r0 + tc, :] = l
            h, l = _split(yi)
            yh_ref[n + r0:n + r0 + tc, :] = h
            yl_ref[n + r0:n + r0 + tc, :] = l

    y = _dot3(ah_ref[...], al_ref[...], yh_ref[...], yl_ref[...])
    up = up_ref[...].astype(F32)
    o_ref[...] = (gt_ref[...] * (y + up * bias_ref[...])).astype(o_ref.dtype)


def hyena_inverse(u_spec, kf, order, aih, ail, uprev, uprev_rb, uprev_cb, gate, gate_rb, gate_cb,
                  bias, n, tmi, out_dtype, name):
    cw = 256
    nj = HY_W // cw
    mt = n // tmi
    nyq_blk = n // 8
    kern = functools.partial(_hinv_kernel, n=n, tc=min(n, 256))
    return pl.pallas_call(
        kern,
        grid=(B, nj, mt),
        in_specs=[pl.BlockSpec((None, 2 * n, cw), lambda b, j, i: (b, 0, j)),
                  pl.BlockSpec((n, cw), lambda b, j, i: (0, order * 2 * nj + j)),
                  pl.BlockSpec((n, cw), lambda b, j, i: (1, order * 2 * nj + nj + j)),
                  pl.BlockSpec((8, cw), lambda b, j, i: (nyq_blk, order * 2 * nj + j)),
                  pl.BlockSpec((tmi, 2 * n), lambda b, j, i: (i, 0)),
                  pl.BlockSpec((tmi, 2 * n), lambda b, j, i: (i, 0)),
                  pl.BlockSpec((None, tmi, cw), lambda b, j, i: (b, uprev_rb + i, uprev_cb * nj + j)),
                  pl.BlockSpec((None, tmi, cw), lambda b, j, i: (b, gate_rb + i, gate_cb * nj + j)),
                  pl.BlockSpec((1, cw), lambda b, j, i: (0, j))],
        out_specs=pl.BlockSpec((None, tmi, cw), lambda b, j, i: (b, i, j)),
        out_shape=jax.ShapeDtypeStruct((B, n, HY_W), out_dtype),
        scratch_shapes=[pltpu.VMEM((2 * n, cw), BF16), pltpu.VMEM((2 * n, cw), BF16)],
        compiler_params=_cp(("parallel", "parallel", "arbitrary")),
        name=name,
    )(u_spec, kf, kf, kf, aih, ail, uprev, gate, bias.reshape(1, HY_W))


def _rope128(r, c, sa, sb):
    return r * c + pltpu.roll(r, 32, 1) * sa + pltpu.roll(r, 96, 1) * sb


def _qkv_kernel(z_ref, qg_ref, kvg_ref, wq_ref, wk_ref, wv_ref, c_ref, sa_ref, sb_ref,
                q_ref, k_ref, v_ref):
    c = c_ref[...]
    sa = sa_ref[...]
    sb = sb_ref[...]
    zq = z_ref[:, :Q_LORA]
    qn = (zq * lax.rsqrt(jnp.mean(zq * zq, axis=-1, keepdims=True) + EPS) * qg_ref[...]).astype(BF16)
    zkv = z_ref[:, Q_LORA:Q_LORA + KV_LORA]
    ckv = (zkv * lax.rsqrt(jnp.mean(zkv * zkv, axis=-1, keepdims=True) + EPS) * kvg_ref[...]).astype(BF16)
    kr = _rope128(z_ref[:, Q_LORA + KV_LORA:], c, sa, sb).astype(BF16)
    for h in range(HEADS):
        q = _dot(qn, wq_ref[:, h * HQ:(h + 1) * HQ])
        q_ref[:, h * HQ:h * HQ + NOPE] = q[:, :NOPE].astype(BF16)
        q_ref[:, h * HQ + NOPE:(h + 1) * HQ] = _rope128(q[:, NOPE:], c, sa, sb).astype(BF16)
        k_ref[:, h * HQ:h * HQ + NOPE] = _dot(ckv, wk_ref[:, h * NOPE:(h + 1) * NOPE]).astype(BF16)
        k_ref[:, h * HQ + NOPE:(h + 1) * HQ] = kr
    v_ref[...] = _dot(ckv, wv_ref[...]).astype(BF16)


def mla_qkv(z, q_g, kv_g, wq, wk, wv, ctab, satab, sbtab, tm):
    full = lambda shape: pl.BlockSpec(shape, lambda b, i: (0,) * len(shape))
    tab = pl.BlockSpec((tm, 128), lambda b, i: (i, 0))
    return pl.pallas_call(
        _qkv_kernel,
        grid=(B, T // tm),
        in_specs=[pl.BlockSpec((None, tm, ODD_IN_P), lambda b, i: (b, i, 0)),
                  full((1, Q_LORA)), full((1, KV_LORA)),
                  full((Q_LORA, HEADS * HQ)), full((KV_LORA, HEADS * NOPE)), full((KV_LORA, HEADS * VH)),
                  tab, tab, tab],
        out_specs=[pl.BlockSpec((None, tm, HEADS * HQ), lambda b, i: (b, i, 0)),
                   pl.BlockSpec((None, tm, HEADS * HQ), lambda b, i: (b, i, 0)),
                   pl.BlockSpec((None, tm, HEADS * VH), lambda b, i: (b, i, 0))],
        out_shape=[jax.ShapeDtypeStruct((B, T, HEADS * HQ), BF16),
                   jax.ShapeDtypeStruct((B, T, HEADS * HQ), BF16),
                   jax.ShapeDtypeStruct((B, T, HEADS * VH), BF16)],
        compiler_params=_cp(("parallel", "parallel")),
        name="mla_qkv",
    )(z, q_g.reshape(1, Q_LORA), kv_g.reshape(1, KV_LORA), wq, wk, wv, ctab, satab, sbtab)


def _attn_kernel(q_ref, k_ref, v_ref, o_ref, *, scale):
    s = lax.dot_general(q_ref[...], k_ref[...], (((1,), (1,)), ((), ())),
                        preferred_element_type=F32) * scale
    m = jnp.max(s, axis=-1, keepdims=True)
    p = jnp.exp(s - m)
    l = jnp.sum(p, axis=-1, keepdims=True)
    o = _dot(p.astype(BF16), v_ref[...])
    o_ref[...] = (o / l).astype(o_ref.dtype)


def mla_attention(q, k, v, tq):
    kern = functools.partial(_attn_kernel, scale=(NOPE + ROPE) ** -0.5)
    return pl.pallas_call(
        kern,
        grid=(B, HEADS, S // tq),
        in_specs=[pl.BlockSpec((None, tq, HQ), lambda b, h, i: (b, i, h)),
                  pl.BlockSpec((None, T, HQ), lambda b, h, i: (b, 0, h)),
                  pl.BlockSpec((None, T, VH), lambda b, h, i: (b, 0, h))],
        out_specs=pl.BlockSpec((None, tq, VH), lambda b, h, i: (b, i, h)),
        out_shape=jax.ShapeDtypeStruct((B, S, HEADS * VH), BF16),
        compiler_params=_cp(("parallel", "parallel", "parallel")),
        name="mla_attention",
    )(q, k, v)


@functools.lru_cache(maxsize=None)
def _dft_tables(n):
    m = 2 * n
    j = np.arange(n, dtype=np.int64)
    kk = np.arange(n, dtype=np.int64)
    ang = 2.0 * np.pi * ((kk[:, None] * j[None, :]) % m).astype(np.float64) / m
    cosm = np.cos(ang)
    sinm = np.sin(ang)
    fwd = np.concatenate([cosm, -sinm], axis=0)
    fwd[n, :] = np.where(j % 2 == 0, 1.0, -1.0)
    inv = np.concatenate([2.0 * cosm.T, -2.0 * sinm.T], axis=1) / m
    inv[:, 0] = 1.0 / m
    inv[:, n] = np.where(j % 2 == 0, 1.0, -1.0) / m

    def hl(a):
        a32 = a.astype(np.float32)
        hi = a32.astype(BF16)
        lo = (a32 - hi.astype(np.float32)).astype(BF16)
        return hi, lo

    return hl(fwd), hl(inv)


def _filter_consts(n):
    pos = jnp.arange(n, dtype=F32)
    t = jnp.linspace(0.0, 1.0, n, dtype=F32)[:, None]
    bands = (HY_EMB - 1) // 2
    w = 2.0 * math.pi * pos / n
    f = jnp.linspace(1e-4, bands - 1, bands, dtype=F32)
    ang = w[:, None] * f[None, :]
    z = jnp.concatenate([t, jnp.cos(ang), -jnp.sin(ang)], axis=-1)
    z = jnp.pad(z, ((0, 0), (0, 128 - HY_EMB)))
    max_decay = math.log(1e-2) / 0.3
    min_decay = math.log(1e-2) / 1.5
    deltas = jnp.linspace(min_decay, max_decay, HY_W, dtype=F32)
    decay = jnp.exp(-t * jnp.abs(deltas)[None, :])
    return z, decay


def _rope_tables():
    rows = S // 64
    row = jnp.repeat(jnp.arange(rows, dtype=F32), 64)
    col = jnp.tile(jnp.arange(64, dtype=F32), rows)
    n_freq = ROPE // 4
    inv = 10000.0 ** (-jnp.arange(n_freq, dtype=F32) / n_freq)
    ang = jnp.concatenate([row[:, None] * inv[None, :], col[:, None] * inv[None, :]], axis=-1)
    cos = jnp.concatenate([jnp.cos(ang), jnp.ones((C, 32), F32)], axis=0)
    sin = jnp.concatenate([jnp.sin(ang), jnp.zeros((C, 32), F32)], axis=0)
    z32 = jnp.zeros((T, 32), F32)
    z64 = jnp.zeros((T, 64), F32)
    ctab = jnp.concatenate([cos, cos, z64], axis=-1)
    satab = jnp.concatenate([z32, sin, z64], axis=-1)
    sbtab = jnp.concatenate([-sin, z32, z64], axis=-1)
    return ctab, satab, sbtab


def _even_layer(xs, mod, norm1_g, w_in, rg_conv_w, rg_conv_b, rg_wa, rg_ba, rg_wx, rg_bx, rg_lambda,
                hy_conv_w, hy_conv_b, f_w1, f_b1, f_freq, f_w2, f_b2, f_w3, hy_bias, w_out):
    z = norm_mod_matmul(xs, norm1_g, mod, 0, 1, w_in.astype(BF16), F32, 768, S, "even_in_proj")

    def blockdiag(w):
        out = jnp.zeros((2, RGH, RGH), F32)
        for h in range(RG_HEADS):
            g, l = divmod(h, RG_HEADS // 2)
            out = out.at[g, l * RG_HD:(l + 1) * RG_HD, l * RG_HD:(l + 1) * RG_HD].set(w[h])
        return out

    wbd = jnp.concatenate([blockdiag(rg_wa[0]), blockdiag(rg_wx[0]), blockdiag(rg_wa[1]), blockdiag(rg_wx[1])],
                          axis=-1).astype(BF16)
    bias4 = jnp.stack([rg_ba[0], rg_bx[0], rg_ba[1], rg_bx[1]], axis=0)
    a = rglru(z, rg_conv_w, rg_conv_b, wbd, bias4, softplus_neg(rg_lambda))

    vc = hyena_short_conv(z, hy_conv_w, hy_conv_b)
    w1p = jnp.pad(f_w1, ((0, 128 - HY_EMB), (0, 0)))
    outs = []
    for n, rb in ((S, 0), (C, S // C)):
        (fh, fl), (ih, il) = _dft_tables(n)
        zemb, decay = _filter_consts(n)
        xf = hyena_filter_halves(n, zemb, w1p, f_b1, f_freq, f_w2, f_b2, f_w3, decay)
        tmf = min(2 * n, 512)
        tmi = min(n, 512)
        kf = const_matmul(fh, fl, xf[None], n, 0, 4 * HY_W, 512, tmf, f"filt_dft_{n}")[0]
        u1s = const_matmul(fh, fl, vc, n, rb, HY_W, 512, tmf, f"dft_fwd1_{n}")
        u1 = hyena_inverse(u1s, kf, 0, ih, il, vc, rb * (n // tmi), 0, vc, rb * (n // tmi), 1,
                           hy_bias[0], n, tmi, F32, f"hyena_inv1_{n}")
        u2s = const_matmul(fh, fl, u1, n, 0, HY_W, 512, tmf, f"dft_fwd2_{n}")
        outs.append(hyena_inverse(u2s, kf, 1, ih, il, u1, 0, 0, vc, rb * (n // tmi), 2,
                                  hy_bias[1], n, tmi, BF16, f"hyena_inv2_{n}"))
    bh = jnp.concatenate(outs, axis=1)
    return matmul_residual([a, bh], w_out.astype(BF16), xs, mod, 2, T, 768, S, "even_out_proj")


def _odd_layer(xs, mod, norm1_g, w_in, q_g, kv_g, w_uq, w_ukv, w_o):
    perm = np.concatenate([np.arange(0, ROPE, 2), np.arange(1, ROPE, 2)])
    w_rope = jnp.pad(w_in[:, Q_LORA + KV_LORA:][:, perm], ((0, 0), (0, 128 - ROPE)))
    w_in_p = jnp.concatenate([w_in[:, :Q_LORA + KV_LORA], w_rope], axis=-1).astype(BF16)
    z = norm_mod_matmul(xs, norm1_g, mod, 0, 1, w_in_p, F32, 768, S, "odd_in_proj")

    wq = w_uq.reshape(Q_LORA, HEADS, NOPE + ROPE)
    wq = jnp.concatenate([wq[..., :NOPE], wq[..., NOPE:][..., perm],
                          jnp.zeros((Q_LORA, HEADS, HQ - NOPE - ROPE), F32)], axis=-1)
    wq = wq.reshape(Q_LORA, HEADS * HQ).astype(BF16)
    wkv = w_ukv.reshape(KV_LORA, HEADS, NOPE + VH)
    wk = wkv[..., :NOPE].reshape(KV_LORA, HEADS * NOPE).astype(BF16)
    wv = wkv[..., NOPE:].reshape(KV_LORA, HEADS * VH).astype(BF16)
    ctab, satab, sbtab = _rope_tables()
    q, k, v = mla_qkv(z, q_g, kv_g, wq, wk, wv, ctab, satab, sbtab, 768)
    o = mla_attention(q, k, v, 256)
    return matmul_residual([o], w_o.astype(BF16), xs, mod, 2, S, 512, None, "odd_out_proj")


def _ffn(xs, mod, norm2_g, w_up, conv_w, conv_b, w_down, final_g, tm, seg_starts, ctx_start, final, tag):
    u = norm_mod_matmul(xs, norm2_g, mod, 3, 4, w_up.astype(BF16), BF16, tm, ctx_start, f"ffn_up_{tag}")
    return ffn_second_half(u, conv_w, conv_b, w_down.astype(BF16), xs, mod, final_g, tm, seg_starts,
                           ctx_start, final, f"ffn_down_{tag}")


def kernel(x, c, ctx, c_ctx, ada_w, ada_b, norm1_g, norm2_g, ev_w_in, ev_rg_conv_w, ev_rg_conv_b, ev_rg_wa, ev_rg_ba, ev_rg_wx, ev_rg_bx, ev_rg_lambda, ev_hy_conv_w, ev_hy_conv_b, ev_hy_f_w1, ev_hy_f_b1, ev_hy_f_freq, ev_hy_f_w2, ev_hy_f_b2, ev_hy_f_w3, ev_hy_bias, ev_w_out, od_w_in, od_q_norm_g, od_kv_norm_g, od_w_uq, od_w_ukv, od_w_o, ffn_w_up, ffn_conv_w, ffn_conv_b, ffn_w_down, final_g):
    cvec = jnp.concatenate([c, c_ctx[None], jnp.zeros((16 - B - 1, D), F32)], axis=0)
    mod = modulation(cvec, ada_w, ada_b)

    def mod_sel(layer):
        lat = mod[layer, :B].reshape(B, 1, 6, D)
        cx = jnp.broadcast_to(mod[layer, B].reshape(1, 1, 6, D), (B, 1, 6, D))
        return jnp.concatenate([lat, cx], axis=1)

    xs = jnp.concatenate([x, ctx], axis=1)
    m0 = mod_sel(0)
    xs = _even_layer(xs, m0, norm1_g[0], ev_w_in[0], ev_rg_conv_w[0], ev_rg_conv_b[0], ev_rg_wa[0], ev_rg_ba[0],
                     ev_rg_wx[0], ev_rg_bx[0], ev_rg_lambda[0], ev_hy_conv_w[0], ev_hy_conv_b[0], ev_hy_f_w1[0],
                     ev_hy_f_b1[0], ev_hy_f_freq[0], ev_hy_f_w2[0], ev_hy_f_b2[0], ev_hy_f_w3[0], ev_hy_bias[0],
                     ev_w_out[0])
    xs = _ffn(xs, m0, norm2_g[0], ffn_w_up[0], ffn_conv_w[0], ffn_conv_b[0], ffn_w_down[0], final_g,
              384, (0, S), S, False, "0")
    m1 = mod_sel(1)
    xl = _odd_layer(xs, m1, norm1_g[1], od_w_in[0], od_q_norm_g[0], od_kv_norm_g[0], od_w_uq[0], od_w_ukv[0],
                    od_w_o[0])
    return _ffn(xl, m1, norm2_g[1], ffn_w_up[1], ffn_conv_w[1], ffn_conv_b[1], ffn_w_down[1], final_g,
                512, (0,), None, True, "1")
```

```python
import functools
import math

import numpy as np
import jax
import jax.numpy as jnp
from jax import lax
from jax.experimental import pallas as pl
from jax.experimental.pallas import tpu as pltpu

F32 = jnp.float32
BF16 = jnp.bfloat16

D = 1024
B = 8
S = 2048
C = 256
T = S + C
EPS = 1e-6

RG_W = 512
RG_HEADS = 8
RG_HD = 64
RG_C = 8.0
HY_W = 512
HY_EMB = 33
HY_HID = 64
EVEN_IN = 2560

HEADS = 8
Q_LORA = 512
KV_LORA = 256
NOPE = 128
ROPE = 64
VH = 128
HQ = 256
ODD_IN_P = 896
D_FF = 2816

VMEM_LIMIT = 52 * 1024 * 1024


def _cp(sem):
    return pltpu.CompilerParams(dimension_semantics=sem, vmem_limit_bytes=VMEM_LIMIT)


def _dot(a, b):
    return jnp.dot(a, b, preferred_element_type=F32)


def _split(x):
    hi = x.astype(BF16)
    lo = (x - hi.astype(F32)).astype(BF16)
    return hi, lo


def _dot3(ah, al, bh, bl):
    return _dot(ah, bh) + (_dot(ah, bl) + _dot(al, bh))


def _sigmoid(x):
    return 1.0 / (1.0 + jnp.exp2(x * (-math.log2(math.e))))


def _mod_rows(m_ref, k, row, ctx_start):
    lat = m_ref[0, k:k + 1, :]
    if ctx_start is None:
        return lat
    return jnp.where(row >= ctx_start, m_ref[1, k:k + 1, :], lat)


def _strided_rows(ref3, start, stride):
    return jnp.concatenate([ref3[cb, pl.ds(start, 8, stride=stride), :] for cb in range(ref3.shape[0])],
                           axis=-1)


def _perm_rows(ref3, base, nrows, r):
    return _strided_rows(ref3, base + r, nrows // 8)


def _unperm_rows(ref3, base, nrows, q):
    s, r0 = divmod(8 * q, nrows // 8)
    return _strided_rows(ref3, base + 8 * r0 + s, 8)


def _mod_kernel(c_ref, w_ref, b_ref, o_ref):
    c = c_ref[...]
    s = (c * _sigmoid(c)).astype(BF16)
    o_ref[...] = _dot(s, w_ref[...].astype(BF16)) + b_ref[...]


def modulation(cvec, ada_w, ada_b):
    depth = ada_w.shape[0]
    tn = 1024
    return pl.pallas_call(
        _mod_kernel,
        grid=(depth, 6 * D // tn),
        in_specs=[pl.BlockSpec((16, D), lambda l, j: (0, 0)),
                  pl.BlockSpec((None, D, tn), lambda l, j: (l, 0, j)),
                  pl.BlockSpec((None, 1, tn), lambda l, j: (l, 0, j))],
        out_specs=pl.BlockSpec((None, 16, tn), lambda l, j: (l, 0, j)),
        out_shape=jax.ShapeDtypeStruct((depth, 16, 6 * D), F32),
        compiler_params=_cp(("parallel", "parallel")),
        name="modulation",
    )(cvec, ada_w, ada_b.reshape(depth, 1, 6 * D))


def _nmm_kernel(x_ref, g_ref, m_ref, w_ref, o_ref, h_ref, *x3_ref, tm, ish, isc, ctx_start, nchunk, sb):
    i = pl.program_id(1)
    if sb is not None:
        x3_ref = x3_ref[0]
        for cb in range(D // 128):
            x3_ref[cb] = x_ref[:, cb * 128:(cb + 1) * 128]
    if sb is None:
        x = x_ref[...]
        ms = jnp.mean(x * x, axis=-1, keepdims=True)
        y = x * lax.rsqrt(ms + EPS) * g_ref[...]
        row = i * tm + lax.broadcasted_iota(jnp.int32, (tm, 1), 0)
        sc = _mod_rows(m_ref, isc, row, ctx_start)
        sh = _mod_rows(m_ref, ish, row, ctx_start)
        h_ref[...] = (y * (1.0 + sc) + sh).astype(BF16)
    else:
        g = g_ref[...]
        for j in range(tm // sb):
            row = i * tm + j * sb
            sc = 1.0 + _mod_rows(m_ref, isc, row, ctx_start)
            sh = _mod_rows(m_ref, ish, row, ctx_start)
            for r in range(0, sb // 8, 2):
                x = jnp.concatenate([_perm_rows(x3_ref, j * sb, sb, r), _perm_rows(x3_ref, j * sb, sb, r + 1)],
                                    axis=0)
                ms = jnp.mean(x * x, axis=-1, keepdims=True)
                y = x * lax.rsqrt(ms + EPS) * g
                h_ref[j * sb + 8 * r:j * sb + 8 * r + 16, :] = (y * sc + sh).astype(BF16)
    n = o_ref.shape[-1]
    for n0 in range(0, n, nchunk):
        n1 = min(n, n0 + nchunk)
        o_ref[:, n0:n1] = _dot(h_ref[...], w_ref[:, n0:n1]).astype(o_ref.dtype)


def norm_mod_matmul(xs, g, mod, ish, isc, w, out_dtype, tm, ctx_start, name, sb=None):
    bsz, t, _ = xs.shape
    n = w.shape[1]
    kern = functools.partial(_nmm_kernel, tm=tm, ish=ish, isc=isc, ctx_start=ctx_start, nchunk=512, sb=sb)
    return pl.pallas_call(
        kern,
        grid=(bsz, t // tm),
        in_specs=[pl.BlockSpec((None, tm, D), lambda b, i: (b, i, 0)),
                  pl.BlockSpec((1, D), lambda b, i: (0, 0)),
                  pl.BlockSpec((None, 2, 6, D), lambda b, i: (b, 0, 0, 0)),
                  pl.BlockSpec((D, n), lambda b, i: (0, 0))],
        out_specs=pl.BlockSpec((None, tm, n), lambda b, i: (b, i, 0)),
        out_shape=jax.ShapeDtypeStruct((bsz, t, n), out_dtype),
        scratch_shapes=[pltpu.VMEM((tm, D), BF16)] + ([] if sb is None else [pltpu.VMEM((D // 128, tm, 128), F32)]),
        compiler_params=_cp(("parallel", "parallel")),
        name=name,
    )(xs, g.reshape(1, D), mod, w)


def _mmres_kernel(*refs, nparts, tm, ig, ctx_start):
    parts = refs[:nparts]
    w_ref, x_ref, m_ref, o_ref = refs[nparts:nparts + 4]
    acc = None
    k0 = 0
    for p in parts:
        kp = p.shape[-1]
        d = _dot(p[...], w_ref[k0:k0 + kp, :])
        acc = d if acc is None else acc + d
        k0 += kp
    row = pl.program_id(1) * tm + lax.broadcasted_iota(jnp.int32, (tm, 1), 0)
    gate = _mod_rows(m_ref, ig, row, ctx_start)
    o_ref[...] = x_ref[...] + gate * acc


def matmul_residual(parts, w, xs, mod, ig, t_out, tm, ctx_start, name):
    bsz = xs.shape[0]
    kern = functools.partial(_mmres_kernel, nparts=len(parts), tm=tm, ig=ig, ctx_start=ctx_start)
    in_specs = [pl.BlockSpec((None, tm, p.shape[-1]), lambda b, i: (b, i, 0)) for p in parts]
    in_specs += [pl.BlockSpec(w.shape, lambda b, i: (0, 0)),
                 pl.BlockSpec((None, tm, D), lambda b, i: (b, i, 0)),
                 pl.BlockSpec((None, 2, 6, D), lambda b, i: (b, 0, 0, 0))]
    return pl.pallas_call(
        kern,
        grid=(bsz, t_out // tm),
        in_specs=in_specs,
        out_specs=pl.BlockSpec((None, tm, D), lambda b, i: (b, i, 0)),
        out_shape=jax.ShapeDtypeStruct((bsz, t_out, D), F32),
        compiler_params=_cp(("parallel", "parallel")),
        name=name,
    )(*parts, w, xs, mod)


def _ffn2_kernel(u_ref, up_ref, un_ref, cw_ref, cb_ref, wd_ref, x_ref, m_ref, fg_ref, o_ref, acc_ref,
                 *, tm, t, sb, seg_starts, ctx_start, final, ck):
    i = pl.program_id(1)
    nsub = tm // sb
    sub8 = lax.broadcasted_iota(jnp.int32, (8, 1), 0)
    keep_prev = []
    keep_next = []
    for j in range(nsub):
        row0 = i * tm + j * sb
        kp = row0 != seg_starts[0]
        kn = row0 + sb != t
        for s0 in seg_starts[1:]:
            kp = kp & (row0 != s0)
            kn = kn & (row0 + sb != s0)
        keep_prev.append(jnp.where(kp, 1.0, 0.0))
        keep_next.append(jnp.where(kn, 1.0, 0.0))

    def conv(j, c0):
        r0 = j * sb
        cur = u_ref[r0:r0 + sb, c0:c0 + ck].astype(F32)
        hp = up_ref[15:16, c0:c0 + ck] if j == 0 else u_ref[r0 - 1:r0, c0:c0 + ck]
        hn = un_ref[0:1, c0:c0 + ck] if j == nsub - 1 else u_ref[r0 + sb:r0 + sb + 1, c0:c0 + ck]
        hp = hp.astype(F32) * keep_prev[j]
        hn = hn.astype(F32) * keep_next[j]
        p0 = jnp.where(sub8 == 0, hp, pltpu.roll(cur[sb - 8:sb], 1, 0))
        nl = jnp.where(sub8 == 7, hn, pltpu.roll(cur[0:8], 7, 0))
        prev = jnp.concatenate([p0, cur[:sb - 8]], axis=0)
        nxt = jnp.concatenate([cur[8:], nl], axis=0)
        w = cw_ref[:, c0:c0 + ck]
        return prev * w[0:1] + cur * w[1:2] + nxt * w[2:3] + cb_ref[:, c0:c0 + ck]

    for k, c0 in enumerate(range(0, D_FF, ck)):
        parts = []
        for j in range(nsub):
            g = conv(j, c0)
            v = conv(j, D_FF + c0)
            parts.append((g * _sigmoid(g) * v).astype(BF16))
        a = parts[0] if nsub == 1 else jnp.concatenate(parts, axis=0)
        d = _dot(a, wd_ref[c0:c0 + ck, :])
        for cb in range(D // 128):
            if k == 0:
                acc_ref[cb] = d[:, cb * 128:(cb + 1) * 128]
            else:
                acc_ref[cb] += d[:, cb * 128:(cb + 1) * 128]
    fg = fg_ref[...]
    for j in range(nsub):
        gate = _mod_rows(m_ref, 5, i * tm + j * sb, ctx_start)
        for q in range(sb // 8):
            rows = slice(j * sb + 8 * q, j * sb + 8 * q + 8)
            out = x_ref[rows, :] + gate * _unperm_rows(acc_ref, j * sb, sb, q)
            if final:
                ms = jnp.mean(out * out, axis=-1, keepdims=True)
                out = out * lax.rsqrt(ms + EPS) * fg
            o_ref[rows, :] = out


def ffn_second_half(u, conv_w, conv_b, w_down, xs, mod, final_g, tm, sb, seg_starts, ctx_start, final, name):
    bsz, t, f2 = u.shape
    hb = 16
    nhb = t // hb
    r = tm // hb
    kern = functools.partial(_ffn2_kernel, tm=tm, t=t, sb=sb, seg_starts=seg_starts, ctx_start=ctx_start,
                             final=final, ck=256)
    return pl.pallas_call(
        kern,
        grid=(bsz, t // tm),
        in_specs=[pl.BlockSpec((None, tm, f2), lambda b, i: (b, i, 0)),
                  pl.BlockSpec((None, hb, f2), lambda b, i: (b, jnp.maximum(i * r - 1, 0), 0)),
                  pl.BlockSpec((None, hb, f2), lambda b, i: (b, jnp.minimum((i + 1) * r, nhb - 1), 0)),
                  pl.BlockSpec((3, f2), lambda b, i: (0, 0)),
                  pl.BlockSpec((1, f2), lambda b, i: (0, 0)),
                  pl.BlockSpec((D_FF, D), lambda b, i: (0, 0)),
                  pl.BlockSpec((None, tm, D), lambda b, i: (b, i, 0)),
                  pl.BlockSpec((None, 2, 6, D), lambda b, i: (b, 0, 0, 0)),
                  pl.BlockSpec((1, D), lambda b, i: (0, 0))],
        out_specs=pl.BlockSpec((None, tm, D), lambda b, i: (b, i, 0)),
        out_shape=jax.ShapeDtypeStruct((bsz, t, D), F32),
        scratch_shapes=[pltpu.VMEM((D // 128, tm, 128), F32)],
        compiler_params=_cp(("parallel", "parallel")),
        name=name,
    )(u, u, u, conv_w, conv_b.reshape(1, f2), w_down, xs, mod, final_g.reshape(1, D))


PAD = 8


def _fill_padded(xp_ref, x_ref, width):
    z = jnp.zeros((PAD, width), F32)
    xp_ref[0:PAD, :] = z
    xp_ref[PAD:PAD + S, :] = x_ref[0:S, :].astype(F32)
    xp_ref[PAD + S:2 * PAD + S, :] = z
    xp_ref[2 * PAD + S:2 * PAD + T, :] = x_ref[S:T, :].astype(F32)
    xp_ref[2 * PAD + T:3 * PAD + T, :] = z


def _conv_tile(xp_ref, w, bias, r0, rows, ktaps):
    base = PAD + r0 if r0 < S else 2 * PAD + r0
    left = ktaps // 2
    acc = bias
    for j in range(ktaps):
        acc = acc + xp_ref[base + j - left:base + j - left + rows, :] * w[j:j + 1, :]
    return acc


def _hconv_kernel(z_ref, w_ref, b_ref, o_ref, xp_ref):
    _fill_padded(xp_ref, z_ref, 512)
    w = w_ref[...]
    bias = b_ref[...]
    for r0 in range(0, T, 256):
        o_ref[r0:r0 + 256, :] = _conv_tile(xp_ref, w, bias, r0, 256, 3)


def hyena_short_conv(z, conv_w, conv_b):
    return pl.pallas_call(
        _hconv_kernel,
        grid=(B, 3),
        in_specs=[pl.BlockSpec((None, T, 512), lambda b, j: (b, 0, 2 + j)),
                  pl.BlockSpec((3, 512), lambda b, j: (0, j)),
                  pl.BlockSpec((1, 512), lambda b, j: (0, j))],
        out_specs=pl.BlockSpec((None, T, 512), lambda b, j: (b, 0, j)),
        out_shape=jax.ShapeDtypeStruct((B, T, 3 * HY_W), F32),
        scratch_shapes=[pltpu.VMEM((T + 3 * PAD, 512), F32)],
        compiler_params=_cp(("parallel", "parallel")),
        name="hyena_short_conv",
    )(z, conv_w, conv_b.reshape(1, 3 * HY_W))


RGH = 128
SEGS = ((S, C, C // 8), (0, S, S // 8))
POFF = {S: 0, 0: C + 24}


def _rg_kernel(x_ref, gt_ref, cw_ref, cb_ref, wbd_ref, bias_ref, sp_ref, o_ref,
               xp_ref, af_ref, bf_ref, ar_ref, br_ref, cf_ref, cr_ref):
    sub8 = lax.broadcasted_iota(jnp.int32, (8, 1), 0)
    cw = cw_ref[...]
    cb = cb_ref[...]
    nl2e8 = -RG_C * math.log2(math.e)
    spf = sp_ref[0:1, :] * nl2e8
    spr = sp_ref[1:2, :] * nl2e8

    for seg0, rows, nslot in SEGS:
        p0 = POFF[seg0]

        def fill(r, _):
            xp_ref[pl.ds(pl.multiple_of(p0 + 16 + 8 * r, 8), 8), :] = (
                x_ref[pl.ds(seg0 + r, 8, stride=nslot), :].astype(F32))
            return 0

        lax.fori_loop(0, nslot, fill, 0, unroll=8)
        for slot, src, sh in ((0, nslot - 2, 1), (1, nslot - 1, 1), (nslot + 2, 0, 7)):
            v = pltpu.roll(xp_ref[p0 + 16 + 8 * src:p0 + 24 + 8 * src, :], sh, 0)
            xp_ref[p0 + 8 * slot:p0 + 8 * slot + 8, :] = jnp.where(sub8 == (0 if sh == 1 else 7), 0.0, v)

        for t0 in range(0, rows, 256):
            xc = cb
            for j in range(4):
                xc = xc + xp_ref[p0 + 8 * j + t0:p0 + 8 * j + t0 + 256, :] * cw[j:j + 1, :]
            g = _dot(xc.astype(BF16), wbd_ref[...])
            for d, (a_ref, b_ref, sp) in enumerate(((af_ref, bf_ref, spf), (ar_ref, br_ref, spr))):
                rg = _sigmoid(g[:, (2 * d) * RGH:(2 * d + 1) * RGH] + bias_ref[2 * d:2 * d + 1, :])
                ig = _sigmoid(g[:, (2 * d + 1) * RGH:(2 * d + 2) * RGH] + bias_ref[2 * d + 1:2 * d + 2, :])
                a = jnp.exp2(rg * sp)
                a_ref[seg0 + t0:seg0 + t0 + 256, :] = a
                om = 1.0 - a * a
                root = jnp.where(om > 0.0, om * lax.rsqrt(om), 0.0)
                b_ref[seg0 + t0:seg0 + t0 + 256, :] = root * (ig * xc)

    one = jnp.ones((8, RGH), F32)
    zero = jnp.zeros((8, RGH), F32)
    for seg0, rows, nslot in SEGS:
        def body(i, carry):
            pf, hf, pr, hr = carry
            tf = pl.multiple_of(seg0 + 8 * i, 8)
            tr = pl.multiple_of(seg0 + 8 * (nslot - 1 - i), 8)
            a = af_ref[pl.ds(tf, 8), :]
            hf = a * hf + bf_ref[pl.ds(tf, 8), :]
            pf = a * pf
            af_ref[pl.ds(tf, 8), :] = pf
            bf_ref[pl.ds(tf, 8), :] = hf
            a = ar_ref[pl.ds(tr, 8), :]
            hr = a * hr + br_ref[pl.ds(tr, 8), :]
            pr = a * pr
            ar_ref[pl.ds(tr, 8), :] = pr
            br_ref[pl.ds(tr, 8), :] = hr
            return pf, hf, pr, hr

        lax.fori_loop(0, nslot, body, (one, zero, one, zero), unroll=4)

    cf = jnp.zeros((1, RGH), F32)
    cr = jnp.zeros((1, RGH), F32)
    for k, (seg0, rows, nslot) in enumerate(SEGS):
        last = seg0 + 8 * (nslot - 1)
        for s in range(8):
            cf_ref[8 * k + s:8 * k + s + 1, :] = cf
            cf = bf_ref[last + s:last + s + 1, :] + af_ref[last + s:last + s + 1, :] * cf
        for s in range(7, -1, -1):
            cr_ref[8 * k + s:8 * k + s + 1, :] = cr
            cr = br_ref[seg0 + s:seg0 + s + 1, :] + ar_ref[seg0 + s:seg0 + s + 1, :] * cr

    for k, (seg0, rows, nslot) in enumerate(SEGS):
        reps = min(rows, 256) // 8
        cft = jnp.concatenate([cf_ref[8 * k:8 * k + 8, :]] * reps, axis=0)
        crt = jnp.concatenate([cr_ref[8 * k:8 * k + 8, :]] * reps, axis=0)
        for t0 in range(0, rows, 256):
            sl = slice(seg0 + t0, seg0 + t0 + 256)
            bf_ref[sl, :] = (bf_ref[sl, :] + af_ref[sl, :] * cft) + (br_ref[sl, :] + ar_ref[sl, :] * crt)

    for seg0, rows, nslot in SEGS:
        per = nslot // 8

        def unperm(q, _):
            s = lax.shift_right_logical(q, per.bit_length() - 1)
            r0 = q & (per - 1)
            af_ref[pl.ds(pl.multiple_of(seg0 + 8 * q, 8), 8), :] = (
                bf_ref[pl.ds(seg0 + 64 * r0 + s, 8, stride=8), :])
            return 0

        lax.fori_loop(0, rows // 8, unperm, 0, unroll=8)

    c0 = math.sqrt(2.0 / math.pi)
    for r0 in range(0, T, 256):
        gt = gt_ref[r0:r0 + 256, :].astype(F32)
        gelu = 0.5 * gt * (1.0 + jnp.tanh(c0 * (gt + 0.044715 * (gt * gt * gt))))
        o_ref[r0:r0 + 256, :] = (af_ref[r0:r0 + 256, :] * gelu).astype(o_ref.dtype)


def rglru(z, conv_w, conv_b, wbd, bias4, softplus_neg_lam):
    nh = RG_W // RGH
    return pl.pallas_call(
        _rg_kernel,
        grid=(B, nh),
        in_specs=[pl.BlockSpec((None, T, RGH), lambda b, j: (b, 0, j)),
                  pl.BlockSpec((None, T, RGH), lambda b, j: (b, 0, nh + j)),
                  pl.BlockSpec((4, RGH), lambda b, j: (0, j)),
                  pl.BlockSpec((1, RGH), lambda b, j: (0, j)),
                  pl.BlockSpec((None, RGH, 4 * RGH), lambda b, j: (j, 0, 0)),
                  pl.BlockSpec((4, RGH), lambda b, j: (0, j)),
                  pl.BlockSpec((2, RGH), lambda b, j: (0, j))],
        out_specs=pl.BlockSpec((None, T, RGH), lambda b, j: (b, 0, j)),
        out_shape=jax.ShapeDtypeStruct((B, T, RG_W), BF16),
        scratch_shapes=[pltpu.VMEM((T + 48, RGH), F32)] + [pltpu.VMEM((T, RGH), F32)] * 4
                       + [pltpu.VMEM((16, RGH), F32)] * 2,
        compiler_params=_cp(("parallel", "parallel")),
        name="rglru",
    )(z, z, conv_w, conv_b.reshape(1, RG_W), wbd, bias4, softplus_neg_lam)


def _softplus_kernel(x_ref, o_ref):
    y = -x_ref[...]
    o_ref[...] = jnp.maximum(y, 0.0) + jnp.log(1.0 + jnp.exp(-jnp.abs(y)))


def softplus_neg(lam):
    return pl.pallas_call(
        _softplus_kernel,
        out_shape=jax.ShapeDtypeStruct(lam.shape, F32),
        name="softplus_neg",
    )(lam)


def _hfilt_kernel(z_ref, w1_ref, b1_ref, fr_ref, w2_ref, b2_ref, w3_ref, dec_ref, x1_ref, x2_ref, nyq_ref,
                  s1_ref, s2_ref, *, n, tr):
    fr = fr_ref[...]
    w1h, w1l = _split(w1_ref[...])
    w2h, w2l = _split(w2_ref[...])
    w3h, w3l = _split(w3_ref[...])
    tot = jnp.zeros((1, HY_W), F32)
    alt = jnp.zeros((1, HY_W), F32)
    sign = jnp.where(lax.broadcasted_iota(jnp.int32, (tr, 1), 0) % 2 == 0, 1.0, -1.0)
    for r0 in range(0, n, tr):
        zh, zl = _split(z_ref[r0:r0 + tr, :])
        h = jnp.sin(fr * (_dot3(zh, zl, w1h, w1l) + b1_ref[...]))
        hh, hl = _split(h)
        h = jnp.sin(fr * (_dot3(hh, hl, w2h, w2l) + b2_ref[...]))
        hh, hl = _split(h)
        h = _dot3(hh, hl, w3h, w3l)
        dec = dec_ref[r0:r0 + tr, :]
        h0 = h[:, :HY_W] * dec
        h1 = h[:, HY_W:] * dec
        if r0 == 0:
            h1 = jnp.where(lax.broadcasted_iota(jnp.int32, (tr, 1), 0) == 0, 0.0, h1)
        tot = tot + jnp.sum(jnp.abs(h0) + jnp.abs(h1), axis=0, keepdims=True)
        alt = alt + jnp.sum((h0 + h1) * sign, axis=0, keepdims=True)
        s1_ref[r0:r0 + tr, :] = h0 + h1
        s2_ref[r0:r0 + tr, :] = h0 - h1
    inv = 1.0 / tot
    for r0 in range(0, n, tr):
        x1_ref[r0:r0 + tr, :] = (s1_ref[r0:r0 + tr, :] * inv).astype(BF16)
        x2_ref[r0:r0 + tr, :] = (s2_ref[r0:r0 + tr, :] * inv).astype(BF16)
    nyq_ref[...] = jnp.broadcast_to(alt * inv, (8, HY_W))


def hyena_filter_halves(n, zemb, w1p, b1, freq, w2, b2, w3, decay):
    kern = functools.partial(_hfilt_kernel, n=n, tr=256)
    full = lambda shape: pl.BlockSpec(shape, lambda o: (0,) * len(shape))
    return pl.pallas_call(
        kern,
        grid=(2,),
        in_specs=[full((n, 128)), full((128, HY_HID)), full((1, HY_HID)), full((1, HY_HID)),
                  full((HY_HID, HY_HID)), full((1, HY_HID)),
                  pl.BlockSpec((HY_HID, 2 * HY_W), lambda o: (0, o)),
                  full((n, HY_W))],
        out_specs=[pl.BlockSpec((n, HY_W), lambda o: (0, o)),
                   pl.BlockSpec((n, HY_W), lambda o: (0, o)),
                   pl.BlockSpec((8, HY_W), lambda o: (0, o))],
        out_shape=[jax.ShapeDtypeStruct((n, 2 * HY_W), BF16),
                   jax.ShapeDtypeStruct((n, 2 * HY_W), BF16),
                   jax.ShapeDtypeStruct((8, 2 * HY_W), F32)],
        scratch_shapes=[pltpu.VMEM((n, HY_W), F32), pltpu.VMEM((n, HY_W), F32)],
        compiler_params=_cp(("parallel",)),
        name=f"hyena_filter_{n}",
    )(zemb, w1p, b1.reshape(1, HY_HID), freq.reshape(1, HY_HID), w2, b2.reshape(1, HY_HID), w3, decay)


def _kspec_kernel(at_ref, ab_ref, x1_ref, x2_ref, kt_ref, kb_ref):
    kt_ref[...] = _dot(at_ref[...], x1_ref[...])
    kb_ref[...] = _dot(ab_ref[...], x2_ref[...])


def filter_spectrum(fwd, x1, x2, n, tr):
    nt = n // tr
    return pl.pallas_call(
        _kspec_kernel,
        grid=(nt,),
        in_specs=[pl.BlockSpec((tr, n), lambda i: (i, 0)),
                  pl.BlockSpec((tr, n), lambda i: (nt + i, 0)),
                  pl.BlockSpec((n, 2 * HY_W), lambda i: (0, 0)),
                  pl.BlockSpec((n, 2 * HY_W), lambda i: (0, 0))],
        out_specs=[pl.BlockSpec((tr, 2 * HY_W), lambda i: (i, 0)),
                   pl.BlockSpec((tr, 2 * HY_W), lambda i: (i, 0))],
        out_shape=[jax.ShapeDtypeStruct((n, 2 * HY_W), F32)] * 2,
        compiler_params=_cp(("parallel",)),
        name=f"filter_spectrum_{n}",
    )(fwd, fwd, x1, x2)


def _hfwd_kernel(at_ref, ab_ref, x_ref, kt_ref, kb_ref, kn_ref, yr_ref, yi_ref, xb_ref, *, tr):
    i = pl.program_id(1)

    @pl.when(i == 0)
    def _():
        xb_ref[...] = x_ref[...].astype(BF16)

    xb = xb_ref[...]
    ur = _dot(at_ref[...], xb)
    ui = _dot(ab_ref[...], xb)
    kr = kt_ref[...]
    ki = kb_ref[...]
    first = (lax.broadcasted_iota(jnp.int32, (tr, 1), 0) == 0) & (i == 0)
    yr_ref[...] = (ur * kr - jnp.where(first, 0.0, ui * ki)).astype(BF16)
    yi_ref[...] = jnp.where(first, ui * kn_ref[0:1, :], ur * ki + ui * kr).astype(BF16)


def hyena_forward(fwd, x, x_rb, kt, kb, nyq, order, n, tr, name):
    nt = n // tr
    kern = functools.partial(_hfwd_kernel, tr=tr)
    return pl.pallas_call(
        kern,
        grid=(B, nt),
        in_specs=[pl.BlockSpec((tr, n), lambda b, i: (i, 0)),
                  pl.BlockSpec((tr, n), lambda b, i: (nt + i, 0)),
                  pl.BlockSpec((None, n, HY_W), lambda b, i: (b, x_rb, 0)),
                  pl.BlockSpec((tr, HY_W), lambda b, i: (i, order)),
                  pl.BlockSpec((tr, HY_W), lambda b, i: (i, order)),
                  pl.BlockSpec((8, HY_W), lambda b, i: (0, order))],
        out_specs=[pl.BlockSpec((None, tr, HY_W), lambda b, i: (b, i, 0)),
                   pl.BlockSpec((None, tr, HY_W), lambda b, i: (b, i, 0))],
        out_shape=[jax.ShapeDtypeStruct((B, n, HY_W), BF16)] * 2,
        scratch_shapes=[pltpu.VMEM((n, HY_W), BF16)],
        compiler_params=_cp(("parallel", "arbitrary")),
        name=name,
    )(fwd, fwd, x, kt, kb, nyq)


def _hinv_kernel(al_ref, ar_ref, yr_ref, yi_ref, up_ref, gt_ref, bias_ref, o_ref):
    y = _dot(al_ref[...], yr_ref[...]) + _dot(ar_ref[...], yi_ref[...])
    o_ref[...] = (gt_ref[...] * (y + up_ref[...].astype(F32) * bias_ref[...])).astype(o_ref.dtype)


def hyena_inverse(inv, yr, yi, uprev, uprev_rb, gate, gate_rb, gate_cb, bias, n, tmi, out_dtype, name):
    return pl.pallas_call(
        _hinv_kernel,
        grid=(B, n // tmi),
        in_specs=[pl.BlockSpec((tmi, n), lambda b, i: (i, 0)),
                  pl.BlockSpec((tmi, n), lambda b, i: (i, 1)),
                  pl.BlockSpec((None, n, HY_W), lambda b, i: (b, 0, 0)),
                  pl.BlockSpec((None, n, HY_W), lambda b, i: (b, 0, 0)),
                  pl.BlockSpec((None, tmi, HY_W), lambda b, i: (b, uprev_rb + i, 0)),
                  pl.BlockSpec((None, tmi, HY_W), lambda b, i: (b, gate_rb + i, gate_cb)),
                  pl.BlockSpec((1, HY_W), lambda b, i: (0, 0))],
        out_specs=pl.BlockSpec((None, tmi, HY_W), lambda b, i: (b, i, 0)),
        out_shape=jax.ShapeDtypeStruct((B, n, HY_W), out_dtype),
        compiler_params=_cp(("parallel", "parallel")),
        name=name,
    )(inv, inv, yr, yi, uprev, gate, bias.reshape(1, HY_W))


def _rope128(r, c, sa, sb):
    return r * c + pltpu.roll(r, 32, 1) * sa + pltpu.roll(r, 96, 1) * sb


def _qkv_kernel(z_ref, qg_ref, kvg_ref, wq_ref, wk_ref, wv_ref, c_ref, sa_ref, sb_ref,
                q_ref, k_ref, v_ref):
    c = c_ref[...]
    sa = sa_ref[...]
    sb = sb_ref[...]
    zq = z_ref[:, :Q_LORA]
    qn = (zq * lax.rsqrt(jnp.mean(zq * zq, axis=-1, keepdims=True) + EPS) * qg_ref[...]).astype(BF16)
    zkv = z_ref[:, Q_LORA:Q_LORA + KV_LORA]
    ckv = (zkv * lax.rsqrt(jnp.mean(zkv * zkv, axis=-1, keepdims=True) + EPS) * kvg_ref[...]).astype(BF16)
    kr = _rope128(z_ref[:, Q_LORA + KV_LORA:], c, sa, sb).astype(BF16)
    for h in range(HEADS):
        q = _dot(qn, wq_ref[:, h * HQ:(h + 1) * HQ])
        q_ref[:, h * HQ:h * HQ + NOPE] = q[:, :NOPE].astype(BF16)
        q_ref[:, h * HQ + NOPE:(h + 1) * HQ] = _rope128(q[:, NOPE:], c, sa, sb).astype(BF16)
        k_ref[:, h * HQ:h * HQ + NOPE] = _dot(ckv, wk_ref[:, h * NOPE:(h + 1) * NOPE]).astype(BF16)
        k_ref[:, h * HQ + NOPE:(h + 1) * HQ] = kr
    v_ref[...] = _dot(ckv, wv_ref[...]).astype(BF16)


def mla_qkv(z, q_g, kv_g, wq, wk, wv, ctab, satab, sbtab, tm):
    full = lambda shape: pl.BlockSpec(shape, lambda b, i: (0,) * len(shape))
    tab = pl.BlockSpec((tm, 128), lambda b, i: (i, 0))
    return pl.pallas_call(
        _qkv_kernel,
        grid=(B, T // tm),
        in_specs=[pl.BlockSpec((None, tm, ODD_IN_P), lambda b, i: (b, i, 0)),
                  full((1, Q_LORA)), full((1, KV_LORA)),
                  full((Q_LORA, HEADS * HQ)), full((KV_LORA, HEADS * NOPE)), full((KV_LORA, HEADS * VH)),
                  tab, tab, tab],
        out_specs=[pl.BlockSpec((None, tm, HEADS * HQ), lambda b, i: (b, i, 0)),
                   pl.BlockSpec((None, tm, HEADS * HQ), lambda b, i: (b, i, 0)),
                   pl.BlockSpec((None, tm, HEADS * VH), lambda b, i: (b, i, 0))],
        out_shape=[jax.ShapeDtypeStruct((B, T, HEADS * HQ), BF16),
                   jax.ShapeDtypeStruct((B, T, HEADS * HQ), BF16),
                   jax.ShapeDtypeStruct((B, T, HEADS * VH), BF16)],
        compiler_params=_cp(("parallel", "parallel")),
        name="mla_qkv",
    )(z, q_g.reshape(1, Q_LORA), kv_g.reshape(1, KV_LORA), wq, wk, wv, ctab, satab, sbtab)


def _attn_kernel(q_ref, k_ref, v_ref, o_ref, *, scale, ts):
    c = scale * math.log2(math.e)
    k = k_ref[...]
    kc = 256
    for r0 in range(0, q_ref.shape[0], ts):
        s = lax.dot_general(q_ref[r0:r0 + ts, :], k, (((1,), (1,)), ((), ())), preferred_element_type=F32)
        m = jnp.max(s, axis=-1, keepdims=True)
        acc = None
        lv = None
        for c0 in range(0, T, kc):
            p = jnp.exp2((s[:, c0:c0 + kc] - m) * c)
            pl_ = p[:, :128] + p[:, 128:]
            lv = pl_ if lv is None else lv + pl_
            d = _dot(p.astype(BF16), v_ref[c0:c0 + kc, :])
            acc = d if acc is None else acc + d
        l = jnp.sum(lv, axis=-1, keepdims=True)
        o_ref[r0:r0 + ts, :] = (acc * (1.0 / l)).astype(o_ref.dtype)


def mla_attention(q, k, v, tq):
    kern = functools.partial(_attn_kernel, scale=(NOPE + ROPE) ** -0.5, ts=512)
    return pl.pallas_call(
        kern,
        grid=(B, HEADS, S // tq),
        in_specs=[pl.BlockSpec((None, tq, HQ), lambda b, h, i: (b, i, h)),
                  pl.BlockSpec((None, T, HQ), lambda b, h, i: (b, 0, h)),
                  pl.BlockSpec((None, T, VH), lambda b, h, i: (b, 0, h))],
        out_specs=pl.BlockSpec((None, tq, VH), lambda b, h, i: (b, i, h)),
        out_shape=jax.ShapeDtypeStruct((B, S, HEADS * VH), BF16),
        compiler_params=_cp(("parallel", "parallel", "parallel")),
        name="mla_attention",
    )(q, k, v)


@functools.lru_cache(maxsize=None)
def _dft_tables(n):
    m = 2 * n
    j = np.arange(n, dtype=np.int64)
    kk = np.arange(n, dtype=np.int64)
    ang = 2.0 * np.pi * ((kk[:, None] * j[None, :]) % m).astype(np.float64) / m
    cosm = np.cos(ang)
    sinm = np.sin(ang)
    fwd = np.concatenate([cosm, -sinm], axis=0)
    fwd[n, :] = np.where(j % 2 == 0, 1.0, -1.0)
    inv = np.concatenate([2.0 * cosm.T, -2.0 * sinm.T], axis=1) / m
    inv[:, 0] = 1.0 / m
    inv[:, n] = np.where(j % 2 == 0, 1.0, -1.0) / m
    return fwd.astype(np.float32).astype(BF16), inv.astype(np.float32).astype(BF16)


def _filter_consts(n):
    pos = jnp.arange(n, dtype=F32)
    t = jnp.linspace(0.0, 1.0, n, dtype=F32)[:, None]
    bands = (HY_EMB - 1) // 2
    w = 2.0 * math.pi * pos / n
    f = jnp.linspace(1e-4, bands - 1, bands, dtype=F32)
    ang = w[:, None] * f[None, :]
    z = jnp.concatenate([t, jnp.cos(ang), -jnp.sin(ang)], axis=-1)
    z = jnp.pad(z, ((0, 0), (0, 128 - HY_EMB)))
    max_decay = math.log(1e-2) / 0.3
    min_decay = math.log(1e-2) / 1.5
    deltas = jnp.linspace(min_decay, max_decay, HY_W, dtype=F32)
    decay = jnp.exp(-t * jnp.abs(deltas)[None, :])
    return z, decay


def _rope_tables():
    rows = S // 64
    row = jnp.repeat(jnp.arange(rows, dtype=F32), 64)
    col = jnp.tile(jnp.arange(64, dtype=F32), rows)
    n_freq = ROPE // 4
    inv = 10000.0 ** (-jnp.arange(n_freq, dtype=F32) / n_freq)
    ang = jnp.concatenate([row[:, None] * inv[None, :], col[:, None] * inv[None, :]], axis=-1)
    cos = jnp.concatenate([jnp.cos(ang), jnp.ones((C, 32), F32)], axis=0)
    sin = jnp.concatenate([jnp.sin(ang), jnp.zeros((C, 32), F32)], axis=0)
    z32 = jnp.zeros((T, 32), F32)
    z64 = jnp.zeros((T, 64), F32)
    ctab = jnp.concatenate([cos, cos, z64], axis=-1)
    satab = jnp.concatenate([z32, sin, z64], axis=-1)
    sbtab = jnp.concatenate([-sin, z32, z64], axis=-1)
    return ctab, satab, sbtab


def _even_layer(xs, mod, norm1_g, w_in, rg_conv_w, rg_conv_b, rg_wa, rg_ba, rg_wx, rg_bx, rg_lambda,
                hy_conv_w, hy_conv_b, f_w1, f_b1, f_freq, f_w2, f_b2, f_w3, hy_bias, w_out):
    z = norm_mod_matmul(xs, norm1_g, mod, 0, 1, w_in.astype(BF16), F32, 768, S, "even_in_proj")

    def blockdiag(w):
        per = RGH // RG_HD
        out = jnp.zeros((RG_W // RGH, RGH, RGH), F32)
        for h in range(RG_HEADS):
            g, l = divmod(h, per)
            out = out.at[g, l * RG_HD:(l + 1) * RG_HD, l * RG_HD:(l + 1) * RG_HD].set(w[h])
        return out

    wbd = jnp.concatenate([blockdiag(rg_wa[0]), blockdiag(rg_wx[0]), blockdiag(rg_wa[1]), blockdiag(rg_wx[1])],
                          axis=-1).astype(BF16)
    bias4 = jnp.stack([rg_ba[0], rg_bx[0], rg_ba[1], rg_bx[1]], axis=0)
    a = rglru(z, rg_conv_w, rg_conv_b, wbd, bias4, softplus_neg(rg_lambda))

    vc = hyena_short_conv(z, hy_conv_w, hy_conv_b)
    w1p = jnp.pad(f_w1, ((0, 128 - HY_EMB), (0, 0)))
    outs = []
    for n, rb in ((S, 0), (C, S // C)):
        fwd, inv = _dft_tables(n)
        zemb, decay = _filter_consts(n)
        x1, x2, nyq = hyena_filter_halves(n, zemb, w1p, f_b1, f_freq, f_w2, f_b2, f_w3, decay)
        tr = 256
        tmi = min(n, 512)
        kt, kb = filter_spectrum(fwd, x1, x2, n, tr)
        grb = rb * (n // tmi)
        yr, yi = hyena_forward(fwd, vc, rb, kt, kb, nyq, 0, n, tr, f"hyena_fwd1_{n}")
        u1 = hyena_inverse(inv, yr, yi, vc, grb, vc, grb, 1, hy_bias[0], n, tmi, F32, f"hyena_inv1_{n}")
        yr, yi = hyena_forward(fwd, u1, 0, kt, kb, nyq, 1, n, tr, f"hyena_fwd2_{n}")
        outs.append(hyena_inverse(inv, yr, yi, u1, 0, vc, grb, 2, hy_bias[1], n, tmi, BF16,
                                  f"hyena_inv2_{n}"))
    bh = jnp.concatenate(outs, axis=1)
    return matmul_residual([a, bh], w_out.astype(BF16), xs, mod, 2, T, 768, S, "even_out_proj")


def _odd_layer(xs, mod, norm1_g, w_in, q_g, kv_g, w_uq, w_ukv, w_o):
    perm = np.concatenate([np.arange(0, ROPE, 2), np.arange(1, ROPE, 2)])
    w_rope = jnp.pad(w_in[:, Q_LORA + KV_LORA:][:, perm], ((0, 0), (0, 128 - ROPE)))
    w_in_p = jnp.concatenate([w_in[:, :Q_LORA + KV_LORA], w_rope], axis=-1).astype(BF16)
    z = norm_mod_matmul(xs, norm1_g, mod, 0, 1, w_in_p, F32, 768, S, "odd_in_proj")

    wq = w_uq.reshape(Q_LORA, HEADS, NOPE + ROPE)
    wq = jnp.concatenate([wq[..., :NOPE], wq[..., NOPE:][..., perm],
                          jnp.zeros((Q_LORA, HEADS, HQ - NOPE - ROPE), F32)], axis=-1)
    wq = wq.reshape(Q_LORA, HEADS * HQ).astype(BF16)
    wkv = w_ukv.reshape(KV_LORA, HEADS, NOPE + VH)
    wk = wkv[..., :NOPE].reshape(KV_LORA, HEADS * NOPE).astype(BF16)
    wv = wkv[..., NOPE:].reshape(KV_LORA, HEADS * VH).astype(BF16)
    ctab, satab, sbtab = _rope_tables()
    q, k, v = mla_qkv(z, q_g, kv_g, wq, wk, wv, ctab, satab, sbtab, 768)
    o = mla_attention(q, k, v, 1024)
    return matmul_residual([o], w_o.astype(BF16), xs, mod, 2, S, 512, None, "odd_out_proj")


def _ffn(xs, mod, norm2_g, w_up, conv_w, conv_b, w_down, final_g, tm, sb, seg_starts, ctx_start, final, tag):
    u = norm_mod_matmul(xs, norm2_g, mod, 3, 4, w_up.astype(BF16), BF16, tm, ctx_start, f"ffn_up_{tag}", sb=sb)
    return ffn_second_half(u, conv_w, conv_b, w_down.astype(BF16), xs, mod, final_g, tm, sb, seg_starts,
                           ctx_start, final, f"ffn_down_{tag}")


def kernel(x, c, ctx, c_ctx, ada_w, ada_b, norm1_g, norm2_g, ev_w_in, ev_rg_conv_w, ev_rg_conv_b, ev_rg_wa, ev_rg_ba, ev_rg_wx, ev_rg_bx, ev_rg_lambda, ev_hy_conv_w, ev_hy_conv_b, ev_hy_f_w1, ev_hy_f_b1, ev_hy_f_freq, ev_hy_f_w2, ev_hy_f_b2, ev_hy_f_w3, ev_hy_bias, ev_w_out, od_w_in, od_q_norm_g, od_kv_norm_g, od_w_uq, od_w_ukv, od_w_o, ffn_w_up, ffn_conv_w, ffn_conv_b, ffn_w_down, final_g):
    cvec = jnp.concatenate([c, c_ctx[None], jnp.zeros((16 - B - 1, D), F32)], axis=0)
    mod = modulation(cvec, ada_w, ada_b)

    def mod_sel(layer):
        lat = mod[layer, :B].reshape(B, 1, 6, D)
        cx = jnp.broadcast_to(mod[layer, B].reshape(1, 1, 6, D), (B, 1, 6, D))
        return jnp.concatenate([lat, cx], axis=1)

    xs = jnp.concatenate([x, ctx], axis=1)
    m0 = mod_sel(0)
    xs = _even_layer(xs, m0, norm1_g[0], ev_w_in[0], ev_rg_conv_w[0], ev_rg_conv_b[0], ev_rg_wa[0], ev_rg_ba[0],
                     ev_rg_wx[0], ev_rg_bx[0], ev_rg_lambda[0], ev_hy_conv_w[0], ev_hy_conv_b[0], ev_hy_f_w1[0],
                     ev_hy_f_b1[0], ev_hy_f_freq[0], ev_hy_f_w2[0], ev_hy_f_b2[0], ev_hy_f_w3[0], ev_hy_bias[0],
                     ev_w_out[0])
    xs = _ffn(xs, m0, norm2_g[0], ffn_w_up[0], ffn_conv_w[0], ffn_conv_b[0], ffn_w_down[0], final_g,
              384, 128, (0, S), S, False, "0")
    m1 = mod_sel(1)
    xl = _odd_layer(xs, m1, norm1_g[1], od_w_in[0], od_q_norm_g[0], od_kv_norm_g[0], od_w_uq[0], od_w_ukv[0],
                    od_w_o[0])
    return _ffn(xl, m1, norm2_g[1], ffn_w_up[1], ffn_conv_w[1], ffn_conv_b[1], ffn_w_down[1], final_g,
                512, 256, (0,), None, True, "1")
```

```python
import functools
import math

import numpy as np
import jax
import jax.numpy as jnp
from jax import lax
from jax.experimental import pallas as pl
from jax.experimental.pallas import tpu as pltpu

F32 = jnp.float32
BF16 = jnp.bfloat16

D = 1024
B = 8
S = 2048
C = 256
T = S + C
EPS = 1e-6

RG_W = 512
RG_HEADS = 8
RG_HD = 64
RG_C = 8.0
HY_W = 512
HY_EMB = 33
HY_HID = 64
EVEN_IN = 2560

HEADS = 8
Q_LORA = 512
KV_LORA = 256
NOPE = 128
ROPE = 64
VH = 128
HQ = 256
ODD_IN_P = 896
D_FF = 2816

VMEM_LIMIT = 52 * 1024 * 1024


def _cp(sem):
    return pltpu.CompilerParams(dimension_semantics=sem, vmem_limit_bytes=VMEM_LIMIT)


def _dot(a, b):
    return jnp.dot(a, b, preferred_element_type=F32)


def _split(x):
    hi = x.astype(BF16)
    lo = (x - hi.astype(F32)).astype(BF16)
    return hi, lo


def _dot3(ah, al, bh, bl):
    return _dot(ah, bh) + (_dot(ah, bl) + _dot(al, bh))


def _sigmoid(x):
    return 1.0 / (1.0 + jnp.exp2(x * (-math.log2(math.e))))


def _mod_rows(m_ref, k, row, ctx_start):
    lat = m_ref[0, k:k + 1, :]
    if ctx_start is None:
        return lat
    return jnp.where(row >= ctx_start, m_ref[1, k:k + 1, :], lat)


def _strided_rows(ref3, start, stride):
    return jnp.concatenate([ref3[cb, pl.ds(start, 8, stride=stride), :] for cb in range(ref3.shape[0])],
                           axis=-1)


def _perm_rows(ref3, base, nrows, r):
    return _strided_rows(ref3, base + r, nrows // 8)


def _unperm_rows(ref3, base, nrows, q):
    s, r0 = divmod(8 * q, nrows // 8)
    return _strided_rows(ref3, base + 8 * r0 + s, 8)


def _mod_kernel(c_ref, w_ref, b_ref, o_ref):
    c = c_ref[...]
    s = (c * _sigmoid(c)).astype(BF16)
    o_ref[...] = _dot(s, w_ref[...].astype(BF16)) + b_ref[...]


def modulation(cvec, ada_w, ada_b):
    depth = ada_w.shape[0]
    tn = 1024
    return pl.pallas_call(
        _mod_kernel,
        grid=(depth, 6 * D // tn),
        in_specs=[pl.BlockSpec((16, D), lambda l, j: (0, 0)),
                  pl.BlockSpec((None, D, tn), lambda l, j: (l, 0, j)),
                  pl.BlockSpec((None, 1, tn), lambda l, j: (l, 0, j))],
        out_specs=pl.BlockSpec((None, 16, tn), lambda l, j: (l, 0, j)),
        out_shape=jax.ShapeDtypeStruct((depth, 16, 6 * D), F32),
        compiler_params=_cp(("parallel", "parallel")),
        name="modulation",
    )(cvec, ada_w, ada_b.reshape(depth, 1, 6 * D))


def _nmm_kernel(x_ref, g_ref, m_ref, w_ref, o_ref, h_ref, *, tm, ish, isc, ctx_start, nchunk):
    x = x_ref[...]
    ms = jnp.mean(x * x, axis=-1, keepdims=True)
    y = x * lax.rsqrt(ms + EPS) * g_ref[...]
    row = pl.program_id(1) * tm + lax.broadcasted_iota(jnp.int32, (tm, 1), 0)
    sc = _mod_rows(m_ref, isc, row, ctx_start)
    sh = _mod_rows(m_ref, ish, row, ctx_start)
    h_ref[...] = (y * (1.0 + sc) + sh).astype(BF16)
    n = o_ref.shape[-1]
    for n0 in range(0, n, nchunk):
        n1 = min(n, n0 + nchunk)
        o_ref[:, n0:n1] = _dot(h_ref[...], w_ref[:, n0:n1]).astype(o_ref.dtype)


def norm_mod_matmul(xs, g, mod, ish, isc, w, out_dtype, tm, ctx_start, name):
    bsz, t, _ = xs.shape
    n = w.shape[1]
    kern = functools.partial(_nmm_kernel, tm=tm, ish=ish, isc=isc, ctx_start=ctx_start, nchunk=512)
    return pl.pallas_call(
        kern,
        grid=(bsz, t // tm),
        in_specs=[pl.BlockSpec((None, tm, D), lambda b, i: (b, i, 0)),
                  pl.BlockSpec((1, D), lambda b, i: (0, 0)),
                  pl.BlockSpec((None, 2, 6, D), lambda b, i: (b, 0, 0, 0)),
                  pl.BlockSpec((D, n), lambda b, i: (0, 0))],
        out_specs=pl.BlockSpec((None, tm, n), lambda b, i: (b, i, 0)),
        out_shape=jax.ShapeDtypeStruct((bsz, t, n), out_dtype),
        scratch_shapes=[pltpu.VMEM((tm, D), BF16)],
        compiler_params=_cp(("parallel", "parallel")),
        name=name,
    )(xs, g.reshape(1, D), mod, w)


def _mmres_kernel(*refs, nparts, tm, ig, ctx_start):
    parts = refs[:nparts]
    w_ref, x_ref, m_ref, o_ref = refs[nparts:nparts + 4]
    acc = None
    k0 = 0
    for p in parts:
        kp = p.shape[-1]
        d = _dot(p[...], w_ref[k0:k0 + kp, :])
        acc = d if acc is None else acc + d
        k0 += kp
    row = pl.program_id(1) * tm + lax.broadcasted_iota(jnp.int32, (tm, 1), 0)
    gate = _mod_rows(m_ref, ig, row, ctx_start)
    o_ref[...] = x_ref[...] + gate * acc


def matmul_residual(parts, w, xs, mod, ig, t_out, tm, ctx_start, name):
    bsz = xs.shape[0]
    kern = functools.partial(_mmres_kernel, nparts=len(parts), tm=tm, ig=ig, ctx_start=ctx_start)
    in_specs = [pl.BlockSpec((None, tm, p.shape[-1]), lambda b, i: (b, i, 0)) for p in parts]
    in_specs += [pl.BlockSpec(w.shape, lambda b, i: (0, 0)),
                 pl.BlockSpec((None, tm, D), lambda b, i: (b, i, 0)),
                 pl.BlockSpec((None, 2, 6, D), lambda b, i: (b, 0, 0, 0))]
    return pl.pallas_call(
        kern,
        grid=(bsz, t_out // tm),
        in_specs=in_specs,
        out_specs=pl.BlockSpec((None, tm, D), lambda b, i: (b, i, 0)),
        out_shape=jax.ShapeDtypeStruct((bsz, t_out, D), F32),
        compiler_params=_cp(("parallel", "parallel")),
        name=name,
    )(*parts, w, xs, mod)


def _ffn_kernel(x_ref, xp_ref, xn_ref, g_ref, m_ref, wu_ref, cw_ref, cb_ref, wd_ref, fg_ref, o_ref,
                h_ref, x3_ref, acc_ref, *, tm, t, sb, seg_starts, ctx_start, final, ck):
    i = pl.program_id(1)
    nsub = tm // sb
    sub8 = lax.broadcasted_iota(jnp.int32, (8, 1), 0)
    g = g_ref[...]

    def normmod(x, row):
        ms = jnp.mean(x * x, axis=-1, keepdims=True)
        y = x * lax.rsqrt(ms + EPS) * g
        return y * (1.0 + _mod_rows(m_ref, 4, row, ctx_start)) + _mod_rows(m_ref, 3, row, ctx_start)

    for cb in range(D // 128):
        x3_ref[cb] = x_ref[:, cb * 128:(cb + 1) * 128]
    for j in range(nsub):
        for r in range(0, sb // 8, 2):
            x = jnp.concatenate([_perm_rows(x3_ref, j * sb, sb, r), _perm_rows(x3_ref, j * sb, sb, r + 1)], axis=0)
            h_ref[j * sb + 8 * r:j * sb + 8 * r + 16, :] = normmod(x, i * tm + j * sb).astype(BF16)
    h_ref[tm:tm + 16, :] = jnp.concatenate([normmod(xp_ref[...], i * tm - 1), normmod(xn_ref[...], (i + 1) * tm)],
                                           axis=0).astype(BF16)

    keep_prev = []
    keep_next = []
    for j in range(nsub):
        row0 = i * tm + j * sb
        kp = row0 != seg_starts[0]
        kn = row0 + sb != t
        for s0 in seg_starts[1:]:
            kp = kp & (row0 != s0)
            kn = kn & (row0 + sb != s0)
        keep_prev.append(jnp.where(kp, 1.0, 0.0))
        keep_next.append(jnp.where(kn, 1.0, 0.0))

    def conv(u, j, c0):
        r0 = j * sb
        cur = u[r0:r0 + sb]
        hp = (u[tm + 7:tm + 8] if j == 0 else u[r0 - 1:r0]) * keep_prev[j]
        hn = (u[tm + 8:tm + 9] if j == nsub - 1 else u[r0 + sb:r0 + sb + 1]) * keep_next[j]
        p0 = jnp.where(sub8 == 0, hp, pltpu.roll(cur[sb - 8:sb], 1, 0))
        nl = jnp.where(sub8 == 7, hn, pltpu.roll(cur[0:8], 7, 0))
        prev = jnp.concatenate([p0, cur[:sb - 8]], axis=0)
        nxt = jnp.concatenate([cur[8:], nl], axis=0)
        w = cw_ref[:, c0:c0 + ck]
        return prev * w[0:1] + cur * w[1:2] + nxt * w[2:3] + cb_ref[:, c0:c0 + ck]

    h = h_ref[...]

    def up(c0):
        return _dot(h, wu_ref[:, c0:c0 + ck]), _dot(h, wu_ref[:, D_FF + c0:D_FF + c0 + ck])

    ahead = up(0)
    for k, c0 in enumerate(range(0, D_FF, ck)):
        ug, uv = ahead
        if c0 + ck < D_FF:
            ahead = up(c0 + ck)
        parts = []
        for j in range(nsub):
            gt = conv(ug, j, c0)
            vl = conv(uv, j, D_FF + c0)
            parts.append((gt * _sigmoid(gt) * vl).astype(BF16))
        a = parts[0] if nsub == 1 else jnp.concatenate(parts, axis=0)
        d = _dot(a, wd_ref[c0:c0 + ck, :])
        for cb in range(D // 128):
            if k == 0:
                acc_ref[cb] = d[:, cb * 128:(cb + 1) * 128]
            else:
                acc_ref[cb] += d[:, cb * 128:(cb + 1) * 128]
    fg = fg_ref[...]
    for j in range(nsub):
        gate = _mod_rows(m_ref, 5, i * tm + j * sb, ctx_start)
        for q in range(sb // 8):
            rows = slice(j * sb + 8 * q, j * sb + 8 * q + 8)
            out = x_ref[rows, :] + gate * _unperm_rows(acc_ref, j * sb, sb, q)
            if final:
                ms = jnp.mean(out * out, axis=-1, keepdims=True)
                out = out * lax.rsqrt(ms + EPS) * fg
            o_ref[rows, :] = out


def conv_ffn(xs, g, mod, w_up, conv_w, conv_b, w_down, final_g, tm, sb, seg_starts, ctx_start, final, name):
    bsz, t, _ = xs.shape
    f2 = 2 * D_FF
    r = tm // 8
    n8 = t // 8
    kern = functools.partial(_ffn_kernel, tm=tm, t=t, sb=sb, seg_starts=seg_starts, ctx_start=ctx_start,
                             final=final, ck=256)
    const = lambda shape: pl.BlockSpec(shape, lambda b, i: (0,) * len(shape), pipeline_mode=pl.Buffered(1))
    return pl.pallas_call(
        kern,
        grid=(bsz, t // tm),
        in_specs=[pl.BlockSpec((None, tm, D), lambda b, i: (b, i, 0)),
                  pl.BlockSpec((None, 8, D), lambda b, i: (b, jnp.maximum(i * r - 1, 0), 0)),
                  pl.BlockSpec((None, 8, D), lambda b, i: (b, jnp.minimum((i + 1) * r, n8 - 1), 0)),
                  const((1, D)),
                  pl.BlockSpec((None, 2, 6, D), lambda b, i: (b, 0, 0, 0)),
                  const((D, f2)), const((3, f2)), const((1, f2)), const((D_FF, D)), const((1, D))],
        out_specs=pl.BlockSpec((None, tm, D), lambda b, i: (b, i, 0)),
        out_shape=jax.ShapeDtypeStruct((bsz, t, D), F32),
        scratch_shapes=[pltpu.VMEM((tm + 16, D), BF16),
                        pltpu.VMEM((D // 128, tm, 128), F32), pltpu.VMEM((D // 128, tm, 128), F32)],
        compiler_params=_cp(("parallel", "parallel")),
        name=name,
    )(xs, xs, xs, g.reshape(1, D), mod, w_up, conv_w, conv_b.reshape(1, f2), w_down, final_g.reshape(1, D))


PAD = 8


def _fill_padded(xp_ref, x_ref, width):
    z = jnp.zeros((PAD, width), F32)
    xp_ref[0:PAD, :] = z
    xp_ref[PAD:PAD + S, :] = x_ref[0:S, :].astype(F32)
    xp_ref[PAD + S:2 * PAD + S, :] = z
    xp_ref[2 * PAD + S:2 * PAD + T, :] = x_ref[S:T, :].astype(F32)
    xp_ref[2 * PAD + T:3 * PAD + T, :] = z


def _conv_tile(xp_ref, w, bias, r0, rows, ktaps):
    base = PAD + r0 if r0 < S else 2 * PAD + r0
    left = ktaps // 2
    acc = bias
    for j in range(ktaps):
        acc = acc + xp_ref[base + j - left:base + j - left + rows, :] * w[j:j + 1, :]
    return acc


def _hconv_kernel(z_ref, w_ref, b_ref, o_ref, xp_ref):
    _fill_padded(xp_ref, z_ref, 512)
    w = w_ref[...]
    bias = b_ref[...]
    for r0 in range(0, T, 256):
        o_ref[r0:r0 + 256, :] = _conv_tile(xp_ref, w, bias, r0, 256, 3)


def hyena_short_conv(z, conv_w, conv_b):
    return pl.pallas_call(
        _hconv_kernel,
        grid=(B, 3),
        in_specs=[pl.BlockSpec((None, T, 512), lambda b, j: (b, 0, 2 + j)),
                  pl.BlockSpec((3, 512), lambda b, j: (0, j)),
                  pl.BlockSpec((1, 512), lambda b, j: (0, j))],
        out_specs=pl.BlockSpec((None, T, 512), lambda b, j: (b, 0, j)),
        out_shape=jax.ShapeDtypeStruct((B, T, 3 * HY_W), F32),
        scratch_shapes=[pltpu.VMEM((T + 3 * PAD, 512), F32)],
        compiler_params=_cp(("parallel", "parallel")),
        name="hyena_short_conv",
    )(z, conv_w, conv_b.reshape(1, 3 * HY_W))


RGH = 128
SEGS = ((S, C, C // 8), (0, S, S // 8))
POFF = {S: 0, 0: C + 24}


def _rg_kernel(x_ref, gt_ref, cw_ref, cb_ref, wbd_ref, bias_ref, sp_ref, o_ref,
               xp_ref, af_ref, bf_ref, ar_ref, br_ref, cf_ref, cr_ref):
    sub8 = lax.broadcasted_iota(jnp.int32, (8, 1), 0)
    cw = cw_ref[...]
    cb = cb_ref[...]
    nl2e8 = -RG_C * math.log2(math.e)
    spf = sp_ref[0:1, :] * nl2e8
    spr = sp_ref[1:2, :] * nl2e8

    for seg0, rows, nslot in SEGS:
        p0 = POFF[seg0]

        def fill(r, _):
            xp_ref[pl.ds(pl.multiple_of(p0 + 16 + 8 * r, 8), 8), :] = (
                x_ref[pl.ds(seg0 + r, 8, stride=nslot), :].astype(F32))
            return 0

        lax.fori_loop(0, nslot, fill, 0, unroll=8)
        for slot, src, sh in ((0, nslot - 2, 1), (1, nslot - 1, 1), (nslot + 2, 0, 7)):
            v = pltpu.roll(xp_ref[p0 + 16 + 8 * src:p0 + 24 + 8 * src, :], sh, 0)
            xp_ref[p0 + 8 * slot:p0 + 8 * slot + 8, :] = jnp.where(sub8 == (0 if sh == 1 else 7), 0.0, v)

        for t0 in range(0, rows, 256):
            xc = cb
            for j in range(4):
                xc = xc + xp_ref[p0 + 8 * j + t0:p0 + 8 * j + t0 + 256, :] * cw[j:j + 1, :]
            g = _dot(xc.astype(BF16), wbd_ref[...])
            for d, (a_ref, b_ref, sp) in enumerate(((af_ref, bf_ref, spf), (ar_ref, br_ref, spr))):
                rg = _sigmoid(g[:, (2 * d) * RGH:(2 * d + 1) * RGH] + bias_ref[2 * d:2 * d + 1, :])
                ig = _sigmoid(g[:, (2 * d + 1) * RGH:(2 * d + 2) * RGH] + bias_ref[2 * d + 1:2 * d + 2, :])
                a = jnp.exp2(rg * sp)
                a_ref[seg0 + t0:seg0 + t0 + 256, :] = a
                om = 1.0 - a * a
                root = jnp.where(om > 0.0, om * lax.rsqrt(om), 0.0)
                b_ref[seg0 + t0:seg0 + t0 + 256, :] = root * (ig * xc)

    one = jnp.ones((8, RGH), F32)
    zero = jnp.zeros((8, RGH), F32)
    for seg0, rows, nslot in SEGS:
        def body(i, carry):
            pf, hf, pr, hr = carry
            tf = pl.multiple_of(seg0 + 8 * i, 8)
            tr = pl.multiple_of(seg0 + 8 * (nslot - 1 - i), 8)
            a = af_ref[pl.ds(tf, 8), :]
            hf = a * hf + bf_ref[pl.ds(tf, 8), :]
            pf = a * pf
            af_ref[pl.ds(tf, 8), :] = pf
            bf_ref[pl.ds(tf, 8), :] = hf
            a = ar_ref[pl.ds(tr, 8), :]
            hr = a * hr + br_ref[pl.ds(tr, 8), :]
            pr = a * pr
            ar_ref[pl.ds(tr, 8), :] = pr
            br_ref[pl.ds(tr, 8), :] = hr
            return pf, hf, pr, hr

        lax.fori_loop(0, nslot, body, (one, zero, one, zero), unroll=4)

    cf = jnp.zeros((1, RGH), F32)
    cr = jnp.zeros((1, RGH), F32)
    for k, (seg0, rows, nslot) in enumerate(SEGS):
        last = seg0 + 8 * (nslot - 1)
        for s in range(8):
            cf_ref[8 * k + s:8 * k + s + 1, :] = cf
            cf = bf_ref[last + s:last + s + 1, :] + af_ref[last + s:last + s + 1, :] * cf
        for s in range(7, -1, -1):
            cr_ref[8 * k + s:8 * k + s + 1, :] = cr
            cr = br_ref[seg0 + s:seg0 + s + 1, :] + ar_ref[seg0 + s:seg0 + s + 1, :] * cr

    for k, (seg0, rows, nslot) in enumerate(SEGS):
        reps = min(rows, 256) // 8
        cft = jnp.concatenate([cf_ref[8 * k:8 * k + 8, :]] * reps, axis=0)
        crt = jnp.concatenate([cr_ref[8 * k:8 * k + 8, :]] * reps, axis=0)
        for t0 in range(0, rows, 256):
            sl = slice(seg0 + t0, seg0 + t0 + 256)
            bf_ref[sl, :] = (bf_ref[sl, :] + af_ref[sl, :] * cft) + (br_ref[sl, :] + ar_ref[sl, :] * crt)

    for seg0, rows, nslot in SEGS:
        per = nslot // 8

        def unperm(q, _):
            s = lax.shift_right_logical(q, per.bit_length() - 1)
            r0 = q & (per - 1)
            af_ref[pl.ds(pl.multiple_of(seg0 + 8 * q, 8), 8), :] = (
                bf_ref[pl.ds(seg0 + 64 * r0 + s, 8, stride=8), :])
            return 0

        lax.fori_loop(0, rows // 8, unperm, 0, unroll=8)

    c0 = math.sqrt(2.0 / math.pi)
    for r0 in range(0, T, 256):
        gt = gt_ref[r0:r0 + 256, :].astype(F32)
        gelu = 0.5 * gt * (1.0 + jnp.tanh(c0 * (gt + 0.044715 * (gt * gt * gt))))
        o_ref[r0:r0 + 256, :] = (af_ref[r0:r0 + 256, :] * gelu).astype(o_ref.dtype)


def rglru(z, conv_w, conv_b, wbd, bias4, softplus_neg_lam):
    nh = RG_W // RGH
    return pl.pallas_call(
        _rg_kernel,
        grid=(B, nh),
        in_specs=[pl.BlockSpec((None, T, RGH), lambda b, j: (b, 0, j)),
                  pl.BlockSpec((None, T, RGH), lambda b, j: (b, 0, nh + j)),
                  pl.BlockSpec((4, RGH), lambda b, j: (0, j)),
                  pl.BlockSpec((1, RGH), lambda b, j: (0, j)),
                  pl.BlockSpec((None, RGH, 4 * RGH), lambda b, j: (j, 0, 0)),
                  pl.BlockSpec((4, RGH), lambda b, j: (0, j)),
                  pl.BlockSpec((2, RGH), lambda b, j: (0, j))],
        out_specs=pl.BlockSpec((None, T, RGH), lambda b, j: (b, 0, j)),
        out_shape=jax.ShapeDtypeStruct((B, T, RG_W), BF16),
        scratch_shapes=[pltpu.VMEM((T + 48, RGH), F32)] + [pltpu.VMEM((T, RGH), F32)] * 4
                       + [pltpu.VMEM((16, RGH), F32)] * 2,
        compiler_params=_cp(("parallel", "parallel")),
        name="rglru",
    )(z, z, conv_w, conv_b.reshape(1, RG_W), wbd, bias4, softplus_neg_lam)


def _softplus_kernel(x_ref, o_ref):
    y = -x_ref[...]
    o_ref[...] = jnp.maximum(y, 0.0) + jnp.log(1.0 + jnp.exp(-jnp.abs(y)))


def softplus_neg(lam):
    return pl.pallas_call(
        _softplus_kernel,
        out_shape=jax.ShapeDtypeStruct(lam.shape, F32),
        name="softplus_neg",
    )(lam)


def _hfilt_kernel(z_ref, w1_ref, b1_ref, fr_ref, w2_ref, b2_ref, w3_ref, dec_ref, x1_ref, x2_ref, nyq_ref,
                  s1_ref, s2_ref, *, n, tr):
    fr = fr_ref[...]
    w1h, w1l = _split(w1_ref[...])
    w2h, w2l = _split(w2_ref[...])
    w3h, w3l = _split(w3_ref[...])
    tot = jnp.zeros((1, HY_W), F32)
    alt = jnp.zeros((1, HY_W), F32)
    sign = jnp.where(lax.broadcasted_iota(jnp.int32, (tr, 1), 0) % 2 == 0, 1.0, -1.0)
    for r0 in range(0, n, tr):
        zh, zl = _split(z_ref[r0:r0 + tr, :])
        h = jnp.sin(fr * (_dot3(zh, zl, w1h, w1l) + b1_ref[...]))
        hh, hl = _split(h)
        h = jnp.sin(fr * (_dot3(hh, hl, w2h, w2l) + b2_ref[...]))
        hh, hl = _split(h)
        h = _dot3(hh, hl, w3h, w3l)
        dec = dec_ref[r0:r0 + tr, :]
        h0 = h[:, :HY_W] * dec
        h1 = h[:, HY_W:] * dec
        if r0 == 0:
            h1 = jnp.where(lax.broadcasted_iota(jnp.int32, (tr, 1), 0) == 0, 0.0, h1)
        tot = tot + jnp.sum(jnp.abs(h0) + jnp.abs(h1), axis=0, keepdims=True)
        alt = alt + jnp.sum((h0 + h1) * sign, axis=0, keepdims=True)
        s1_ref[r0:r0 + tr, :] = h0 + h1
        s2_ref[r0:r0 + tr, :] = h0 - h1
    inv = 1.0 / tot
    for r0 in range(0, n, tr):
        x1_ref[r0:r0 + tr, :] = (s1_ref[r0:r0 + tr, :] * inv).astype(BF16)
        x2_ref[r0:r0 + tr, :] = (s2_ref[r0:r0 + tr, :] * inv).astype(BF16)
    nyq_ref[...] = jnp.broadcast_to(alt * inv, (8, HY_W))


def hyena_filter_halves(n, zemb, w1p, b1, freq, w2, b2, w3, decay):
    kern = functools.partial(_hfilt_kernel, n=n, tr=256)
    full = lambda shape: pl.BlockSpec(shape, lambda o: (0,) * len(shape))
    return pl.pallas_call(
        kern,
        grid=(2,),
        in_specs=[full((n, 128)), full((128, HY_HID)), full((1, HY_HID)), full((1, HY_HID)),
                  full((HY_HID, HY_HID)), full((1, HY_HID)),
                  pl.BlockSpec((HY_HID, 2 * HY_W), lambda o: (0, o)),
                  full((n, HY_W))],
        out_specs=[pl.BlockSpec((n, HY_W), lambda o: (0, o)),
                   pl.BlockSpec((n, HY_W), lambda o: (0, o)),
                   pl.BlockSpec((8, HY_W), lambda o: (0, o))],
        out_shape=[jax.ShapeDtypeStruct((n, 2 * HY_W), BF16),
                   jax.ShapeDtypeStruct((n, 2 * HY_W), BF16),
                   jax.ShapeDtypeStruct((8, 2 * HY_W), F32)],
        scratch_shapes=[pltpu.VMEM((n, HY_W), F32), pltpu.VMEM((n, HY_W), F32)],
        compiler_params=_cp(("parallel",)),
        name=f"hyena_filter_{n}",
    )(zemb, w1p, b1.reshape(1, HY_HID), freq.reshape(1, HY_HID), w2, b2.reshape(1, HY_HID), w3, decay)


def _kspec_kernel(at_ref, ab_ref, x1_ref, x2_ref, kt_ref, kb_ref):
    kt_ref[...] = _dot(at_ref[...], x1_ref[...])
    kb_ref[...] = _dot(ab_ref[...], x2_ref[...])


def filter_spectrum(fwd, x1, x2, n, tr):
    nt = n // tr
    return pl.pallas_call(
        _kspec_kernel,
        grid=(nt,),
        in_specs=[pl.BlockSpec((tr, n), lambda i: (i, 0)),
                  pl.BlockSpec((tr, n), lambda i: (nt + i, 0)),
                  pl.BlockSpec((n, 2 * HY_W), lambda i: (0, 0)),
                  pl.BlockSpec((n, 2 * HY_W), lambda i: (0, 0))],
        out_specs=[pl.BlockSpec((tr, 2 * HY_W), lambda i: (i, 0)),
                   pl.BlockSpec((tr, 2 * HY_W), lambda i: (i, 0))],
        out_shape=[jax.ShapeDtypeStruct((n, 2 * HY_W), F32)] * 2,
        compiler_params=_cp(("parallel",)),
        name=f"filter_spectrum_{n}",
    )(fwd, fwd, x1, x2)


def _hfwd_kernel(at_ref, ab_ref, x_ref, kt_ref, kb_ref, kn_ref, yr_ref, yi_ref, xb_ref, *, tr):
    i = pl.program_id(1)

    @pl.when(i == 0)
    def _():
        xb_ref[...] = x_ref[...].astype(BF16)

    u = _dot(jnp.concatenate([at_ref[...], ab_ref[...]], axis=0), xb_ref[...])
    ur = u[:tr]
    ui = u[tr:]
    kr = kt_ref[...]
    ki = kb_ref[...]
    first = (lax.broadcasted_iota(jnp.int32, (tr, 1), 0) == 0) & (i == 0)
    yr_ref[...] = (ur * kr - jnp.where(first, 0.0, ui * ki)).astype(BF16)
    yi_ref[...] = jnp.where(first, ui * kn_ref[0:1, :], ur * ki + ui * kr).astype(BF16)


def hyena_forward(fwd, x, x_rb, kt, kb, nyq, order, n, tr, name):
    nt = n // tr
    kern = functools.partial(_hfwd_kernel, tr=tr)
    return pl.pallas_call(
        kern,
        grid=(B, nt),
        in_specs=[pl.BlockSpec((tr, n), lambda b, i: (i, 0)),
                  pl.BlockSpec((tr, n), lambda b, i: (nt + i, 0)),
                  pl.BlockSpec((None, n, HY_W), lambda b, i: (b, x_rb, 0)),
                  pl.BlockSpec((tr, HY_W), lambda b, i: (i, order)),
                  pl.BlockSpec((tr, HY_W), lambda b, i: (i, order)),
                  pl.BlockSpec((8, HY_W), lambda b, i: (0, order))],
        out_specs=[pl.BlockSpec((None, tr, HY_W), lambda b, i: (b, i, 0)),
                   pl.BlockSpec((None, tr, HY_W), lambda b, i: (b, i, 0))],
        out_shape=[jax.ShapeDtypeStruct((B, n, HY_W), BF16)] * 2,
        scratch_shapes=[pltpu.VMEM((n, HY_W), BF16)],
        compiler_params=_cp(("parallel", "arbitrary")),
        name=name,
    )(fwd, fwd, x, kt, kb, nyq)


def _hinv_kernel(al_ref, ar_ref, yr_ref, yi_ref, up_ref, gt_ref, bias_ref, o_ref):
    y = _dot(al_ref[...], yr_ref[...]) + _dot(ar_ref[...], yi_ref[...])
    o_ref[...] = (gt_ref[...] * (y + up_ref[...].astype(F32) * bias_ref[...])).astype(o_ref.dtype)


def hyena_inverse(inv, yr, yi, uprev, uprev_rb, gate, gate_rb, gate_cb, bias, n, tmi, out_dtype, name):
    return pl.pallas_call(
        _hinv_kernel,
        grid=(B, n // tmi),
        in_specs=[pl.BlockSpec((tmi, n), lambda b, i: (i, 0)),
                  pl.BlockSpec((tmi, n), lambda b, i: (i, 1)),
                  pl.BlockSpec((None, n, HY_W), lambda b, i: (b, 0, 0)),
                  pl.BlockSpec((None, n, HY_W), lambda b, i: (b, 0, 0)),
                  pl.BlockSpec((None, tmi, HY_W), lambda b, i: (b, uprev_rb + i, 0)),
                  pl.BlockSpec((None, tmi, HY_W), lambda b, i: (b, gate_rb + i, gate_cb)),
                  pl.BlockSpec((1, HY_W), lambda b, i: (0, 0))],
        out_specs=pl.BlockSpec((None, tmi, HY_W), lambda b, i: (b, i, 0)),
        out_shape=jax.ShapeDtypeStruct((B, n, HY_W), out_dtype),
        compiler_params=_cp(("parallel", "parallel")),
        name=name,
    )(inv, inv, yr, yi, uprev, gate, bias.reshape(1, HY_W))


def _rope128(r, c, sa, sb):
    return r * c + pltpu.roll(r, 32, 1) * sa + pltpu.roll(r, 96, 1) * sb


def _qkv_kernel(z_ref, qg_ref, kvg_ref, wq_ref, wk_ref, wv_ref, c_ref, sa_ref, sb_ref,
                q_ref, k_ref, v_ref):
    c = c_ref[...]
    sa = sa_ref[...]
    sb = sb_ref[...]
    zq = z_ref[:, :Q_LORA]
    qn = (zq * lax.rsqrt(jnp.mean(zq * zq, axis=-1, keepdims=True) + EPS) * qg_ref[...]).astype(BF16)
    zkv = z_ref[:, Q_LORA:Q_LORA + KV_LORA]
    ckv = (zkv * lax.rsqrt(jnp.mean(zkv * zkv, axis=-1, keepdims=True) + EPS) * kvg_ref[...]).astype(BF16)
    kr = _rope128(z_ref[:, Q_LORA + KV_LORA:], c, sa, sb).astype(BF16)
    for h in range(HEADS):
        q = _dot(qn, wq_ref[:, h * HQ:(h + 1) * HQ])
        q_ref[:, h * HQ:h * HQ + NOPE] = q[:, :NOPE].astype(BF16)
        q_ref[:, h * HQ + NOPE:(h + 1) * HQ] = _rope128(q[:, NOPE:], c, sa, sb).astype(BF16)
        k_ref[:, h * HQ:h * HQ + NOPE] = _dot(ckv, wk_ref[:, h * NOPE:(h + 1) * NOPE]).astype(BF16)
        k_ref[:, h * HQ + NOPE:(h + 1) * HQ] = kr
    v_ref[...] = _dot(ckv, wv_ref[...]).astype(BF16)


def mla_qkv(z, q_g, kv_g, wq, wk, wv, ctab, satab, sbtab, tm):
    full = lambda shape: pl.BlockSpec(shape, lambda b, i: (0,) * len(shape))
    tab = pl.BlockSpec((tm, 128), lambda b, i: (i, 0))
    return pl.pallas_call(
        _qkv_kernel,
        grid=(B, T // tm),
        in_specs=[pl.BlockSpec((None, tm, ODD_IN_P), lambda b, i: (b, i, 0)),
                  full((1, Q_LORA)), full((1, KV_LORA)),
                  full((Q_LORA, HEADS * HQ)), full((KV_LORA, HEADS * NOPE)), full((KV_LORA, HEADS * VH)),
                  tab, tab, tab],
        out_specs=[pl.BlockSpec((None, tm, HEADS * HQ), lambda b, i: (b, i, 0)),
                   pl.BlockSpec((None, tm, HEADS * HQ), lambda b, i: (b, i, 0)),
                   pl.BlockSpec((None, tm, HEADS * VH), lambda b, i: (b, i, 0))],
        out_shape=[jax.ShapeDtypeStruct((B, T, HEADS * HQ), BF16),
                   jax.ShapeDtypeStruct((B, T, HEADS * HQ), BF16),
                   jax.ShapeDtypeStruct((B, T, HEADS * VH), BF16)],
        compiler_params=_cp(("parallel", "parallel")),
        name="mla_qkv",
    )(z, q_g.reshape(1, Q_LORA), kv_g.reshape(1, KV_LORA), wq, wk, wv, ctab, satab, sbtab)


def _attn_kernel(q_ref, k_ref, v_ref, o_ref, *, scale, ts):
    c = scale * math.log2(math.e)
    k = k_ref[...]
    kc = 256
    def scores(r0):
        return lax.dot_general(q_ref[r0:r0 + ts, :], k, (((1,), (1,)), ((), ())), preferred_element_type=F32)

    tq = q_ref.shape[0]
    ahead = scores(0)
    for r0 in range(0, tq, ts):
        s = ahead
        if r0 + ts < tq:
            ahead = scores(r0 + ts)
        m = jnp.max(s, axis=-1, keepdims=True)
        acc = None
        lv = None
        for c0 in range(0, T, kc):
            p = jnp.exp2((s[:, c0:c0 + kc] - m) * c)
            pl_ = p[:, :128] + p[:, 128:]
            lv = pl_ if lv is None else lv + pl_
            d = _dot(p.astype(BF16), v_ref[c0:c0 + kc, :])
            acc = d if acc is None else acc + d
        l = jnp.sum(lv, axis=-1, keepdims=True)
        o_ref[r0:r0 + ts, :] = (acc * (1.0 / l)).astype(o_ref.dtype)


def mla_attention(q, k, v, tq):
    kern = functools.partial(_attn_kernel, scale=(NOPE + ROPE) ** -0.5, ts=512)
    return pl.pallas_call(
        kern,
        grid=(B, HEADS, S // tq),
        in_specs=[pl.BlockSpec((None, tq, HQ), lambda b, h, i: (b, i, h)),
                  pl.BlockSpec((None, T, HQ), lambda b, h, i: (b, 0, h)),
                  pl.BlockSpec((None, T, VH), lambda b, h, i: (b, 0, h))],
        out_specs=pl.BlockSpec((None, tq, VH), lambda b, h, i: (b, i, h)),
        out_shape=jax.ShapeDtypeStruct((B, S, HEADS * VH), BF16),
        compiler_params=_cp(("parallel", "parallel", "parallel")),
        name="mla_attention",
    )(q, k, v)


@functools.lru_cache(maxsize=None)
def _dft_tables(n):
    m = 2 * n
    j = np.arange(n, dtype=np.int64)
    kk = np.arange(n, dtype=np.int64)
    ang = 2.0 * np.pi * ((kk[:, None] * j[None, :]) % m).astype(np.float64) / m
    cosm = np.cos(ang)
    sinm = np.sin(ang)
    fwd = np.concatenate([cosm, -sinm], axis=0)
    fwd[n, :] = np.where(j % 2 == 0, 1.0, -1.0)
    inv = np.concatenate([2.0 * cosm.T, -2.0 * sinm.T], axis=1) / m
    inv[:, 0] = 1.0 / m
    inv[:, n] = np.where(j % 2 == 0, 1.0, -1.0) / m
    return fwd.astype(np.float32).astype(BF16), inv.astype(np.float32).astype(BF16)


def _filter_consts(n):
    pos = jnp.arange(n, dtype=F32)
    t = jnp.linspace(0.0, 1.0, n, dtype=F32)[:, None]
    bands = (HY_EMB - 1) // 2
    w = 2.0 * math.pi * pos / n
    f = jnp.linspace(1e-4, bands - 1, bands, dtype=F32)
    ang = w[:, None] * f[None, :]
    z = jnp.concatenate([t, jnp.cos(ang), -jnp.sin(ang)], axis=-1)
    z = jnp.pad(z, ((0, 0), (0, 128 - HY_EMB)))
    max_decay = math.log(1e-2) / 0.3
    min_decay = math.log(1e-2) / 1.5
    deltas = jnp.linspace(min_decay, max_decay, HY_W, dtype=F32)
    decay = jnp.exp(-t * jnp.abs(deltas)[None, :])
    return z, decay


def _rope_tables():
    rows = S // 64
    row = jnp.repeat(jnp.arange(rows, dtype=F32), 64)
    col = jnp.tile(jnp.arange(64, dtype=F32), rows)
    n_freq = ROPE // 4
    inv = 10000.0 ** (-jnp.arange(n_freq, dtype=F32) / n_freq)
    ang = jnp.concatenate([row[:, None] * inv[None, :], col[:, None] * inv[None, :]], axis=-1)
    cos = jnp.concatenate([jnp.cos(ang), jnp.ones((C, 32), F32)], axis=0)
    sin = jnp.concatenate([jnp.sin(ang), jnp.zeros((C, 32), F32)], axis=0)
    z32 = jnp.zeros((T, 32), F32)
    z64 = jnp.zeros((T, 64), F32)
    ctab = jnp.concatenate([cos, cos, z64], axis=-1)
    satab = jnp.concatenate([z32, sin, z64], axis=-1)
    sbtab = jnp.concatenate([-sin, z32, z64], axis=-1)
    return ctab, satab, sbtab


def _even_layer(xs, mod, norm1_g, w_in, rg_conv_w, rg_conv_b, rg_wa, rg_ba, rg_wx, rg_bx, rg_lambda,
                hy_conv_w, hy_conv_b, f_w1, f_b1, f_freq, f_w2, f_b2, f_w3, hy_bias, w_out):
    z = norm_mod_matmul(xs, norm1_g, mod, 0, 1, w_in.astype(BF16), F32, 768, S, "even_in_proj")

    def blockdiag(w):
        per = RGH // RG_HD
        out = jnp.zeros((RG_W // RGH, RGH, RGH), F32)
        for h in range(RG_HEADS):
            g, l = divmod(h, per)
            out = out.at[g, l * RG_HD:(l + 1) * RG_HD, l * RG_HD:(l + 1) * RG_HD].set(w[h])
        return out

    wbd = jnp.concatenate([blockdiag(rg_wa[0]), blockdiag(rg_wx[0]), blockdiag(rg_wa[1]), blockdiag(rg_wx[1])],
                          axis=-1).astype(BF16)
    bias4 = jnp.stack([rg_ba[0], rg_bx[0], rg_ba[1], rg_bx[1]], axis=0)
    a = rglru(z, rg_conv_w, rg_conv_b, wbd, bias4, softplus_neg(rg_lambda))

    vc = hyena_short_conv(z, hy_conv_w, hy_conv_b)
    w1p = jnp.pad(f_w1, ((0, 128 - HY_EMB), (0, 0)))
    outs = []
    for n, rb in ((S, 0), (C, S // C)):
        fwd, inv = _dft_tables(n)
        zemb, decay = _filter_consts(n)
        x1, x2, nyq = hyena_filter_halves(n, zemb, w1p, f_b1, f_freq, f_w2, f_b2, f_w3, decay)
        tr = min(n, 512)
        tmi = min(n, 1024)
        kt, kb = filter_spectrum(fwd, x1, x2, n, tr)
        grb = rb * (n // tmi)
        yr, yi = hyena_forward(fwd, vc, rb, kt, kb, nyq, 0, n, tr, f"hyena_fwd1_{n}")
        u1 = hyena_inverse(inv, yr, yi, vc, grb, vc, grb, 1, hy_bias[0], n, tmi, F32, f"hyena_inv1_{n}")
        yr, yi = hyena_forward(fwd, u1, 0, kt, kb, nyq, 1, n, tr, f"hyena_fwd2_{n}")
        outs.append(hyena_inverse(inv, yr, yi, u1, 0, vc, grb, 2, hy_bias[1], n, tmi, BF16,
                                  f"hyena_inv2_{n}"))
    bh = jnp.concatenate(outs, axis=1)
    return matmul_residual([a, bh], w_out.astype(BF16), xs, mod, 2, T, 768, S, "even_out_proj")


def _odd_layer(xs, mod, norm1_g, w_in, q_g, kv_g, w_uq, w_ukv, w_o):
    perm = np.concatenate([np.arange(0, ROPE, 2), np.arange(1, ROPE, 2)])
    w_rope = jnp.pad(w_in[:, Q_LORA + KV_LORA:][:, perm], ((0, 0), (0, 128 - ROPE)))
    w_in_p = jnp.concatenate([w_in[:, :Q_LORA + KV_LORA], w_rope], axis=-1).astype(BF16)
    z = norm_mod_matmul(xs, norm1_g, mod, 0, 1, w_in_p, F32, 768, S, "odd_in_proj")

    wq = w_uq.reshape(Q_LORA, HEADS, NOPE + ROPE)
    wq = jnp.concatenate([wq[..., :NOPE], wq[..., NOPE:][..., perm],
                          jnp.zeros((Q_LORA, HEADS, HQ - NOPE - ROPE), F32)], axis=-1)
    wq = wq.reshape(Q_LORA, HEADS * HQ).astype(BF16)
    wkv = w_ukv.reshape(KV_LORA, HEADS, NOPE + VH)
    wk = wkv[..., :NOPE].reshape(KV_LORA, HEADS * NOPE).astype(BF16)
    wv = wkv[..., NOPE:].reshape(KV_LORA, HEADS * VH).astype(BF16)
    ctab, satab, sbtab = _rope_tables()
    q, k, v = mla_qkv(z, q_g, kv_g, wq, wk, wv, ctab, satab, sbtab, 768)
    o = mla_attention(q, k, v, 2048)
    return matmul_residual([o], w_o.astype(BF16), xs, mod, 2, S, 512, None, "odd_out_proj")


def _ffn(xs, mod, norm2_g, w_up, conv_w, conv_b, w_down, final_g, tm, sb, seg_starts, ctx_start, final, tag):
    return conv_ffn(xs, norm2_g, mod, w_up.astype(BF16), conv_w, conv_b, w_down.astype(BF16), final_g, tm, sb,
                    seg_starts, ctx_start, final, f"conv_ffn_{tag}")


def kernel(x, c, ctx, c_ctx, ada_w, ada_b, norm1_g, norm2_g, ev_w_in, ev_rg_conv_w, ev_rg_conv_b, ev_rg_wa, ev_rg_ba, ev_rg_wx, ev_rg_bx, ev_rg_lambda, ev_hy_conv_w, ev_hy_conv_b, ev_hy_f_w1, ev_hy_f_b1, ev_hy_f_freq, ev_hy_f_w2, ev_hy_f_b2, ev_hy_f_w3, ev_hy_bias, ev_w_out, od_w_in, od_q_norm_g, od_kv_norm_g, od_w_uq, od_w_ukv, od_w_o, ffn_w_up, ffn_conv_w, ffn_conv_b, ffn_w_down, final_g):
    cvec = jnp.concatenate([c, c_ctx[None], jnp.zeros((16 - B - 1, D), F32)], axis=0)
    mod = modulation(cvec, ada_w, ada_b)

    def mod_sel(layer):
        lat = mod[layer, :B].reshape(B, 1, 6, D)
        cx = jnp.broadcast_to(mod[layer, B].reshape(1, 1, 6, D), (B, 1, 6, D))
        return jnp.concatenate([lat, cx], axis=1)

    xs = jnp.concatenate([x, ctx], axis=1)
    m0 = mod_sel(0)
    xs = _even_layer(xs, m0, norm1_g[0], ev_w_in[0], ev_rg_conv_w[0], ev_rg_conv_b[0], ev_rg_wa[0], ev_rg_ba[0],
                     ev_rg_wx[0], ev_rg_bx[0], ev_rg_lambda[0], ev_hy_conv_w[0], ev_hy_conv_b[0], ev_hy_f_w1[0],
                     ev_hy_f_b1[0], ev_hy_f_freq[0], ev_hy_f_w2[0], ev_hy_f_b2[0], ev_hy_f_w3[0], ev_hy_bias[0],
                     ev_w_out[0])
    xs = _ffn(xs, m0, norm2_g[0], ffn_w_up[0], ffn_conv_w[0], ffn_conv_b[0], ffn_w_down[0], final_g,
              384, 128, (0, S), S, False, "0")
    m1 = mod_sel(1)
    xl = _odd_layer(xs, m1, norm1_g[1], od_w_in[0], od_q_norm_g[0], od_kv_norm_g[0], od_w_uq[0], od_w_ukv[0],
                    od_w_o[0])
    return _ffn(xl, m1, norm2_g[1], ffn_w_up[1], ffn_conv_w[1], ffn_conv_b[1], ffn_w_down[1], final_g,
                256, 128, (0,), None, True, "1")
```

```python
import functools
import math

import numpy as np
import jax
import jax.numpy as jnp
from jax import lax
from jax.experimental import pallas as pl
from jax.experimental.pallas import tpu as pltpu

F32 = jnp.float32
BF16 = jnp.bfloat16

D = 1024
B = 8
S = 2048
C = 256
T = S + C
EPS = 1e-6

RG_W = 512
RG_HEADS = 8
RG_HD = 64
RG_C = 8.0
HY_W = 512
HY_EMB = 33
HY_HID = 64
EVEN_IN = 2560

HEADS = 8
Q_LORA = 512
KV_LORA = 256
NOPE = 128
ROPE = 64
VH = 128
HQ = 256
ODD_IN_P = 896
D_FF = 2816

VMEM_LIMIT = 52 * 1024 * 1024


def _cp(sem):
    return pltpu.CompilerParams(dimension_semantics=sem, vmem_limit_bytes=VMEM_LIMIT)


def _dot(a, b):
    return jnp.dot(a, b, preferred_element_type=F32)


def _split(x):
    hi = x.astype(BF16)
    lo = (x - hi.astype(F32)).astype(BF16)
    return hi, lo


def _dot3(ah, al, bh, bl):
    return _dot(ah, bh) + (_dot(ah, bl) + _dot(al, bh))


def _sigmoid(x):
    return 1.0 / (1.0 + jnp.exp2(x * (-math.log2(math.e))))


def _mod_rows(m_ref, k, row, ctx_start):
    lat = m_ref[0, k:k + 1, :]
    if ctx_start is None:
        return lat
    return jnp.where(row >= ctx_start, m_ref[1, k:k + 1, :], lat)


def _strided_rows(ref3, start, stride):
    return jnp.concatenate([ref3[cb, pl.ds(start, 8, stride=stride), :] for cb in range(ref3.shape[0])],
                           axis=-1)


def _perm_rows(ref3, base, nrows, r):
    return _strided_rows(ref3, base + r, nrows // 8)


def _unperm_rows(ref3, base, nrows, q):
    s, r0 = divmod(8 * q, nrows // 8)
    return _strided_rows(ref3, base + 8 * r0 + s, 8)


def _mod_kernel(c_ref, w_ref, b_ref, o_ref):
    c = c_ref[...]
    s = (c * _sigmoid(c)).astype(BF16)
    o_ref[...] = _dot(s, w_ref[...].astype(BF16)) + b_ref[...]


def modulation(cvec, ada_w, ada_b):
    depth = ada_w.shape[0]
    tn = 1024
    return pl.pallas_call(
        _mod_kernel,
        grid=(depth, 6 * D // tn),
        in_specs=[pl.BlockSpec((16, D), lambda l, j: (0, 0)),
                  pl.BlockSpec((None, D, tn), lambda l, j: (l, 0, j)),
                  pl.BlockSpec((None, 1, tn), lambda l, j: (l, 0, j))],
        out_specs=pl.BlockSpec((None, 16, tn), lambda l, j: (l, 0, j)),
        out_shape=jax.ShapeDtypeStruct((depth, 16, 6 * D), F32),
        compiler_params=_cp(("parallel", "parallel")),
        name="modulation",
    )(cvec, ada_w, ada_b.reshape(depth, 1, 6 * D))


def _even_in_kernel(xa_ref, xb_ref, xc_ref, ctx_ref, g_ref, m_ref, w_ref, z_ref, xs_ref, h_ref, *, tm, nchunk):
    i = pl.program_id(1)
    third = jnp.where(i == T // tm - 1, ctx_ref[...], xc_ref[...])
    x = jnp.concatenate([xa_ref[...], xb_ref[...], third], axis=0)
    xs_ref[...] = x
    ms = jnp.mean(x * x, axis=-1, keepdims=True)
    y = x * lax.rsqrt(ms + EPS) * g_ref[...]
    row = i * tm + lax.broadcasted_iota(jnp.int32, (tm, 1), 0)
    h_ref[...] = (y * (1.0 + _mod_rows(m_ref, 1, row, S)) + _mod_rows(m_ref, 0, row, S)).astype(BF16)
    n = z_ref.shape[-1]
    for n0 in range(0, n, nchunk):
        z_ref[:, n0:n0 + nchunk] = _dot(h_ref[...], w_ref[:, n0:n0 + nchunk])


def even_in_proj(x, ctx, g, mod, w):
    tm = 3 * C
    n = w.shape[1]
    last = S // C - 1
    xblk = lambda k: pl.BlockSpec((None, C, D), lambda b, i: (b, jnp.minimum(3 * i + k, last), 0))
    return pl.pallas_call(
        functools.partial(_even_in_kernel, tm=tm, nchunk=512),
        grid=(B, T // tm),
        in_specs=[xblk(0), xblk(1), xblk(2),
                  pl.BlockSpec((None, C, D), lambda b, i: (b, 0, 0)),
                  pl.BlockSpec((1, D), lambda b, i: (0, 0)),
                  pl.BlockSpec((None, 2, 6, D), lambda b, i: (b, 0, 0, 0)),
                  pl.BlockSpec((D, n), lambda b, i: (0, 0))],
        out_specs=[pl.BlockSpec((None, tm, n), lambda b, i: (b, i, 0)),
                   pl.BlockSpec((None, tm, D), lambda b, i: (b, i, 0))],
        out_shape=[jax.ShapeDtypeStruct((B, T, n), F32), jax.ShapeDtypeStruct((B, T, D), F32)],
        scratch_shapes=[pltpu.VMEM((tm, D), BF16)],
        compiler_params=_cp(("parallel", "parallel")),
        name="even_in_proj",
    )(x, x, x, ctx, g.reshape(1, D), mod, w)


def _ffn_kernel(*refs, nparts, tm, t, sb, seg_starts, ctx_start, final, ck):
    x_ref, xp_ref, xn_ref = refs[:3]
    parts = [refs[3 + 3 * p:6 + 3 * p] for p in range(nparts)]
    (wo_ref, g_ref, m_ref, wu_ref, cw_ref, cb_ref, wd_ref, fg_ref, o_ref,
     h_ref, x3_ref, acc_ref) = refs[3 + 3 * nparts:]
    i = pl.program_id(1)
    nsub = tm // sb
    sub8 = lax.broadcasted_iota(jnp.int32, (8, 1), 0)
    g = g_ref[...]

    def normmod(x, row):
        ms = jnp.mean(x * x, axis=-1, keepdims=True)
        y = x * lax.rsqrt(ms + EPS) * g
        return y * (1.0 + _mod_rows(m_ref, 4, row, ctx_start)) + _mod_rows(m_ref, 3, row, ctx_start)

    mix = None
    k0 = 0
    for p_ref, pp_ref, pn_ref in parts:
        kp = p_ref.shape[-1]
        d = _dot(jnp.concatenate([p_ref[...], pp_ref[...], pn_ref[...]], axis=0), wo_ref[k0:k0 + kp, :])
        mix = d if mix is None else mix + d
        k0 += kp
    row = i * tm + lax.broadcasted_iota(jnp.int32, (tm, 1), 0)
    x1 = x_ref[...] + _mod_rows(m_ref, 2, row, ctx_start) * mix[:tm]
    x1p = xp_ref[...] + _mod_rows(m_ref, 2, i * tm - 1, ctx_start) * mix[tm + 8:tm + 16]
    x1n = xn_ref[...] + _mod_rows(m_ref, 2, (i + 1) * tm, ctx_start) * mix[tm + 16:tm + 24]

    for cb in range(D // 128):
        x3_ref[cb] = x1[:, cb * 128:(cb + 1) * 128]
    for j in range(nsub):
        for r in range(0, sb // 8, 2):
            x = jnp.concatenate([_perm_rows(x3_ref, j * sb, sb, r), _perm_rows(x3_ref, j * sb, sb, r + 1)], axis=0)
            h_ref[j * sb + 8 * r:j * sb + 8 * r + 16, :] = normmod(x, i * tm + j * sb).astype(BF16)
    h_ref[tm:tm + 16, :] = jnp.concatenate([normmod(x1p, i * tm - 1), normmod(x1n, (i + 1) * tm)],
                                           axis=0).astype(BF16)

    keep_prev = []
    keep_next = []
    for j in range(nsub):
        row0 = i * tm + j * sb
        kp = row0 != seg_starts[0]
        kn = row0 + sb != t
        for s0 in seg_starts[1:]:
            kp = kp & (row0 != s0)
            kn = kn & (row0 + sb != s0)
        keep_prev.append(jnp.where(kp, 1.0, 0.0))
        keep_next.append(jnp.where(kn, 1.0, 0.0))

    def conv(u, j, c0):
        r0 = j * sb
        cur = u[r0:r0 + sb]
        hp = (u[tm + 7:tm + 8] if j == 0 else u[r0 - 1:r0]) * keep_prev[j]
        hn = (u[tm + 8:tm + 9] if j == nsub - 1 else u[r0 + sb:r0 + sb + 1]) * keep_next[j]
        p0 = jnp.where(sub8 == 0, hp, pltpu.roll(cur[sb - 8:sb], 1, 0))
        nl = jnp.where(sub8 == 7, hn, pltpu.roll(cur[0:8], 7, 0))
        prev = jnp.concatenate([p0, cur[:sb - 8]], axis=0)
        nxt = jnp.concatenate([cur[8:], nl], axis=0)
        w = cw_ref[:, c0:c0 + ck]
        return prev * w[0:1] + cur * w[1:2] + nxt * w[2:3] + cb_ref[:, c0:c0 + ck]

    h = h_ref[...]

    def up(c0):
        return _dot(h, wu_ref[:, c0:c0 + ck]), _dot(h, wu_ref[:, D_FF + c0:D_FF + c0 + ck])

    ahead = up(0)
    for k, c0 in enumerate(range(0, D_FF, ck)):
        ug, uv = ahead
        if c0 + ck < D_FF:
            ahead = up(c0 + ck)
        parts = []
        for j in range(nsub):
            gt = conv(ug, j, c0)
            vl = conv(uv, j, D_FF + c0)
            parts.append((gt * _sigmoid(gt) * vl).astype(BF16))
        a = parts[0] if nsub == 1 else jnp.concatenate(parts, axis=0)
        d = _dot(a, wd_ref[c0:c0 + ck, :])
        for cb in range(D // 128):
            if k == 0:
                acc_ref[cb] = d[:, cb * 128:(cb + 1) * 128]
            else:
                acc_ref[cb] += d[:, cb * 128:(cb + 1) * 128]
    fg = fg_ref[...]
    for j in range(nsub):
        gate = _mod_rows(m_ref, 5, i * tm + j * sb, ctx_start)
        for q in range(sb // 8):
            rows = slice(j * sb + 8 * q, j * sb + 8 * q + 8)
            x1r = jnp.concatenate([x3_ref[cb, rows, :] for cb in range(D // 128)], axis=-1)
            out = x1r + gate * _unperm_rows(acc_ref, j * sb, sb, q)
            if final:
                ms = jnp.mean(out * out, axis=-1, keepdims=True)
                out = out * lax.rsqrt(ms + EPS) * fg
            o_ref[rows, :] = out


def mixer_out_conv_ffn(xs, t, parts, w_out, g, mod, w_up, conv_w, conv_b, w_down, final_g, tm, sb, seg_starts,
                       ctx_start, final, name):
    bsz = xs.shape[0]
    f2 = 2 * D_FF
    r8 = tm // 8
    r16 = tm // 16
    kern = functools.partial(_ffn_kernel, nparts=len(parts), tm=tm, t=t, sb=sb, seg_starts=seg_starts,
                             ctx_start=ctx_start, final=final, ck=256)
    const = lambda shape: pl.BlockSpec(shape, lambda b, i: (0,) * len(shape), pipeline_mode=pl.Buffered(1))
    in_specs = [pl.BlockSpec((None, tm, D), lambda b, i: (b, i, 0)),
                pl.BlockSpec((None, 8, D), lambda b, i: (b, jnp.maximum(i * r8 - 1, 0), 0)),
                pl.BlockSpec((None, 8, D), lambda b, i: (b, jnp.minimum((i + 1) * r8, t // 8 - 1), 0))]
    args = [xs, xs, xs]
    for p in parts:
        kp = p.shape[-1]
        in_specs += [pl.BlockSpec((None, tm, kp), lambda b, i: (b, i, 0)),
                     pl.BlockSpec((None, 16, kp), lambda b, i: (b, jnp.maximum(i * r16 - 1, 0), 0)),
                     pl.BlockSpec((None, 16, kp), lambda b, i: (b, jnp.minimum((i + 1) * r16, t // 16 - 1), 0))]
        args += [p, p, p]
    in_specs += [const(w_out.shape), const((1, D)),
                 pl.BlockSpec((None, 2, 6, D), lambda b, i: (b, 0, 0, 0)),
                 const((D, f2)), const((3, f2)), const((1, f2)), const((D_FF, D)), const((1, D))]
    args += [w_out, g.reshape(1, D), mod, w_up, conv_w, conv_b.reshape(1, f2), w_down, final_g.reshape(1, D)]
    return pl.pallas_call(
        kern,
        grid=(bsz, t // tm),
        in_specs=in_specs,
        out_specs=pl.BlockSpec((None, tm, D), lambda b, i: (b, i, 0)),
        out_shape=jax.ShapeDtypeStruct((bsz, t, D), F32),
        scratch_shapes=[pltpu.VMEM((tm + 16, D), BF16),
                        pltpu.VMEM((D // 128, tm, 128), F32), pltpu.VMEM((D // 128, tm, 128), F32)],
        compiler_params=_cp(("parallel", "parallel")),
        name=name,
    )(*args)


PAD = 8


def _fill_padded(xp_ref, x_ref, width):
    z = jnp.zeros((PAD, width), F32)
    xp_ref[0:PAD, :] = z
    xp_ref[PAD:PAD + S, :] = x_ref[0:S, :].astype(F32)
    xp_ref[PAD + S:2 * PAD + S, :] = z
    xp_ref[2 * PAD + S:2 * PAD + T, :] = x_ref[S:T, :].astype(F32)
    xp_ref[2 * PAD + T:3 * PAD + T, :] = z


def _conv_tile(xp_ref, w, bias, r0, rows, ktaps):
    base = PAD + r0 if r0 < S else 2 * PAD + r0
    left = ktaps // 2
    acc = bias
    for j in range(ktaps):
        acc = acc + xp_ref[base + j - left:base + j - left + rows, :] * w[j:j + 1, :]
    return acc


def _hconv_kernel(z_ref, w_ref, b_ref, o_ref, xp_ref):
    _fill_padded(xp_ref, z_ref, 512)
    w = w_ref[...]
    bias = b_ref[...]
    for r0 in range(0, T, 256):
        o_ref[r0:r0 + 256, :] = _conv_tile(xp_ref, w, bias, r0, 256, 3)


def hyena_short_conv(z, conv_w, conv_b):
    return pl.pallas_call(
        _hconv_kernel,
        grid=(B, 3),
        in_specs=[pl.BlockSpec((None, T, 512), lambda b, j: (b, 0, 2 + j)),
                  pl.BlockSpec((3, 512), lambda b, j: (0, j)),
                  pl.BlockSpec((1, 512), lambda b, j: (0, j))],
        out_specs=pl.BlockSpec((None, T, 512), lambda b, j: (b, 0, j)),
        out_shape=jax.ShapeDtypeStruct((B, T, 3 * HY_W), F32),
        scratch_shapes=[pltpu.VMEM((T + 3 * PAD, 512), F32)],
        compiler_params=_cp(("parallel", "parallel")),
        name="hyena_short_conv",
    )(z, conv_w, conv_b.reshape(1, 3 * HY_W))


RGH = 128
SEGS = ((S, C, C // 8), (0, S, S // 8))
POFF = {S: 0, 0: C + 24}


def _rg_kernel(x_ref, gt_ref, cw_ref, cb_ref, wbd_ref, bias_ref, sp_ref, o_ref,
               xp_ref, af_ref, bf_ref, ar_ref, br_ref, cf_ref, cr_ref):
    sub8 = lax.broadcasted_iota(jnp.int32, (8, 1), 0)
    cw = cw_ref[...]
    cb = cb_ref[...]
    nl2e8 = -RG_C * math.log2(math.e)
    spf = sp_ref[0:1, :] * nl2e8
    spr = sp_ref[1:2, :] * nl2e8

    for seg0, rows, nslot in SEGS:
        p0 = POFF[seg0]

        def fill(r, _):
            xp_ref[pl.ds(pl.multiple_of(p0 + 16 + 8 * r, 8), 8), :] = (
                x_ref[pl.ds(seg0 + r, 8, stride=nslot), :].astype(F32))
            return 0

        lax.fori_loop(0, nslot, fill, 0, unroll=8)
        for slot, src, sh in ((0, nslot - 2, 1), (1, nslot - 1, 1), (nslot + 2, 0, 7)):
            v = pltpu.roll(xp_ref[p0 + 16 + 8 * src:p0 + 24 + 8 * src, :], sh, 0)
            xp_ref[p0 + 8 * slot:p0 + 8 * slot + 8, :] = jnp.where(sub8 == (0 if sh == 1 else 7), 0.0, v)

        for t0 in range(0, rows, 256):
            xc = cb
            for j in range(4):
                xc = xc + xp_ref[p0 + 8 * j + t0:p0 + 8 * j + t0 + 256, :] * cw[j:j + 1, :]
            g = _dot(xc.astype(BF16), wbd_ref[...])
            for d, (a_ref, b_ref, sp) in enumerate(((af_ref, bf_ref, spf), (ar_ref, br_ref, spr))):
                rg = _sigmoid(g[:, (2 * d) * RGH:(2 * d + 1) * RGH] + bias_ref[2 * d:2 * d + 1, :])
                ig = _sigmoid(g[:, (2 * d + 1) * RGH:(2 * d + 2) * RGH] + bias_ref[2 * d + 1:2 * d + 2, :])
                a = jnp.exp2(rg * sp)
                a_ref[seg0 + t0:seg0 + t0 + 256, :] = a
                om = 1.0 - a * a
                root = jnp.where(om > 0.0, om * lax.rsqrt(om), 0.0)
                b_ref[seg0 + t0:seg0 + t0 + 256, :] = root * (ig * xc)

    one = jnp.ones((8, RGH), F32)
    zero = jnp.zeros((8, RGH), F32)
    for seg0, rows, nslot in SEGS:
        def body(i, carry):
            pf, hf, pr, hr = carry
            tf = pl.multiple_of(seg0 + 8 * i, 8)
            tr = pl.multiple_of(seg0 + 8 * (nslot - 1 - i), 8)
            a = af_ref[pl.ds(tf, 8), :]
            hf = a * hf + bf_ref[pl.ds(tf, 8), :]
            pf = a * pf
            af_ref[pl.ds(tf, 8), :] = pf
            bf_ref[pl.ds(tf, 8), :] = hf
            a = ar_ref[pl.ds(tr, 8), :]
            hr = a * hr + br_ref[pl.ds(tr, 8), :]
            pr = a * pr
            ar_ref[pl.ds(tr, 8), :] = pr
            br_ref[pl.ds(tr, 8), :] = hr
            return pf, hf, pr, hr

        lax.fori_loop(0, nslot, body, (one, zero, one, zero), unroll=4)

    cf = jnp.zeros((1, RGH), F32)
    cr = jnp.zeros((1, RGH), F32)
    for k, (seg0, rows, nslot) in enumerate(SEGS):
        last = seg0 + 8 * (nslot - 1)
        for s in range(8):
            cf_ref[8 * k + s:8 * k + s + 1, :] = cf
            cf = bf_ref[last + s:last + s + 1, :] + af_ref[last + s:last + s + 1, :] * cf
        for s in range(7, -1, -1):
            cr_ref[8 * k + s:8 * k + s + 1, :] = cr
            cr = br_ref[seg0 + s:seg0 + s + 1, :] + ar_ref[seg0 + s:seg0 + s + 1, :] * cr

    for k, (seg0, rows, nslot) in enumerate(SEGS):
        reps = min(rows, 256) // 8
        cft = jnp.concatenate([cf_ref[8 * k:8 * k + 8, :]] * reps, axis=0)
        crt = jnp.concatenate([cr_ref[8 * k:8 * k + 8, :]] * reps, axis=0)
        for t0 in range(0, rows, 256):
            sl = slice(seg0 + t0, seg0 + t0 + 256)
            bf_ref[sl, :] = (bf_ref[sl, :] + af_ref[sl, :] * cft) + (br_ref[sl, :] + ar_ref[sl, :] * crt)

    for seg0, rows, nslot in SEGS:
        per = nslot // 8

        def unperm(q, _):
            s = lax.shift_right_logical(q, per.bit_length() - 1)
            r0 = q & (per - 1)
            af_ref[pl.ds(pl.multiple_of(seg0 + 8 * q, 8), 8), :] = (
                bf_ref[pl.ds(seg0 + 64 * r0 + s, 8, stride=8), :])
            return 0

        lax.fori_loop(0, rows // 8, unperm, 0, unroll=8)

    c0 = math.sqrt(2.0 / math.pi)
    for r0 in range(0, T, 256):
        gt = gt_ref[r0:r0 + 256, :].astype(F32)
        gelu = 0.5 * gt * (1.0 + jnp.tanh(c0 * (gt + 0.044715 * (gt * gt * gt))))
        o_ref[r0:r0 + 256, :] = (af_ref[r0:r0 + 256, :] * gelu).astype(o_ref.dtype)


def rglru(z, conv_w, conv_b, wbd, bias4, softplus_neg_lam):
    nh = RG_W // RGH
    return pl.pallas_call(
        _rg_kernel,
        grid=(B, nh),
        in_specs=[pl.BlockSpec((None, T, RGH), lambda b, j: (b, 0, j)),
                  pl.BlockSpec((None, T, RGH), lambda b, j: (b, 0, nh + j)),
                  pl.BlockSpec((4, RGH), lambda b, j: (0, j)),
                  pl.BlockSpec((1, RGH), lambda b, j: (0, j)),
                  pl.BlockSpec((None, RGH, 4 * RGH), lambda b, j: (j, 0, 0)),
                  pl.BlockSpec((4, RGH), lambda b, j: (0, j)),
                  pl.BlockSpec((2, RGH), lambda b, j: (0, j))],
        out_specs=pl.BlockSpec((None, T, RGH), lambda b, j: (b, 0, j)),
        out_shape=jax.ShapeDtypeStruct((B, T, RG_W), BF16),
        scratch_shapes=[pltpu.VMEM((T + 48, RGH), F32)] + [pltpu.VMEM((T, RGH), F32)] * 4
                       + [pltpu.VMEM((16, RGH), F32)] * 2,
        compiler_params=_cp(("parallel", "parallel")),
        name="rglru",
    )(z, z, conv_w, conv_b.reshape(1, RG_W), wbd, bias4, softplus_neg_lam)


def _softplus_kernel(x_ref, o_ref):
    y = -x_ref[...]
    o_ref[...] = jnp.maximum(y, 0.0) + jnp.log(1.0 + jnp.exp(-jnp.abs(y)))


def softplus_neg(lam):
    return pl.pallas_call(
        _softplus_kernel,
        out_shape=jax.ShapeDtypeStruct(lam.shape, F32),
        name="softplus_neg",
    )(lam)


def _hfilt_kernel(z_ref, w1_ref, b1_ref, fr_ref, w2_ref, b2_ref, w3_ref, dec_ref, x1_ref, x2_ref, nyq_ref,
                  s1_ref, s2_ref, *, n, tr):
    fr = fr_ref[...]
    w1h, w1l = _split(w1_ref[...])
    w2h, w2l = _split(w2_ref[...])
    w3h, w3l = _split(w3_ref[...])
    tot = jnp.zeros((1, HY_W), F32)
    alt = jnp.zeros((1, HY_W), F32)
    sign = jnp.where(lax.broadcasted_iota(jnp.int32, (tr, 1), 0) % 2 == 0, 1.0, -1.0)
    for r0 in range(0, n, tr):
        zh, zl = _split(z_ref[r0:r0 + tr, :])
        h = jnp.sin(fr * (_dot3(zh, zl, w1h, w1l) + b1_ref[...]))
        hh, hl = _split(h)
        h = jnp.sin(fr * (_dot3(hh, hl, w2h, w2l) + b2_ref[...]))
        hh, hl = _split(h)
        h = _dot3(hh, hl, w3h, w3l)
        dec = dec_ref[r0:r0 + tr, :]
        h0 = h[:, :HY_W] * dec
        h1 = h[:, HY_W:] * dec
        if r0 == 0:
            h1 = jnp.where(lax.broadcasted_iota(jnp.int32, (tr, 1), 0) == 0, 0.0, h1)
        tot = tot + jnp.sum(jnp.abs(h0) + jnp.abs(h1), axis=0, keepdims=True)
        alt = alt + jnp.sum((h0 + h1) * sign, axis=0, keepdims=True)
        s1_ref[r0:r0 + tr, :] = h0 + h1
        s2_ref[r0:r0 + tr, :] = h0 - h1
    inv = 1.0 / tot
    for r0 in range(0, n, tr):
        x1_ref[r0:r0 + tr, :] = (s1_ref[r0:r0 + tr, :] * inv).astype(BF16)
        x2_ref[r0:r0 + tr, :] = (s2_ref[r0:r0 + tr, :] * inv).astype(BF16)
    nyq_ref[...] = jnp.broadcast_to(alt * inv, (8, HY_W))


def hyena_filter_halves(n, zemb, w1p, b1, freq, w2, b2, w3, decay):
    kern = functools.partial(_hfilt_kernel, n=n, tr=256)
    full = lambda shape: pl.BlockSpec(shape, lambda o: (0,) * len(shape))
    return pl.pallas_call(
        kern,
        grid=(2,),
        in_specs=[full((n, 128)), full((128, HY_HID)), full((1, HY_HID)), full((1, HY_HID)),
                  full((HY_HID, HY_HID)), full((1, HY_HID)),
                  pl.BlockSpec((HY_HID, 2 * HY_W), lambda o: (0, o)),
                  full((n, HY_W))],
        out_specs=[pl.BlockSpec((n, HY_W), lambda o: (0, o)),
                   pl.BlockSpec((n, HY_W), lambda o: (0, o)),
                   pl.BlockSpec((8, HY_W), lambda o: (0, o))],
        out_shape=[jax.ShapeDtypeStruct((n, 2 * HY_W), BF16),
                   jax.ShapeDtypeStruct((n, 2 * HY_W), BF16),
                   jax.ShapeDtypeStruct((8, 2 * HY_W), F32)],
        scratch_shapes=[pltpu.VMEM((n, HY_W), F32), pltpu.VMEM((n, HY_W), F32)],
        compiler_params=_cp(("parallel",)),
        name=f"hyena_filter_{n}",
    )(zemb, w1p, b1.reshape(1, HY_HID), freq.reshape(1, HY_HID), w2, b2.reshape(1, HY_HID), w3, decay)


def _kspec_kernel(at_ref, ab_ref, x1_ref, x2_ref, kt_ref, kb_ref):
    kt_ref[...] = _dot(at_ref[...], x1_ref[...])
    kb_ref[...] = _dot(ab_ref[...], x2_ref[...])


def filter_spectrum(fwd, x1, x2, n, tr):
    nt = n // tr
    return pl.pallas_call(
        _kspec_kernel,
        grid=(nt,),
        in_specs=[pl.BlockSpec((tr, n), lambda i: (i, 0)),
                  pl.BlockSpec((tr, n), lambda i: (nt + i, 0)),
                  pl.BlockSpec((n, 2 * HY_W), lambda i: (0, 0)),
                  pl.BlockSpec((n, 2 * HY_W), lambda i: (0, 0))],
        out_specs=[pl.BlockSpec((tr, 2 * HY_W), lambda i: (i, 0)),
                   pl.BlockSpec((tr, 2 * HY_W), lambda i: (i, 0))],
        out_shape=[jax.ShapeDtypeStruct((n, 2 * HY_W), F32)] * 2,
        compiler_params=_cp(("parallel",)),
        name=f"filter_spectrum_{n}",
    )(fwd, fwd, x1, x2)


def _hfwd_kernel(at_ref, ab_ref, x_ref, kt_ref, kb_ref, kn_ref, yr_ref, yi_ref, xb_ref, *, tr):
    i = pl.program_id(1)

    @pl.when(i == 0)
    def _():
        xb_ref[...] = x_ref[...].astype(BF16)

    u = _dot(jnp.concatenate([at_ref[...], ab_ref[...]], axis=0), xb_ref[...])
    ur = u[:tr]
    ui = u[tr:]
    kr = kt_ref[...]
    ki = kb_ref[...]
    first = (lax.broadcasted_iota(jnp.int32, (tr, 1), 0) == 0) & (i == 0)
    yr_ref[...] = (ur * kr - jnp.where(first, 0.0, ui * ki)).astype(BF16)
    yi_ref[...] = jnp.where(first, ui * kn_ref[0:1, :], ur * ki + ui * kr).astype(BF16)


def hyena_forward(fwd, x, x_rb, kt, kb, nyq, order, n, tr, name):
    nt = n // tr
    kern = functools.partial(_hfwd_kernel, tr=tr)
    return pl.pallas_call(
        kern,
        grid=(B, nt),
        in_specs=[pl.BlockSpec((tr, n), lambda b, i: (i, 0)),
                  pl.BlockSpec((tr, n), lambda b, i: (nt + i, 0)),
                  pl.BlockSpec((None, n, HY_W), lambda b, i: (b, x_rb, 0)),
                  pl.BlockSpec((tr, HY_W), lambda b, i: (i, order)),
                  pl.BlockSpec((tr, HY_W), lambda b, i: (i, order)),
                  pl.BlockSpec((8, HY_W), lambda b, i: (0, order))],
        out_specs=[pl.BlockSpec((None, tr, HY_W), lambda b, i: (b, i, 0)),
                   pl.BlockSpec((None, tr, HY_W), lambda b, i: (b, i, 0))],
        out_shape=[jax.ShapeDtypeStruct((B, n, HY_W), BF16)] * 2,
        scratch_shapes=[pltpu.VMEM((n, HY_W), BF16)],
        compiler_params=_cp(("parallel", "arbitrary")),
        name=name,
    )(fwd, fwd, x, kt, kb, nyq)


def _hinv_kernel(al_ref, ar_ref, yr_ref, yi_ref, up_ref, gt_ref, bias_ref, o_ref):
    y = _dot(al_ref[...], yr_ref[...]) + _dot(ar_ref[...], yi_ref[...])
    o_ref[...] = (gt_ref[...] * (y + up_ref[...].astype(F32) * bias_ref[...])).astype(o_ref.dtype)


def hyena_inverse(inv, yr, yi, uprev, uprev_rb, gate, gate_rb, gate_cb, bias, n, tmi, out_dtype, name):
    return pl.pallas_call(
        _hinv_kernel,
        grid=(B, n // tmi),
        in_specs=[pl.BlockSpec((tmi, n), lambda b, i: (i, 0)),
                  pl.BlockSpec((tmi, n), lambda b, i: (i, 1)),
                  pl.BlockSpec((None, n, HY_W), lambda b, i: (b, 0, 0)),
                  pl.BlockSpec((None, n, HY_W), lambda b, i: (b, 0, 0)),
                  pl.BlockSpec((None, tmi, HY_W), lambda b, i: (b, uprev_rb + i, 0)),
                  pl.BlockSpec((None, tmi, HY_W), lambda b, i: (b, gate_rb + i, gate_cb)),
                  pl.BlockSpec((1, HY_W), lambda b, i: (0, 0))],
        out_specs=pl.BlockSpec((None, tmi, HY_W), lambda b, i: (b, i, 0)),
        out_shape=jax.ShapeDtypeStruct((B, n, HY_W), out_dtype),
        compiler_params=_cp(("parallel", "parallel")),
        name=name,
    )(inv, inv, yr, yi, uprev, gate, bias.reshape(1, HY_W))


def _rope128(r, c, sa, sb):
    return r * c + pltpu.roll(r, 32, 1) * sa + pltpu.roll(r, 96, 1) * sb


def _qkv_kernel(x_ref, g_ref, m_ref, win_ref, qg_ref, kvg_ref, wq_ref, wk_ref, wv_ref, c_ref, sa_ref, sb_ref,
                q_ref, k_ref, v_ref, *, tm):
    x = x_ref[...]
    y = x * lax.rsqrt(jnp.mean(x * x, axis=-1, keepdims=True) + EPS) * g_ref[...]
    row = pl.program_id(1) * tm + lax.broadcasted_iota(jnp.int32, (tm, 1), 0)
    h = (y * (1.0 + _mod_rows(m_ref, 1, row, S)) + _mod_rows(m_ref, 0, row, S)).astype(BF16)
    z = _dot(h, win_ref[...])
    c = c_ref[...]
    sa = sa_ref[...]
    sb = sb_ref[...]
    zq = z[:, :Q_LORA]
    qn = (zq * lax.rsqrt(jnp.mean(zq * zq, axis=-1, keepdims=True) + EPS) * qg_ref[...]).astype(BF16)
    zkv = z[:, Q_LORA:Q_LORA + KV_LORA]
    ckv = (zkv * lax.rsqrt(jnp.mean(zkv * zkv, axis=-1, keepdims=True) + EPS) * kvg_ref[...]).astype(BF16)
    kr = _rope128(z[:, Q_LORA + KV_LORA:], c, sa, sb).astype(BF16)
    for h in range(HEADS):
        q = _dot(qn, wq_ref[:, h * HQ:(h + 1) * HQ])
        q_ref[:, h * HQ:h * HQ + NOPE] = q[:, :NOPE].astype(BF16)
        q_ref[:, h * HQ + NOPE:(h + 1) * HQ] = _rope128(q[:, NOPE:], c, sa, sb).astype(BF16)
        k_ref[:, h * HQ:h * HQ + NOPE] = _dot(ckv, wk_ref[:, h * NOPE:(h + 1) * NOPE]).astype(BF16)
        k_ref[:, h * HQ + NOPE:(h + 1) * HQ] = kr
    v_ref[...] = _dot(ckv, wv_ref[...]).astype(BF16)


def mla_qkv(xs, g, mod, w_in, q_g, kv_g, wq, wk, wv, ctab, satab, sbtab, tm):
    full = lambda shape: pl.BlockSpec(shape, lambda b, i: (0,) * len(shape))
    tab = pl.BlockSpec((tm, 128), lambda b, i: (i, 0))
    return pl.pallas_call(
        functools.partial(_qkv_kernel, tm=tm),
        grid=(B, T // tm),
        in_specs=[pl.BlockSpec((None, tm, D), lambda b, i: (b, i, 0)),
                  full((1, D)),
                  pl.BlockSpec((None, 2, 6, D), lambda b, i: (b, 0, 0, 0)),
                  full((D, ODD_IN_P)),
                  full((1, Q_LORA)), full((1, KV_LORA)),
                  full((Q_LORA, HEADS * HQ)), full((KV_LORA, HEADS * NOPE)), full((KV_LORA, HEADS * VH)),
                  tab, tab, tab],
        out_specs=[pl.BlockSpec((None, tm, HEADS * HQ), lambda b, i: (b, i, 0)),
                   pl.BlockSpec((None, tm, HEADS * HQ), lambda b, i: (b, i, 0)),
                   pl.BlockSpec((None, tm, HEADS * VH), lambda b, i: (b, i, 0))],
        out_shape=[jax.ShapeDtypeStruct((B, T, HEADS * HQ), BF16),
                   jax.ShapeDtypeStruct((B, T, HEADS * HQ), BF16),
                   jax.ShapeDtypeStruct((B, T, HEADS * VH), BF16)],
        compiler_params=_cp(("parallel", "parallel")),
        name="mla_qkv",
    )(xs, g.reshape(1, D), mod, w_in, q_g.reshape(1, Q_LORA), kv_g.reshape(1, KV_LORA), wq, wk, wv,
      ctab, satab, sbtab)


def _attn_kernel(q_ref, k_ref, v_ref, o_ref, *, scale, ts):
    c = scale * math.log2(math.e)
    k = k_ref[...]
    kc = 256
    def scores(r0):
        return lax.dot_general(q_ref[r0:r0 + ts, :], k, (((1,), (1,)), ((), ())), preferred_element_type=F32)

    tq = q_ref.shape[0]
    ahead = scores(0)
    for r0 in range(0, tq, ts):
        s = ahead
        if r0 + ts < tq:
            ahead = scores(r0 + ts)
        m = jnp.max(s, axis=-1, keepdims=True)
        acc = None
        lv = None
        for c0 in range(0, T, kc):
            p = jnp.exp2((s[:, c0:c0 + kc] - m) * c)
            pl_ = p[:, :128] + p[:, 128:]
            lv = pl_ if lv is None else lv + pl_
            d = _dot(p.astype(BF16), v_ref[c0:c0 + kc, :])
            acc = d if acc is None else acc + d
        l = jnp.sum(lv, axis=-1, keepdims=True)
        o_ref[r0:r0 + ts, :] = (acc * (1.0 / l)).astype(o_ref.dtype)


def mla_attention(q, k, v, tq):
    kern = functools.partial(_attn_kernel, scale=(NOPE + ROPE) ** -0.5, ts=512)
    return pl.pallas_call(
        kern,
        grid=(B, HEADS, S // tq),
        in_specs=[pl.BlockSpec((None, tq, HQ), lambda b, h, i: (b, i, h)),
                  pl.BlockSpec((None, T, HQ), lambda b, h, i: (b, 0, h)),
                  pl.BlockSpec((None, T, VH), lambda b, h, i: (b, 0, h))],
        out_specs=pl.BlockSpec((None, tq, VH), lambda b, h, i: (b, i, h)),
        out_shape=jax.ShapeDtypeStruct((B, S, HEADS * VH), BF16),
        compiler_params=_cp(("parallel", "parallel", "parallel")),
        name="mla_attention",
    )(q, k, v)


@functools.lru_cache(maxsize=None)
def _dft_tables(n):
    m = 2 * n
    j = np.arange(n, dtype=np.int64)
    kk = np.arange(n, dtype=np.int64)
    ang = 2.0 * np.pi * ((kk[:, None] * j[None, :]) % m).astype(np.float64) / m
    cosm = np.cos(ang)
    sinm = np.sin(ang)
    fwd = np.concatenate([cosm, -sinm], axis=0)
    fwd[n, :] = np.where(j % 2 == 0, 1.0, -1.0)
    inv = np.concatenate([2.0 * cosm.T, -2.0 * sinm.T], axis=1) / m
    inv[:, 0] = 1.0 / m
    inv[:, n] = np.where(j % 2 == 0, 1.0, -1.0) / m
    return fwd.astype(np.float32).astype(BF16), inv.astype(np.float32).astype(BF16)


def _filter_consts(n):
    pos = jnp.arange(n, dtype=F32)
    t = jnp.linspace(0.0, 1.0, n, dtype=F32)[:, None]
    bands = (HY_EMB - 1) // 2
    w = 2.0 * math.pi * pos / n
    f = jnp.linspace(1e-4, bands - 1, bands, dtype=F32)
    ang = w[:, None] * f[None, :]
    z = jnp.concatenate([t, jnp.cos(ang), -jnp.sin(ang)], axis=-1)
    z = jnp.pad(z, ((0, 0), (0, 128 - HY_EMB)))
    max_decay = math.log(1e-2) / 0.3
    min_decay = math.log(1e-2) / 1.5
    deltas = jnp.linspace(min_decay, max_decay, HY_W, dtype=F32)
    decay = jnp.exp(-t * jnp.abs(deltas)[None, :])
    return z, decay


def _rope_tables():
    rows = S // 64
    row = jnp.repeat(jnp.arange(rows, dtype=F32), 64)
    col = jnp.tile(jnp.arange(64, dtype=F32), rows)
    n_freq = ROPE // 4
    inv = 10000.0 ** (-jnp.arange(n_freq, dtype=F32) / n_freq)
    ang = jnp.concatenate([row[:, None] * inv[None, :], col[:, None] * inv[None, :]], axis=-1)
    cos = jnp.concatenate([jnp.cos(ang), jnp.ones((C, 32), F32)], axis=0)
    sin = jnp.concatenate([jnp.sin(ang), jnp.zeros((C, 32), F32)], axis=0)
    z32 = jnp.zeros((T, 32), F32)
    z64 = jnp.zeros((T, 64), F32)
    ctab = jnp.concatenate([cos, cos, z64], axis=-1)
    satab = jnp.concatenate([z32, sin, z64], axis=-1)
    sbtab = jnp.concatenate([-sin, z32, z64], axis=-1)
    return ctab, satab, sbtab


def _even_layer(x, ctx, mod, norm1_g, w_in, rg_conv_w, rg_conv_b, rg_wa, rg_ba, rg_wx, rg_bx, rg_lambda,
                hy_conv_w, hy_conv_b, f_w1, f_b1, f_freq, f_w2, f_b2, f_w3, hy_bias, w_out):
    z, xs = even_in_proj(x, ctx, norm1_g, mod, w_in.astype(BF16))

    def blockdiag(w):
        per = RGH // RG_HD
        out = jnp.zeros((RG_W // RGH, RGH, RGH), F32)
        for h in range(RG_HEADS):
            g, l = divmod(h, per)
            out = out.at[g, l * RG_HD:(l + 1) * RG_HD, l * RG_HD:(l + 1) * RG_HD].set(w[h])
        return out

    wbd = jnp.concatenate([blockdiag(rg_wa[0]), blockdiag(rg_wx[0]), blockdiag(rg_wa[1]), blockdiag(rg_wx[1])],
                          axis=-1).astype(BF16)
    bias4 = jnp.stack([rg_ba[0], rg_bx[0], rg_ba[1], rg_bx[1]], axis=0)
    a = rglru(z, rg_conv_w, rg_conv_b, wbd, bias4, softplus_neg(rg_lambda))

    vc = hyena_short_conv(z, hy_conv_w, hy_conv_b)
    w1p = jnp.pad(f_w1, ((0, 128 - HY_EMB), (0, 0)))
    outs = []
    for n, rb in ((S, 0), (C, S // C)):
        fwd, inv = _dft_tables(n)
        zemb, decay = _filter_consts(n)
        x1, x2, nyq = hyena_filter_halves(n, zemb, w1p, f_b1, f_freq, f_w2, f_b2, f_w3, decay)
        tr = min(n, 512)
        tmi = min(n, 1024)
        kt, kb = filter_spectrum(fwd, x1, x2, n, tr)
        grb = rb * (n // tmi)
        yr, yi = hyena_forward(fwd, vc, rb, kt, kb, nyq, 0, n, tr, f"hyena_fwd1_{n}")
        u1 = hyena_inverse(inv, yr, yi, vc, grb, vc, grb, 1, hy_bias[0], n, tmi, F32, f"hyena_inv1_{n}")
        yr, yi = hyena_forward(fwd, u1, 0, kt, kb, nyq, 1, n, tr, f"hyena_fwd2_{n}")
        outs.append(hyena_inverse(inv, yr, yi, u1, 0, vc, grb, 2, hy_bias[1], n, tmi, BF16,
                                  f"hyena_inv2_{n}"))
    return ([a, jnp.concatenate(outs, axis=1)], w_out.astype(BF16)), xs


def _odd_layer(xs, mod, norm1_g, w_in, q_g, kv_g, w_uq, w_ukv, w_o):
    perm = np.concatenate([np.arange(0, ROPE, 2), np.arange(1, ROPE, 2)])
    w_rope = jnp.pad(w_in[:, Q_LORA + KV_LORA:][:, perm], ((0, 0), (0, 128 - ROPE)))
    w_in_p = jnp.concatenate([w_in[:, :Q_LORA + KV_LORA], w_rope], axis=-1).astype(BF16)
    wq = w_uq.reshape(Q_LORA, HEADS, NOPE + ROPE)
    wq = jnp.concatenate([wq[..., :NOPE], wq[..., NOPE:][..., perm],
                          jnp.zeros((Q_LORA, HEADS, HQ - NOPE - ROPE), F32)], axis=-1)
    wq = wq.reshape(Q_LORA, HEADS * HQ).astype(BF16)
    wkv = w_ukv.reshape(KV_LORA, HEADS, NOPE + VH)
    wk = wkv[..., :NOPE].reshape(KV_LORA, HEADS * NOPE).astype(BF16)
    wv = wkv[..., NOPE:].reshape(KV_LORA, HEADS * VH).astype(BF16)
    ctab, satab, sbtab = _rope_tables()
    q, k, v = mla_qkv(xs, norm1_g, mod, w_in_p, q_g, kv_g, wq, wk, wv, ctab, satab, sbtab, 768)
    return [mla_attention(q, k, v, 2048)], w_o.astype(BF16)


def _ffn(xs, t, mix, mod, norm2_g, w_up, conv_w, conv_b, w_down, final_g, tm, sb, seg_starts, ctx_start, final,
         tag):
    parts, w_out = mix
    return mixer_out_conv_ffn(xs, t, parts, w_out, norm2_g, mod, w_up.astype(BF16), conv_w, conv_b,
                              w_down.astype(BF16), final_g, tm, sb, seg_starts, ctx_start, final,
                              f"mix_out_conv_ffn_{tag}")


def kernel(x, c, ctx, c_ctx, ada_w, ada_b, norm1_g, norm2_g, ev_w_in, ev_rg_conv_w, ev_rg_conv_b, ev_rg_wa, ev_rg_ba, ev_rg_wx, ev_rg_bx, ev_rg_lambda, ev_hy_conv_w, ev_hy_conv_b, ev_hy_f_w1, ev_hy_f_b1, ev_hy_f_freq, ev_hy_f_w2, ev_hy_f_b2, ev_hy_f_w3, ev_hy_bias, ev_w_out, od_w_in, od_q_norm_g, od_kv_norm_g, od_w_uq, od_w_ukv, od_w_o, ffn_w_up, ffn_conv_w, ffn_conv_b, ffn_w_down, final_g):
    cvec = jnp.concatenate([c, c_ctx[None], jnp.zeros((16 - B - 1, D), F32)], axis=0)
    mod = modulation(cvec, ada_w, ada_b)

    def mod_sel(layer):
        lat = mod[layer, :B].reshape(B, 1, 6, D)
        cx = jnp.broadcast_to(mod[layer, B].reshape(1, 1, 6, D), (B, 1, 6, D))
        return jnp.concatenate([lat, cx], axis=1)

    m0 = mod_sel(0)
    mix, xs = _even_layer(x, ctx, m0, norm1_g[0], ev_w_in[0], ev_rg_conv_w[0], ev_rg_conv_b[0], ev_rg_wa[0], ev_rg_ba[0],
                      ev_rg_wx[0], ev_rg_bx[0], ev_rg_lambda[0], ev_hy_conv_w[0], ev_hy_conv_b[0], ev_hy_f_w1[0],
                      ev_hy_f_b1[0], ev_hy_f_freq[0], ev_hy_f_w2[0], ev_hy_f_b2[0], ev_hy_f_w3[0], ev_hy_bias[0],
                      ev_w_out[0])
    xs = _ffn(xs, T, mix, m0, norm2_g[0], ffn_w_up[0], ffn_conv_w[0], ffn_conv_b[0], ffn_w_down[0], final_g,
              384, 128, (0, S), S, False, "0")
    m1 = mod_sel(1)
    mix = _odd_layer(xs, m1, norm1_g[1], od_w_in[0], od_q_norm_g[0], od_kv_norm_g[0], od_w_uq[0], od_w_ukv[0],
                     od_w_o[0])
    return _ffn(xs, S, mix, m1, norm2_g[1], ffn_w_up[1], ffn_conv_w[1], ffn_conv_b[1], ffn_w_down[1], final_g,
                256, 128, (0,), None, True, "1")
```

```python
import functools
import math

import numpy as np
import jax
import jax.numpy as jnp
from jax import lax
from jax.experimental import pallas as pl
from jax.experimental.pallas import tpu as pltpu

F32 = jnp.float32
BF16 = jnp.bfloat16

D = 1024
B = 8
S = 2048
C = 256
T = S + C
EPS = 1e-6

RG_W = 512
RG_HEADS = 8
RG_HD = 64
RG_C = 8.0
HY_W = 512
HY_EMB = 33
HY_HID = 64
EVEN_IN = 2560

HEADS = 8
Q_LORA = 512
KV_LORA = 256
NOPE = 128
ROPE = 64
VH = 128
HQ = 256
ODD_IN_P = 896
D_FF = 2816

VMEM_LIMIT = 52 * 1024 * 1024


def _cp(sem):
    return pltpu.CompilerParams(dimension_semantics=sem, vmem_limit_bytes=VMEM_LIMIT)


def _dot(a, b):
    return jnp.dot(a, b, preferred_element_type=F32)


def _split(x):
    hi = x.astype(BF16)
    lo = (x - hi.astype(F32)).astype(BF16)
    return hi, lo


def _dot3(ah, al, bh, bl):
    return _dot(ah, bh) + (_dot(ah, bl) + _dot(al, bh))


def _sigmoid(x):
    return 1.0 / (1.0 + jnp.exp2(x * (-math.log2(math.e))))


def _mod_rows(m_ref, k, row, ctx_start):
    lat = m_ref[0, k:k + 1, :]
    if ctx_start is None:
        return lat
    return jnp.where(row >= ctx_start, m_ref[1, k:k + 1, :], lat)


def _strided_rows(ref3, start, stride):
    return jnp.concatenate([ref3[cb, pl.ds(start, 8, stride=stride), :] for cb in range(ref3.shape[0])],
                           axis=-1)


def _perm_rows(ref3, base, nrows, r):
    return _strided_rows(ref3, base + r, nrows // 8)


def _unperm_rows(ref3, base, nrows, q):
    s, r0 = divmod(8 * q, nrows // 8)
    return _strided_rows(ref3, base + 8 * r0 + s, 8)


def _mod_kernel(c_ref, w_ref, b_ref, o_ref):
    c = c_ref[...]
    s = (c * _sigmoid(c)).astype(BF16)
    o_ref[...] = _dot(s, w_ref[...].astype(BF16)) + b_ref[...]


def modulation(cvec, ada_w, ada_b):
    depth = ada_w.shape[0]
    tn = 1024
    return pl.pallas_call(
        _mod_kernel,
        grid=(depth, 6 * D // tn),
        in_specs=[pl.BlockSpec((16, D), lambda l, j: (0, 0)),
                  pl.BlockSpec((None, D, tn), lambda l, j: (l, 0, j)),
                  pl.BlockSpec((None, 1, tn), lambda l, j: (l, 0, j))],
        out_specs=pl.BlockSpec((None, 16, tn), lambda l, j: (l, 0, j)),
        out_shape=jax.ShapeDtypeStruct((depth, 16, 6 * D), F32),
        compiler_params=_cp(("parallel", "parallel")),
        name="modulation",
    )(cvec, ada_w, ada_b.reshape(depth, 1, 6 * D))


def _even_in_kernel(xa_ref, xb_ref, xc_ref, ctx_ref, g_ref, m_ref, w_ref, z_ref, xs_ref, h_ref, *, tm, nchunk):
    i = pl.program_id(1)
    third = jnp.where(i == T // tm - 1, ctx_ref[...], xc_ref[...])
    x = jnp.concatenate([xa_ref[...], xb_ref[...], third], axis=0)
    xs_ref[...] = x
    ms = jnp.mean(x * x, axis=-1, keepdims=True)
    y = x * lax.rsqrt(ms + EPS) * g_ref[...]
    row = i * tm + lax.broadcasted_iota(jnp.int32, (tm, 1), 0)
    h_ref[...] = (y * (1.0 + _mod_rows(m_ref, 1, row, S)) + _mod_rows(m_ref, 0, row, S)).astype(BF16)
    n = z_ref.shape[-1]
    for n0 in range(0, n, nchunk):
        z_ref[:, n0:n0 + nchunk] = _dot(h_ref[...], w_ref[:, n0:n0 + nchunk])


def even_in_proj(x, ctx, g, mod, w):
    tm = 3 * C
    n = w.shape[1]
    last = S // C - 1
    xblk = lambda k: pl.BlockSpec((None, C, D), lambda b, i: (b, jnp.minimum(3 * i + k, last), 0))
    return pl.pallas_call(
        functools.partial(_even_in_kernel, tm=tm, nchunk=512),
        grid=(B, T // tm),
        in_specs=[xblk(0), xblk(1), xblk(2),
                  pl.BlockSpec((None, C, D), lambda b, i: (b, 0, 0)),
                  pl.BlockSpec((1, D), lambda b, i: (0, 0)),
                  pl.BlockSpec((None, 2, 6, D), lambda b, i: (b, 0, 0, 0)),
                  pl.BlockSpec((D, n), lambda b, i: (0, 0))],
        out_specs=[pl.BlockSpec((None, tm, n), lambda b, i: (b, i, 0)),
                   pl.BlockSpec((None, tm, D), lambda b, i: (b, i, 0))],
        out_shape=[jax.ShapeDtypeStruct((B, T, n), F32), jax.ShapeDtypeStruct((B, T, D), F32)],
        scratch_shapes=[pltpu.VMEM((tm, D), BF16)],
        compiler_params=_cp(("parallel", "parallel")),
        name="even_in_proj",
    )(x, x, x, ctx, g.reshape(1, D), mod, w)


def _ffn_kernel(*refs, nparts, tm, t, sb, seg_starts, ctx_start, final, ck):
    x_ref, xp_ref, xn_ref = refs[:3]
    parts = [refs[3 + 3 * p:6 + 3 * p] for p in range(nparts)]
    (wo_ref, g_ref, m_ref, wu_ref, cw_ref, cb_ref, wd_ref, fg_ref, o_ref,
     h_ref, x3_ref, acc_ref) = refs[3 + 3 * nparts:]
    i = pl.program_id(1)
    nsub = tm // sb
    sub8 = lax.broadcasted_iota(jnp.int32, (8, 1), 0)
    g = g_ref[...]

    def normmod(x, row):
        ms = jnp.mean(x * x, axis=-1, keepdims=True)
        y = x * lax.rsqrt(ms + EPS) * g
        return y * (1.0 + _mod_rows(m_ref, 4, row, ctx_start)) + _mod_rows(m_ref, 3, row, ctx_start)

    mix = None
    k0 = 0
    for p_ref, pp_ref, pn_ref in parts:
        kp = p_ref.shape[-1]
        d = _dot(jnp.concatenate([p_ref[...], pp_ref[...], pn_ref[...]], axis=0), wo_ref[k0:k0 + kp, :])
        mix = d if mix is None else mix + d
        k0 += kp
    row = i * tm + lax.broadcasted_iota(jnp.int32, (tm, 1), 0)
    x1 = x_ref[...] + _mod_rows(m_ref, 2, row, ctx_start) * mix[:tm]
    x1p = xp_ref[...] + _mod_rows(m_ref, 2, i * tm - 1, ctx_start) * mix[tm + 8:tm + 16]
    x1n = xn_ref[...] + _mod_rows(m_ref, 2, (i + 1) * tm, ctx_start) * mix[tm + 16:tm + 24]

    for cb in range(D // 128):
        x3_ref[cb] = x1[:, cb * 128:(cb + 1) * 128]
    for j in range(nsub):
        for r in range(0, sb // 8, 2):
            x = jnp.concatenate([_perm_rows(x3_ref, j * sb, sb, r), _perm_rows(x3_ref, j * sb, sb, r + 1)], axis=0)
            h_ref[j * sb + 8 * r:j * sb + 8 * r + 16, :] = normmod(x, i * tm + j * sb).astype(BF16)
    h_ref[tm:tm + 16, :] = jnp.concatenate([normmod(x1p, i * tm - 1), normmod(x1n, (i + 1) * tm)],
                                           axis=0).astype(BF16)

    keep_prev = []
    keep_next = []
    for j in range(nsub):
        row0 = i * tm + j * sb
        kp = row0 != seg_starts[0]
        kn = row0 + sb != t
        for s0 in seg_starts[1:]:
            kp = kp & (row0 != s0)
            kn = kn & (row0 + sb != s0)
        keep_prev.append(jnp.where(kp, 1.0, 0.0))
        keep_next.append(jnp.where(kn, 1.0, 0.0))

    def conv(u, j, c0):
        r0 = j * sb
        cur = u[r0:r0 + sb]
        hp = (u[tm + 7:tm + 8] if j == 0 else u[r0 - 1:r0]) * keep_prev[j]
        hn = (u[tm + 8:tm + 9] if j == nsub - 1 else u[r0 + sb:r0 + sb + 1]) * keep_next[j]
        p0 = jnp.where(sub8 == 0, hp, pltpu.roll(cur[sb - 8:sb], 1, 0))
        nl = jnp.where(sub8 == 7, hn, pltpu.roll(cur[0:8], 7, 0))
        prev = jnp.concatenate([p0, cur[:sb - 8]], axis=0)
        nxt = jnp.concatenate([cur[8:], nl], axis=0)
        w = cw_ref[:, c0:c0 + ck]
        return prev * w[0:1] + cur * w[1:2] + nxt * w[2:3] + cb_ref[:, c0:c0 + ck]

    h = h_ref[...]

    def up(c0):
        return _dot(h, wu_ref[:, c0:c0 + ck]), _dot(h, wu_ref[:, D_FF + c0:D_FF + c0 + ck])

    ahead = up(0)
    for k, c0 in enumerate(range(0, D_FF, ck)):
        ug, uv = ahead
        if c0 + ck < D_FF:
            ahead = up(c0 + ck)
        parts = []
        for j in range(nsub):
            gt = conv(ug, j, c0)
            vl = conv(uv, j, D_FF + c0)
            parts.append((gt * _sigmoid(gt) * vl).astype(BF16))
        a = parts[0] if nsub == 1 else jnp.concatenate(parts, axis=0)
        d = _dot(a, wd_ref[c0:c0 + ck, :])
        for cb in range(D // 128):
            if k == 0:
                acc_ref[cb] = d[:, cb * 128:(cb + 1) * 128]
            else:
                acc_ref[cb] += d[:, cb * 128:(cb + 1) * 128]
    fg = fg_ref[...]
    for j in range(nsub):
        gate = _mod_rows(m_ref, 5, i * tm + j * sb, ctx_start)
        for q in range(sb // 8):
            rows = slice(j * sb + 8 * q, j * sb + 8 * q + 8)
            x1r = jnp.concatenate([x3_ref[cb, rows, :] for cb in range(D // 128)], axis=-1)
            out = x1r + gate * _unperm_rows(acc_ref, j * sb, sb, q)
            if final:
                ms = jnp.mean(out * out, axis=-1, keepdims=True)
                out = out * lax.rsqrt(ms + EPS) * fg
            o_ref[rows, :] = out


def mixer_out_conv_ffn(xs, t, parts, w_out, g, mod, w_up, conv_w, conv_b, w_down, final_g, tm, sb, seg_starts,
                       ctx_start, final, name):
    bsz = xs.shape[0]
    f2 = 2 * D_FF
    r8 = tm // 8
    r16 = tm // 16
    kern = functools.partial(_ffn_kernel, nparts=len(parts), tm=tm, t=t, sb=sb, seg_starts=seg_starts,
                             ctx_start=ctx_start, final=final, ck=256)
    const = lambda shape: pl.BlockSpec(shape, lambda b, i: (0,) * len(shape), pipeline_mode=pl.Buffered(1))
    in_specs = [pl.BlockSpec((None, tm, D), lambda b, i: (b, i, 0)),
                pl.BlockSpec((None, 8, D), lambda b, i: (b, jnp.maximum(i * r8 - 1, 0), 0)),
                pl.BlockSpec((None, 8, D), lambda b, i: (b, jnp.minimum((i + 1) * r8, t // 8 - 1), 0))]
    args = [xs, xs, xs]
    for p in parts:
        kp = p.shape[-1]
        in_specs += [pl.BlockSpec((None, tm, kp), lambda b, i: (b, i, 0)),
                     pl.BlockSpec((None, 16, kp), lambda b, i: (b, jnp.maximum(i * r16 - 1, 0), 0)),
                     pl.BlockSpec((None, 16, kp), lambda b, i: (b, jnp.minimum((i + 1) * r16, t // 16 - 1), 0))]
        args += [p, p, p]
    in_specs += [const(w_out.shape), const((1, D)),
                 pl.BlockSpec((None, 2, 6, D), lambda b, i: (b, 0, 0, 0)),
                 const((D, f2)), const((3, f2)), const((1, f2)), const((D_FF, D)), const((1, D))]
    args += [w_out, g.reshape(1, D), mod, w_up, conv_w, conv_b.reshape(1, f2), w_down, final_g.reshape(1, D)]
    return pl.pallas_call(
        kern,
        grid=(bsz, t // tm),
        in_specs=in_specs,
        out_specs=pl.BlockSpec((None, tm, D), lambda b, i: (b, i, 0)),
        out_shape=jax.ShapeDtypeStruct((bsz, t, D), F32),
        scratch_shapes=[pltpu.VMEM((tm + 16, D), BF16),
                        pltpu.VMEM((D // 128, tm, 128), F32), pltpu.VMEM((D // 128, tm, 128), F32)],
        compiler_params=_cp(("parallel", "parallel")),
        name=name,
    )(*args)


PAD = 8


def _fill_padded(xp_ref, x_ref, width):
    z = jnp.zeros((PAD, width), F32)
    xp_ref[0:PAD, :] = z
    xp_ref[PAD:PAD + S, :] = x_ref[0:S, :].astype(F32)
    xp_ref[PAD + S:2 * PAD + S, :] = z
    xp_ref[2 * PAD + S:2 * PAD + T, :] = x_ref[S:T, :].astype(F32)
    xp_ref[2 * PAD + T:3 * PAD + T, :] = z


def _conv_tile(xp_ref, w, bias, r0, rows, ktaps):
    base = PAD + r0 if r0 < S else 2 * PAD + r0
    left = ktaps // 2
    acc = bias
    for j in range(ktaps):
        acc = acc + xp_ref[base + j - left:base + j - left + rows, :] * w[j:j + 1, :]
    return acc


def _hconv_kernel(z_ref, w_ref, b_ref, o_ref, xp_ref):
    _fill_padded(xp_ref, z_ref, 512)
    w = w_ref[...]
    bias = b_ref[...]
    for r0 in range(0, T, 256):
        o_ref[r0:r0 + 256, :] = _conv_tile(xp_ref, w, bias, r0, 256, 3)


def hyena_short_conv(z, conv_w, conv_b):
    return pl.pallas_call(
        _hconv_kernel,
        grid=(B, 3),
        in_specs=[pl.BlockSpec((None, T, 512), lambda b, j: (b, 0, 2 + j)),
                  pl.BlockSpec((3, 512), lambda b, j: (0, j)),
                  pl.BlockSpec((1, 512), lambda b, j: (0, j))],
        out_specs=pl.BlockSpec((None, T, 512), lambda b, j: (b, 0, j)),
        out_shape=jax.ShapeDtypeStruct((B, T, 3 * HY_W), F32),
        scratch_shapes=[pltpu.VMEM((T + 3 * PAD, 512), F32)],
        compiler_params=_cp(("parallel", "parallel")),
        name="hyena_short_conv",
    )(z, conv_w, conv_b.reshape(1, 3 * HY_W))


RGH = 128
SEGS = ((S, C, C // 8), (0, S, S // 8))
POFF = {S: 0, 0: C + 24}


def _rg_kernel(x_ref, gt_ref, cw_ref, cb_ref, wbd_ref, bias_ref, sp_ref, o_ref,
               xp_ref, af_ref, bf_ref, ar_ref, br_ref, cf_ref, cr_ref):
    sub8 = lax.broadcasted_iota(jnp.int32, (8, 1), 0)
    cw = cw_ref[...]
    cb = cb_ref[...]
    nl2e8 = -RG_C * math.log2(math.e)
    spf = sp_ref[0:1, :] * nl2e8
    spr = sp_ref[1:2, :] * nl2e8

    for seg0, rows, nslot in SEGS:
        p0 = POFF[seg0]

        for r in range(nslot):
            xp_ref[p0 + 16 + 8 * r:p0 + 24 + 8 * r, :] = x_ref[pl.ds(seg0 + r, 8, stride=nslot), :]
        for slot, src, sh in ((0, nslot - 2, 1), (1, nslot - 1, 1), (nslot + 2, 0, 7)):
            v = pltpu.roll(xp_ref[p0 + 16 + 8 * src:p0 + 24 + 8 * src, :], sh, 0)
            xp_ref[p0 + 8 * slot:p0 + 8 * slot + 8, :] = jnp.where(sub8 == (0 if sh == 1 else 7), 0.0, v)

        for t0 in range(0, rows, 256):
            xc = cb
            for j in range(4):
                xc = xc + xp_ref[p0 + 8 * j + t0:p0 + 8 * j + t0 + 256, :] * cw[j:j + 1, :]
            g = _dot(xc.astype(BF16), wbd_ref[...])
            for d, (a_ref, b_ref, sp) in enumerate(((af_ref, bf_ref, spf), (ar_ref, br_ref, spr))):
                rg = _sigmoid(g[:, (2 * d) * RGH:(2 * d + 1) * RGH] + bias_ref[2 * d:2 * d + 1, :])
                ig = _sigmoid(g[:, (2 * d + 1) * RGH:(2 * d + 2) * RGH] + bias_ref[2 * d + 1:2 * d + 2, :])
                a = jnp.exp2(rg * sp)
                a_ref[seg0 + t0:seg0 + t0 + 256, :] = a
                om = 1.0 - a * a
                root = jnp.where(om > 0.0, om * lax.rsqrt(om), 0.0)
                b_ref[seg0 + t0:seg0 + t0 + 256, :] = root * (ig * xc)

    one = jnp.ones((8, RGH), F32)
    zero = jnp.zeros((8, RGH), F32)

    def step(a_ref, b_ref, t, p, h):
        t = pl.multiple_of(t, 8)
        a = a_ref[pl.ds(t, 8), :]
        h = a * h + b_ref[pl.ds(t, 8), :]
        p = a * p
        a_ref[pl.ds(t, 8), :] = p
        b_ref[pl.ds(t, 8), :] = h
        return p, h

    for seg0, rows, nslot in SEGS:
        half = nslot // 2

        def body(i, carry):
            fa, fb, ra, rb = carry
            fa = step(af_ref, bf_ref, seg0 + 8 * i, *fa)
            fb = step(af_ref, bf_ref, seg0 + 8 * (half + i), *fb)
            ra = step(ar_ref, br_ref, seg0 + 8 * (half - 1 - i), *ra)
            rb = step(ar_ref, br_ref, seg0 + 8 * (nslot - 1 - i), *rb)
            return fa, fb, ra, rb

        lax.fori_loop(0, half, body, ((one, zero),) * 4, unroll=4)

    def row(ref, t):
        return ref[t:t + 1, :]

    cf = jnp.zeros((1, RGH), F32)
    cr = jnp.zeros((1, RGH), F32)
    for k, (seg0, rows, nslot) in enumerate(SEGS):
        mid = seg0 + 8 * (nslot // 2)
        last = seg0 + 8 * (nslot - 1)
        for s in range(8):
            cf_ref[16 * k + s:16 * k + s + 1, :] = cf
            cf = row(bf_ref, mid - 8 + s) + row(af_ref, mid - 8 + s) * cf
            cf_ref[16 * k + 8 + s:16 * k + 9 + s, :] = cf
            cf = row(bf_ref, last + s) + row(af_ref, last + s) * cf
        for s in range(7, -1, -1):
            cr_ref[16 * k + 8 + s:16 * k + 9 + s, :] = cr
            cr = row(br_ref, mid + s) + row(ar_ref, mid + s) * cr
            cr_ref[16 * k + s:16 * k + s + 1, :] = cr
            cr = row(br_ref, seg0 + s) + row(ar_ref, seg0 + s) * cr

    for k, (seg0, rows, nslot) in enumerate(SEGS):
        hrows = rows // 2
        for t0 in range(0, rows, 256):
            sel = [16 * k + (0 if t0 + 8 * j < hrows else 8) for j in range(32)]
            cft = jnp.concatenate([cf_ref[o:o + 8, :] for o in sel], axis=0)
            crt = jnp.concatenate([cr_ref[o:o + 8, :] for o in sel], axis=0)
            sl = slice(seg0 + t0, seg0 + t0 + 256)
            bf_ref[sl, :] = (bf_ref[sl, :] + af_ref[sl, :] * cft) + (br_ref[sl, :] + ar_ref[sl, :] * crt)

    for seg0, rows, nslot in SEGS:
        per = nslot // 8
        for q in range(rows // 8):
            s, r0 = divmod(q, per)
            af_ref[seg0 + 8 * q:seg0 + 8 * q + 8, :] = bf_ref[pl.ds(seg0 + 64 * r0 + s, 8, stride=8), :]

    c0 = math.sqrt(2.0 / math.pi)
    for r0 in range(0, T, 256):
        gt = gt_ref[r0:r0 + 256, :].astype(F32)
        gelu = 0.5 * gt * (1.0 + jnp.tanh(c0 * (gt + 0.044715 * (gt * gt * gt))))
        o_ref[r0:r0 + 256, :] = (af_ref[r0:r0 + 256, :] * gelu).astype(o_ref.dtype)


def rglru(z, conv_w, conv_b, wbd, bias4, softplus_neg_lam):
    nh = RG_W // RGH
    return pl.pallas_call(
        _rg_kernel,
        grid=(B, nh),
        in_specs=[pl.BlockSpec((None, T, RGH), lambda b, j: (b, 0, j)),
                  pl.BlockSpec((None, T, RGH), lambda b, j: (b, 0, nh + j)),
                  pl.BlockSpec((4, RGH), lambda b, j: (0, j)),
                  pl.BlockSpec((1, RGH), lambda b, j: (0, j)),
                  pl.BlockSpec((None, RGH, 4 * RGH), lambda b, j: (j, 0, 0)),
                  pl.BlockSpec((4, RGH), lambda b, j: (0, j)),
                  pl.BlockSpec((2, RGH), lambda b, j: (0, j))],
        out_specs=pl.BlockSpec((None, T, RGH), lambda b, j: (b, 0, j)),
        out_shape=jax.ShapeDtypeStruct((B, T, RG_W), BF16),
        scratch_shapes=[pltpu.VMEM((T + 48, RGH), F32)] + [pltpu.VMEM((T, RGH), F32)] * 4
                       + [pltpu.VMEM((32, RGH), F32)] * 2,
        compiler_params=_cp(("parallel", "parallel")),
        name="rglru",
    )(z, z, conv_w, conv_b.reshape(1, RG_W), wbd, bias4, softplus_neg_lam)


def _softplus_kernel(x_ref, o_ref):
    y = -x_ref[...]
    o_ref[...] = jnp.maximum(y, 0.0) + jnp.log(1.0 + jnp.exp(-jnp.abs(y)))


def softplus_neg(lam):
    return pl.pallas_call(
        _softplus_kernel,
        out_shape=jax.ShapeDtypeStruct(lam.shape, F32),
        name="softplus_neg",
    )(lam)


def _hfilt_kernel(z_ref, w1_ref, b1_ref, fr_ref, w2_ref, b2_ref, w3_ref, dec_ref, x1_ref, x2_ref, nyq_ref,
                  s1_ref, s2_ref, *, n, tr):
    fr = fr_ref[...]
    w1h, w1l = _split(w1_ref[...])
    w2h, w2l = _split(w2_ref[...])
    w3h, w3l = _split(w3_ref[...])
    tot = jnp.zeros((1, HY_W), F32)
    alt = jnp.zeros((1, HY_W), F32)
    sign = jnp.where(lax.broadcasted_iota(jnp.int32, (tr, 1), 0) % 2 == 0, 1.0, -1.0)
    for r0 in range(0, n, tr):
        zh, zl = _split(z_ref[r0:r0 + tr, :])
        h = jnp.sin(fr * (_dot3(zh, zl, w1h, w1l) + b1_ref[...]))
        hh, hl = _split(h)
        h = jnp.sin(fr * (_dot3(hh, hl, w2h, w2l) + b2_ref[...]))
        hh, hl = _split(h)
        h = _dot3(hh, hl, w3h, w3l)
        dec = dec_ref[r0:r0 + tr, :]
        h0 = h[:, :HY_W] * dec
        h1 = h[:, HY_W:] * dec
        if r0 == 0:
            h1 = jnp.where(lax.broadcasted_iota(jnp.int32, (tr, 1), 0) == 0, 0.0, h1)
        tot = tot + jnp.sum(jnp.abs(h0) + jnp.abs(h1), axis=0, keepdims=True)
        alt = alt + jnp.sum((h0 + h1) * sign, axis=0, keepdims=True)
        s1_ref[r0:r0 + tr, :] = h0 + h1
        s2_ref[r0:r0 + tr, :] = h0 - h1
    inv = 1.0 / tot
    for r0 in range(0, n, tr):
        x1_ref[r0:r0 + tr, :] = (s1_ref[r0:r0 + tr, :] * inv).astype(BF16)
        x2_ref[r0:r0 + tr, :] = (s2_ref[r0:r0 + tr, :] * inv).astype(BF16)
    nyq_ref[...] = jnp.broadcast_to(alt * inv, (8, HY_W))


def hyena_filter_halves(n, zemb, w1p, b1, freq, w2, b2, w3, decay):
    kern = functools.partial(_hfilt_kernel, n=n, tr=256)
    full = lambda shape: pl.BlockSpec(shape, lambda o: (0,) * len(shape))
    return pl.pallas_call(
        kern,
        grid=(2,),
        in_specs=[full((n, 128)), full((128, HY_HID)), full((1, HY_HID)), full((1, HY_HID)),
                  full((HY_HID, HY_HID)), full((1, HY_HID)),
                  pl.BlockSpec((HY_HID, 2 * HY_W), lambda o: (0, o)),
                  full((n, HY_W))],
        out_specs=[pl.BlockSpec((n, HY_W), lambda o: (0, o)),
                   pl.BlockSpec((n, HY_W), lambda o: (0, o)),
                   pl.BlockSpec((8, HY_W), lambda o: (0, o))],
        out_shape=[jax.ShapeDtypeStruct((n, 2 * HY_W), BF16),
                   jax.ShapeDtypeStruct((n, 2 * HY_W), BF16),
                   jax.ShapeDtypeStruct((8, 2 * HY_W), F32)],
        scratch_shapes=[pltpu.VMEM((n, HY_W), F32), pltpu.VMEM((n, HY_W), F32)],
        compiler_params=_cp(("parallel",)),
        name=f"hyena_filter_{n}",
    )(zemb, w1p, b1.reshape(1, HY_HID), freq.reshape(1, HY_HID), w2, b2.reshape(1, HY_HID), w3, decay)


def _kspec_kernel(at_ref, ab_ref, x1_ref, x2_ref, kt_ref, kb_ref):
    kt_ref[...] = _dot(at_ref[...], x1_ref[...])
    kb_ref[...] = _dot(ab_ref[...], x2_ref[...])


def filter_spectrum(fwd, x1, x2, n, tr):
    nt = n // tr
    return pl.pallas_call(
        _kspec_kernel,
        grid=(nt,),
        in_specs=[pl.BlockSpec((tr, n), lambda i: (i, 0)),
                  pl.BlockSpec((tr, n), lambda i: (nt + i, 0)),
                  pl.BlockSpec((n, 2 * HY_W), lambda i: (0, 0)),
                  pl.BlockSpec((n, 2 * HY_W), lambda i: (0, 0))],
        out_specs=[pl.BlockSpec((tr, 2 * HY_W), lambda i: (i, 0)),
                   pl.BlockSpec((tr, 2 * HY_W), lambda i: (i, 0))],
        out_shape=[jax.ShapeDtypeStruct((n, 2 * HY_W), F32)] * 2,
        compiler_params=_cp(("parallel",)),
        name=f"filter_spectrum_{n}",
    )(fwd, fwd, x1, x2)


def _hfwd_kernel(at_ref, ab_ref, x_ref, kt_ref, kb_ref, kn_ref, yr_ref, yi_ref, xb_ref, *, tr):
    i = pl.program_id(1)

    @pl.when(i == 0)
    def _():
        xb_ref[...] = x_ref[...].astype(BF16)

    tab = jnp.concatenate([at_ref[...], ab_ref[...]], axis=0)
    hw = HY_W // 2
    us = [_dot(tab, xb_ref[:, c0:c0 + hw]) for c0 in (0, hw)]
    first = (lax.broadcasted_iota(jnp.int32, (tr, 1), 0) == 0) & (i == 0)
    for u, c0 in zip(us, (0, hw)):
        ur = u[:tr]
        ui = u[tr:]
        kr = kt_ref[:, c0:c0 + hw]
        ki = kb_ref[:, c0:c0 + hw]
        yr_ref[:, c0:c0 + hw] = (ur * kr - jnp.where(first, 0.0, ui * ki)).astype(BF16)
        yi_ref[:, c0:c0 + hw] = jnp.where(first, ui * kn_ref[0:1, c0:c0 + hw], ur * ki + ui * kr).astype(BF16)


def hyena_forward(fwd, x, x_rb, kt, kb, nyq, order, n, tr, name):
    nt = n // tr
    kern = functools.partial(_hfwd_kernel, tr=tr)
    return pl.pallas_call(
        kern,
        grid=(B, nt),
        in_specs=[pl.BlockSpec((tr, n), lambda b, i: (i, 0)),
                  pl.BlockSpec((tr, n), lambda b, i: (nt + i, 0)),
                  pl.BlockSpec((None, n, HY_W), lambda b, i: (b, x_rb, 0)),
                  pl.BlockSpec((tr, HY_W), lambda b, i: (i, order)),
                  pl.BlockSpec((tr, HY_W), lambda b, i: (i, order)),
                  pl.BlockSpec((8, HY_W), lambda b, i: (0, order))],
        out_specs=[pl.BlockSpec((None, tr, HY_W), lambda b, i: (b, i, 0)),
                   pl.BlockSpec((None, tr, HY_W), lambda b, i: (b, i, 0))],
        out_shape=[jax.ShapeDtypeStruct((B, n, HY_W), BF16)] * 2,
        scratch_shapes=[pltpu.VMEM((n, HY_W), BF16)],
        compiler_params=_cp(("parallel", "arbitrary")),
        name=name,
    )(fwd, fwd, x, kt, kb, nyq)


def _hinv_kernel(al_ref, ar_ref, yr_ref, yi_ref, up_ref, gt_ref, bias_ref, o_ref):
    y = _dot(al_ref[...], yr_ref[...]) + _dot(ar_ref[...], yi_ref[...])
    o_ref[...] = (gt_ref[...] * (y + up_ref[...].astype(F32) * bias_ref[...])).astype(o_ref.dtype)


def hyena_inverse(inv, yr, yi, uprev, uprev_rb, gate, gate_rb, gate_cb, bias, n, tmi, out_dtype, name):
    return pl.pallas_call(
        _hinv_kernel,
        grid=(B, n // tmi),
        in_specs=[pl.BlockSpec((tmi, n), lambda b, i: (i, 0)),
                  pl.BlockSpec((tmi, n), lambda b, i: (i, 1)),
                  pl.BlockSpec((None, n, HY_W), lambda b, i: (b, 0, 0)),
                  pl.BlockSpec((None, n, HY_W), lambda b, i: (b, 0, 0)),
                  pl.BlockSpec((None, tmi, HY_W), lambda b, i: (b, uprev_rb + i, 0)),
                  pl.BlockSpec((None, tmi, HY_W), lambda b, i: (b, gate_rb + i, gate_cb)),
                  pl.BlockSpec((1, HY_W), lambda b, i: (0, 0))],
        out_specs=pl.BlockSpec((None, tmi, HY_W), lambda b, i: (b, i, 0)),
        out_shape=jax.ShapeDtypeStruct((B, n, HY_W), out_dtype),
        compiler_params=_cp(("parallel", "parallel")),
        name=name,
    )(inv, inv, yr, yi, uprev, gate, bias.reshape(1, HY_W))


def _rope128(r, c, sa, sb):
    return r * c + pltpu.roll(r, 32, 1) * sa + pltpu.roll(r, 96, 1) * sb


def _qkv_kernel(x_ref, g_ref, m_ref, win_ref, qg_ref, kvg_ref, wq_ref, wk_ref, wv_ref, c_ref, sa_ref, sb_ref,
                q_ref, k_ref, v_ref, *, tm):
    x = x_ref[...]
    y = x * lax.rsqrt(jnp.mean(x * x, axis=-1, keepdims=True) + EPS) * g_ref[...]
    row = pl.program_id(1) * tm + lax.broadcasted_iota(jnp.int32, (tm, 1), 0)
    h = (y * (1.0 + _mod_rows(m_ref, 1, row, S)) + _mod_rows(m_ref, 0, row, S)).astype(BF16)
    z = _dot(h, win_ref[...])
    c = c_ref[...]
    sa = sa_ref[...]
    sb = sb_ref[...]
    zq = z[:, :Q_LORA]
    qn = (zq * lax.rsqrt(jnp.mean(zq * zq, axis=-1, keepdims=True) + EPS) * qg_ref[...]).astype(BF16)
    zkv = z[:, Q_LORA:Q_LORA + KV_LORA]
    ckv = (zkv * lax.rsqrt(jnp.mean(zkv * zkv, axis=-1, keepdims=True) + EPS) * kvg_ref[...]).astype(BF16)
    kr = _rope128(z[:, Q_LORA + KV_LORA:], c, sa, sb).astype(BF16)
    for h in range(HEADS):
        q = _dot(qn, wq_ref[:, h * HQ:(h + 1) * HQ])
        q_ref[:, h * HQ:h * HQ + NOPE] = q[:, :NOPE].astype(BF16)
        q_ref[:, h * HQ + NOPE:(h + 1) * HQ] = _rope128(q[:, NOPE:], c, sa, sb).astype(BF16)
        k_ref[:, h * HQ:h * HQ + NOPE] = _dot(ckv, wk_ref[:, h * NOPE:(h + 1) * NOPE]).astype(BF16)
        k_ref[:, h * HQ + NOPE:(h + 1) * HQ] = kr
    v_ref[...] = _dot(ckv, wv_ref[...]).astype(BF16)


def mla_qkv(xs, g, mod, w_in, q_g, kv_g, wq, wk, wv, ctab, satab, sbtab, tm):
    full = lambda shape: pl.BlockSpec(shape, lambda b, i: (0,) * len(shape))
    tab = pl.BlockSpec((tm, 128), lambda b, i: (i, 0))
    return pl.pallas_call(
        functools.partial(_qkv_kernel, tm=tm),
        grid=(B, T // tm),
        in_specs=[pl.BlockSpec((None, tm, D), lambda b, i: (b, i, 0)),
                  full((1, D)),
                  pl.BlockSpec((None, 2, 6, D), lambda b, i: (b, 0, 0, 0)),
                  full((D, ODD_IN_P)),
                  full((1, Q_LORA)), full((1, KV_LORA)),
                  full((Q_LORA, HEADS * HQ)), full((KV_LORA, HEADS * NOPE)), full((KV_LORA, HEADS * VH)),
                  tab, tab, tab],
        out_specs=[pl.BlockSpec((None, tm, HEADS * HQ), lambda b, i: (b, i, 0)),
                   pl.BlockSpec((None, tm, HEADS * HQ), lambda b, i: (b, i, 0)),
                   pl.BlockSpec((None, tm, HEADS * VH), lambda b, i: (b, i, 0))],
        out_shape=[jax.ShapeDtypeStruct((B, T, HEADS * HQ), BF16),
                   jax.ShapeDtypeStruct((B, T, HEADS * HQ), BF16),
                   jax.ShapeDtypeStruct((B, T, HEADS * VH), BF16)],
        compiler_params=_cp(("parallel", "parallel")),
        name="mla_qkv",
    )(xs, g.reshape(1, D), mod, w_in, q_g.reshape(1, Q_LORA), kv_g.reshape(1, KV_LORA), wq, wk, wv,
      ctab, satab, sbtab)


def _attn_kernel(q_ref, k_ref, v_ref, o_ref, *, scale, ts):
    c = scale * math.log2(math.e)
    kc = 256
    tq = q_ref.shape[0]
    nh = q_ref.shape[1] // HQ
    tiles = [(h, r0) for h in range(nh) for r0 in range(0, tq, ts)]

    def scores(h, r0):
        return lax.dot_general(q_ref[r0:r0 + ts, h * HQ:(h + 1) * HQ], k_ref[:, h * HQ:(h + 1) * HQ],
                               (((1,), (1,)), ((), ())), preferred_element_type=F32)

    ahead = scores(*tiles[0])
    for n, (h, r0) in enumerate(tiles):
        s = ahead
        if n + 1 < len(tiles):
            ahead = scores(*tiles[n + 1])
        m = jnp.max(s, axis=-1, keepdims=True)
        acc = None
        lv = None
        for c0 in range(0, T, kc):
            p = jnp.exp2((s[:, c0:c0 + kc] - m) * c)
            pl_ = p[:, :128]
            for l0 in range(128, kc, 128):
                pl_ = pl_ + p[:, l0:l0 + 128]
            lv = pl_ if lv is None else lv + pl_
            d = _dot(p.astype(BF16), v_ref[c0:c0 + kc, h * VH:(h + 1) * VH])
            acc = d if acc is None else acc + d
        l = jnp.sum(lv, axis=-1, keepdims=True)
        o_ref[r0:r0 + ts, h * VH:(h + 1) * VH] = (acc * (1.0 / l)).astype(o_ref.dtype)


def mla_attention(q, k, v, tq, hp):
    kern = functools.partial(_attn_kernel, scale=(NOPE + ROPE) ** -0.5, ts=512)
    return pl.pallas_call(
        kern,
        grid=(B, HEADS // hp, S // tq),
        in_specs=[pl.BlockSpec((None, tq, hp * HQ), lambda b, h, i: (b, i, h)),
                  pl.BlockSpec((None, T, hp * HQ), lambda b, h, i: (b, 0, h)),
                  pl.BlockSpec((None, T, hp * VH), lambda b, h, i: (b, 0, h))],
        out_specs=pl.BlockSpec((None, tq, hp * VH), lambda b, h, i: (b, i, h)),
        out_shape=jax.ShapeDtypeStruct((B, S, HEADS * VH), BF16),
        compiler_params=_cp(("parallel", "parallel", "parallel")),
        name="mla_attention",
    )(q, k, v)


@functools.lru_cache(maxsize=None)
def _dft_tables(n):
    m = 2 * n
    j = np.arange(n, dtype=np.int64)
    kk = np.arange(n, dtype=np.int64)
    ang = 2.0 * np.pi * ((kk[:, None] * j[None, :]) % m).astype(np.float64) / m
    cosm = np.cos(ang)
    sinm = np.sin(ang)
    fwd = np.concatenate([cosm, -sinm], axis=0)
    fwd[n, :] = np.where(j % 2 == 0, 1.0, -1.0)
    inv = np.concatenate([2.0 * cosm.T, -2.0 * sinm.T], axis=1) / m
    inv[:, 0] = 1.0 / m
    inv[:, n] = np.where(j % 2 == 0, 1.0, -1.0) / m
    return fwd.astype(np.float32).astype(BF16), inv.astype(np.float32).astype(BF16)


def _filter_consts(n):
    pos = jnp.arange(n, dtype=F32)
    t = jnp.linspace(0.0, 1.0, n, dtype=F32)[:, None]
    bands = (HY_EMB - 1) // 2
    w = 2.0 * math.pi * pos / n
    f = jnp.linspace(1e-4, bands - 1, bands, dtype=F32)
    ang = w[:, None] * f[None, :]
    z = jnp.concatenate([t, jnp.cos(ang), -jnp.sin(ang)], axis=-1)
    z = jnp.pad(z, ((0, 0), (0, 128 - HY_EMB)))
    max_decay = math.log(1e-2) / 0.3
    min_decay = math.log(1e-2) / 1.5
    deltas = jnp.linspace(min_decay, max_decay, HY_W, dtype=F32)
    decay = jnp.exp(-t * jnp.abs(deltas)[None, :])
    return z, decay


def _rope_tables():
    rows = S // 64
    row = jnp.repeat(jnp.arange(rows, dtype=F32), 64)
    col = jnp.tile(jnp.arange(64, dtype=F32), rows)
    n_freq = ROPE // 4
    inv = 10000.0 ** (-jnp.arange(n_freq, dtype=F32) / n_freq)
    ang = jnp.concatenate([row[:, None] * inv[None, :], col[:, None] * inv[None, :]], axis=-1)
    cos = jnp.concatenate([jnp.cos(ang), jnp.ones((C, 32), F32)], axis=0)
    sin = jnp.concatenate([jnp.sin(ang), jnp.zeros((C, 32), F32)], axis=0)
    z32 = jnp.zeros((T, 32), F32)
    z64 = jnp.zeros((T, 64), F32)
    ctab = jnp.concatenate([cos, cos, z64], axis=-1)
    satab = jnp.concatenate([z32, sin, z64], axis=-1)
    sbtab = jnp.concatenate([-sin, z32, z64], axis=-1)
    return ctab, satab, sbtab


def _even_layer(x, ctx, mod, norm1_g, w_in, rg_conv_w, rg_conv_b, rg_wa, rg_ba, rg_wx, rg_bx, rg_lambda,
                hy_conv_w, hy_conv_b, f_w1, f_b1, f_freq, f_w2, f_b2, f_w3, hy_bias, w_out):
    z, xs = even_in_proj(x, ctx, norm1_g, mod, w_in.astype(BF16))

    def blockdiag(w):
        per = RGH // RG_HD
        out = jnp.zeros((RG_W // RGH, RGH, RGH), F32)
        for h in range(RG_HEADS):
            g, l = divmod(h, per)
            out = out.at[g, l * RG_HD:(l + 1) * RG_HD, l * RG_HD:(l + 1) * RG_HD].set(w[h])
        return out

    wbd = jnp.concatenate([blockdiag(rg_wa[0]), blockdiag(rg_wx[0]), blockdiag(rg_wa[1]), blockdiag(rg_wx[1])],
                          axis=-1).astype(BF16)
    bias4 = jnp.stack([rg_ba[0], rg_bx[0], rg_ba[1], rg_bx[1]], axis=0)
    a = rglru(z, rg_conv_w, rg_conv_b, wbd, bias4, softplus_neg(rg_lambda))

    vc = hyena_short_conv(z, hy_conv_w, hy_conv_b)
    w1p = jnp.pad(f_w1, ((0, 128 - HY_EMB), (0, 0)))
    outs = []
    for n, rb in ((S, 0), (C, S // C)):
        fwd, inv = _dft_tables(n)
        zemb, decay = _filter_consts(n)
        x1, x2, nyq = hyena_filter_halves(n, zemb, w1p, f_b1, f_freq, f_w2, f_b2, f_w3, decay)
        tr = min(n, 512)
        tmi = min(n, 1024)
        kt, kb = filter_spectrum(fwd, x1, x2, n, tr)
        grb = rb * (n // tmi)
        yr, yi = hyena_forward(fwd, vc, rb, kt, kb, nyq, 0, n, tr, f"hyena_fwd1_{n}")
        u1 = hyena_inverse(inv, yr, yi, vc, grb, vc, grb, 1, hy_bias[0], n, tmi, F32, f"hyena_inv1_{n}")
        yr, yi = hyena_forward(fwd, u1, 0, kt, kb, nyq, 1, n, tr, f"hyena_fwd2_{n}")
        outs.append(hyena_inverse(inv, yr, yi, u1, 0, vc, grb, 2, hy_bias[1], n, tmi, BF16,
                                  f"hyena_inv2_{n}"))
    return ([a, jnp.concatenate(outs, axis=1)], w_out.astype(BF16)), xs


def _odd_layer(xs, mod, norm1_g, w_in, q_g, kv_g, w_uq, w_ukv, w_o):
    perm = np.concatenate([np.arange(0, ROPE, 2), np.arange(1, ROPE, 2)])
    w_rope = jnp.pad(w_in[:, Q_LORA + KV_LORA:][:, perm], ((0, 0), (0, 128 - ROPE)))
    w_in_p = jnp.concatenate([w_in[:, :Q_LORA + KV_LORA], w_rope], axis=-1).astype(BF16)
    wq = w_uq.reshape(Q_LORA, HEADS, NOPE + ROPE)
    wq = jnp.concatenate([wq[..., :NOPE], wq[..., NOPE:][..., perm],
                          jnp.zeros((Q_LORA, HEADS, HQ - NOPE - ROPE), F32)], axis=-1)
    wq = wq.reshape(Q_LORA, HEADS * HQ).astype(BF16)
    wkv = w_ukv.reshape(KV_LORA, HEADS, NOPE + VH)
    wk = wkv[..., :NOPE].reshape(KV_LORA, HEADS * NOPE).astype(BF16)
    wv = wkv[..., NOPE:].reshape(KV_LORA, HEADS * VH).astype(BF16)
    ctab, satab, sbtab = _rope_tables()
    q, k, v = mla_qkv(xs, norm1_g, mod, w_in_p, q_g, kv_g, wq, wk, wv, ctab, satab, sbtab, 768)
    return [mla_attention(q, k, v, 2048, 2)], w_o.astype(BF16)


def _ffn(xs, t, mix, mod, norm2_g, w_up, conv_w, conv_b, w_down, final_g, tm, sb, seg_starts, ctx_start, final,
         tag):
    parts, w_out = mix
    return mixer_out_conv_ffn(xs, t, parts, w_out, norm2_g, mod, w_up.astype(BF16), conv_w, conv_b,
                              w_down.astype(BF16), final_g, tm, sb, seg_starts, ctx_start, final,
                              f"mix_out_conv_ffn_{tag}")


def kernel(x, c, ctx, c_ctx, ada_w, ada_b, norm1_g, norm2_g, ev_w_in, ev_rg_conv_w, ev_rg_conv_b, ev_rg_wa, ev_rg_ba, ev_rg_wx, ev_rg_bx, ev_rg_lambda, ev_hy_conv_w, ev_hy_conv_b, ev_hy_f_w1, ev_hy_f_b1, ev_hy_f_freq, ev_hy_f_w2, ev_hy_f_b2, ev_hy_f_w3, ev_hy_bias, ev_w_out, od_w_in, od_q_norm_g, od_kv_norm_g, od_w_uq, od_w_ukv, od_w_o, ffn_w_up, ffn_conv_w, ffn_conv_b, ffn_w_down, final_g):
    cvec = jnp.concatenate([c, c_ctx[None], jnp.zeros((16 - B - 1, D), F32)], axis=0)
    mod = modulation(cvec, ada_w, ada_b)

    def mod_sel(layer):
        lat = mod[layer, :B].reshape(B, 1, 6, D)
        cx = jnp.broadcast_to(mod[layer, B].reshape(1, 1, 6, D), (B, 1, 6, D))
        return jnp.concatenate([lat, cx], axis=1)

    m0 = mod_sel(0)
    mix, xs = _even_layer(x, ctx, m0, norm1_g[0], ev_w_in[0], ev_rg_conv_w[0], ev_rg_conv_b[0], ev_rg_wa[0], ev_rg_ba[0],
                      ev_rg_wx[0], ev_rg_bx[0], ev_rg_lambda[0], ev_hy_conv_w[0], ev_hy_conv_b[0], ev_hy_f_w1[0],
                      ev_hy_f_b1[0], ev_hy_f_freq[0], ev_hy_f_w2[0], ev_hy_f_b2[0], ev_hy_f_w3[0], ev_hy_bias[0],
                      ev_w_out[0])
    xs = _ffn(xs, T, mix, m0, norm2_g[0], ffn_w_up[0], ffn_conv_w[0], ffn_conv_b[0], ffn_w_down[0], final_g,
              384, 128, (0, S), S, False, "0")
    m1 = mod_sel(1)
    mix = _odd_layer(xs, m1, norm1_g[1], od_w_in[0], od_q_norm_g[0], od_kv_norm_g[0], od_w_uq[0], od_w_ukv[0],
                     od_w_o[0])
    return _ffn(xs, S, mix, m1, norm2_g[1], ffn_w_up[1], ffn_conv_w[1], ffn_conv_b[1], ffn_w_down[1], final_g,
                256, 128, (0,), None, True, "1")
```

```python
import functools
import math

import numpy as np
import jax
import jax.numpy as jnp
from jax import lax
from jax.experimental import pallas as pl
from jax.experimental.pallas import tpu as pltpu

F32 = jnp.float32
BF16 = jnp.bfloat16

D = 1024
B = 8
S = 2048
C = 256
T = S + C
EPS = 1e-6

RG_W = 512
RG_HEADS = 8
RG_HD = 64
RG_C = 8.0
HY_W = 512
HY_EMB = 33
HY_HID = 64
EVEN_IN = 2560

HEADS = 8
Q_LORA = 512
KV_LORA = 256
NOPE = 128
ROPE = 64
VH = 128
HQ = 256
ODD_IN_P = 896
D_FF = 2816

VMEM_LIMIT = 52 * 1024 * 1024


def _cp(sem):
    return pltpu.CompilerParams(dimension_semantics=sem, vmem_limit_bytes=VMEM_LIMIT)


def _dot(a, b):
    return jnp.dot(a, b, preferred_element_type=F32)


def _split(x):
    hi = x.astype(BF16)
    lo = (x - hi.astype(F32)).astype(BF16)
    return hi, lo


def _dot3(ah, al, bh, bl):
    return _dot(ah, bh) + (_dot(ah, bl) + _dot(al, bh))


def _sigmoid(x):
    return 1.0 / (1.0 + jnp.exp2(x * (-math.log2(math.e))))


def _mod_rows(m_ref, k, row, ctx_start):
    lat = m_ref[0, k:k + 1, :]
    if ctx_start is None:
        return lat
    return jnp.where(row >= ctx_start, m_ref[1, k:k + 1, :], lat)


def _strided_rows(ref3, start, stride, cb0=0):
    return jnp.concatenate([ref3[cb0 + cb, pl.ds(start, 8, stride=stride), :] for cb in range(D // 128)],
                           axis=-1)


def _perm_rows(ref3, base, nrows, r, cb0=0):
    return _strided_rows(ref3, base + r, nrows // 8, cb0)


def _unperm_rows(ref3, base, nrows, q):
    s, r0 = divmod(8 * q, nrows // 8)
    return _strided_rows(ref3, base + 8 * r0 + s, 8)


def _mod_kernel(c_ref, w_ref, b_ref, o_ref):
    c = c_ref[...]
    s = (c * _sigmoid(c)).astype(BF16)
    o_ref[...] = _dot(s, w_ref[...].astype(BF16)) + b_ref[...]


def modulation(cvec, ada_w, ada_b):
    depth = ada_w.shape[0]
    tn = 1024
    return pl.pallas_call(
        _mod_kernel,
        grid=(depth, 6 * D // tn),
        in_specs=[pl.BlockSpec((16, D), lambda l, j: (0, 0)),
                  pl.BlockSpec((None, D, tn), lambda l, j: (l, 0, j)),
                  pl.BlockSpec((None, 1, tn), lambda l, j: (l, 0, j))],
        out_specs=pl.BlockSpec((None, 16, tn), lambda l, j: (l, 0, j)),
        out_shape=jax.ShapeDtypeStruct((depth, 16, 6 * D), F32),
        compiler_params=_cp(("parallel", "parallel")),
        name="modulation",
    )(cvec, ada_w, ada_b.reshape(depth, 1, 6 * D))


def _even_in_kernel(xa_ref, xb_ref, xc_ref, ctx_ref, g_ref, m_ref, w_ref, z_ref, xs_ref, h_ref, *, tm, nchunk):
    i = pl.program_id(1)
    third = jnp.where(i == T // tm - 1, ctx_ref[...], xc_ref[...])
    x = jnp.concatenate([xa_ref[...], xb_ref[...], third], axis=0)
    xs_ref[...] = x
    ms = jnp.mean(x * x, axis=-1, keepdims=True)
    y = x * lax.rsqrt(ms + EPS) * g_ref[...]
    row = i * tm + lax.broadcasted_iota(jnp.int32, (tm, 1), 0)
    h_ref[...] = (y * (1.0 + _mod_rows(m_ref, 1, row, S)) + _mod_rows(m_ref, 0, row, S)).astype(BF16)
    n = z_ref.shape[-1]
    for n0 in range(0, n, nchunk):
        z_ref[:, n0:n0 + nchunk] = _dot(h_ref[...], w_ref[:, n0:n0 + nchunk])


def even_in_proj(x, ctx, g, mod, w):
    tm = 3 * C
    n = w.shape[1]
    last = S // C - 1
    xblk = lambda k: pl.BlockSpec((None, C, D), lambda b, i: (b, jnp.minimum(3 * i + k, last), 0))
    return pl.pallas_call(
        functools.partial(_even_in_kernel, tm=tm, nchunk=512),
        grid=(B, T // tm),
        in_specs=[xblk(0), xblk(1), xblk(2),
                  pl.BlockSpec((None, C, D), lambda b, i: (b, 0, 0)),
                  pl.BlockSpec((1, D), lambda b, i: (0, 0)),
                  pl.BlockSpec((None, 2, 6, D), lambda b, i: (b, 0, 0, 0)),
                  pl.BlockSpec((D, n), lambda b, i: (0, 0))],
        out_specs=[pl.BlockSpec((None, tm, n), lambda b, i: (b, i, 0)),
                   pl.BlockSpec((None, tm, D), lambda b, i: (b, i, 0))],
        out_shape=[jax.ShapeDtypeStruct((B, T, n), F32), jax.ShapeDtypeStruct((B, T, D), F32)],
        scratch_shapes=[pltpu.VMEM((tm, D), BF16)],
        compiler_params=_cp(("parallel", "parallel")),
        name="even_in_proj",
    )(x, x, x, ctx, g.reshape(1, D), mod, w)


def _ffn_kernel(*refs, nparts, tm, nin, t, sb, seg_starts, ctx_start, final, ck):
    x_ref, xp_ref, xn_ref = refs[:3]
    parts = [refs[3 + 3 * p:6 + 3 * p] for p in range(nparts)]
    (wo_ref, g_ref, m_ref, wu_ref, cw_ref, cb_ref, wd_ref, fg_ref, o_ref,
     h_ref, x3_ref, acc_ref) = refs[3 + 3 * nparts:]
    i = pl.program_id(1)
    nsub = tm // sb
    ncb = D // 128
    sub8 = lax.broadcasted_iota(jnp.int32, (8, 1), 0)
    g = g_ref[...]

    def normmod(x, row):
        ms = jnp.mean(x * x, axis=-1, keepdims=True)
        y = x * lax.rsqrt(ms + EPS) * g
        return y * (1.0 + _mod_rows(m_ref, 4, row, ctx_start)) + _mod_rows(m_ref, 3, row, ctx_start)

    def prepare(k):
        base = k * tm
        row0 = i * nin * tm + base
        mix = None
        k0 = 0
        for p_ref, pp_ref, pn_ref in parts:
            kp = p_ref.shape[-1]
            before = pp_ref[...] if k == 0 else p_ref[base - 16:base, :]
            after = pn_ref[...] if k == nin - 1 else p_ref[base + tm:base + tm + 16, :]
            d = _dot(jnp.concatenate([p_ref[base:base + tm, :], before, after], axis=0), wo_ref[k0:k0 + kp, :])
            mix = d if mix is None else mix + d
            k0 += kp
        xp = xp_ref[...] if k == 0 else x_ref[base - 8:base, :]
        xn = xn_ref[...] if k == nin - 1 else x_ref[base + tm:base + tm + 8, :]
        row = row0 + lax.broadcasted_iota(jnp.int32, (tm, 1), 0)
        x1 = x_ref[base:base + tm, :] + _mod_rows(m_ref, 2, row, ctx_start) * mix[:tm]
        x1p = xp + _mod_rows(m_ref, 2, row0 - 1, ctx_start) * mix[tm + 8:tm + 16]
        x1n = xn + _mod_rows(m_ref, 2, row0 + tm, ctx_start) * mix[tm + 16:tm + 24]
        for cb in range(ncb):
            x3_ref[k * ncb + cb] = x1[:, cb * 128:(cb + 1) * 128]
        for j in range(nsub):
            for r in range(0, sb // 8, 2):
                x = jnp.concatenate([_perm_rows(x3_ref, j * sb, sb, r, k * ncb),
                                     _perm_rows(x3_ref, j * sb, sb, r + 1, k * ncb)], axis=0)
                h_ref[k, j * sb + 8 * r:j * sb + 8 * r + 16, :] = normmod(x, row0 + j * sb).astype(BF16)
        h_ref[k, tm:tm + 16, :] = jnp.concatenate([normmod(x1p, row0 - 1), normmod(x1n, row0 + tm)],
                                                  axis=0).astype(BF16)

    def finish(k):
        base = k * tm
        row0 = i * nin * tm + base
        keep_prev = []
        keep_next = []
        for j in range(nsub):
            r0 = row0 + j * sb
            kp = r0 != seg_starts[0]
            kn = r0 + sb != t
            for s0 in seg_starts[1:]:
                kp = kp & (r0 != s0)
                kn = kn & (r0 + sb != s0)
            keep_prev.append(jnp.where(kp, 1.0, 0.0))
            keep_next.append(jnp.where(kn, 1.0, 0.0))

        def conv(u, j, c0):
            r0 = j * sb
            cur = u[r0:r0 + sb]
            hp = (u[tm + 7:tm + 8] if j == 0 else u[r0 - 1:r0]) * keep_prev[j]
            hn = (u[tm + 8:tm + 9] if j == nsub - 1 else u[r0 + sb:r0 + sb + 1]) * keep_next[j]
            p0 = jnp.where(sub8 == 0, hp, pltpu.roll(cur[sb - 8:sb], 1, 0))
            nl = jnp.where(sub8 == 7, hn, pltpu.roll(cur[0:8], 7, 0))
            prev = jnp.concatenate([p0, cur[:sb - 8]], axis=0)
            nxt = jnp.concatenate([cur[8:], nl], axis=0)
            w = cw_ref[:, c0:c0 + ck]
            return prev * w[0:1] + cur * w[1:2] + nxt * w[2:3] + cb_ref[:, c0:c0 + ck]

        h = h_ref[k]

        def up(c0):
            return _dot(h, wu_ref[:, c0:c0 + ck]), _dot(h, wu_ref[:, D_FF + c0:D_FF + c0 + ck])

        ahead = up(0)
        for n, c0 in enumerate(range(0, D_FF, ck)):
            ug, uv = ahead
            if c0 + ck < D_FF:
                ahead = up(c0 + ck)
            pieces = []
            for j in range(nsub):
                gt = conv(ug, j, c0)
                vl = conv(uv, j, D_FF + c0)
                pieces.append((gt * _sigmoid(gt) * vl).astype(BF16))
            a = pieces[0] if nsub == 1 else jnp.concatenate(pieces, axis=0)
            d = _dot(a, wd_ref[c0:c0 + ck, :])
            for cb in range(ncb):
                if n == 0:
                    acc_ref[cb] = d[:, cb * 128:(cb + 1) * 128]
                else:
                    acc_ref[cb] += d[:, cb * 128:(cb + 1) * 128]
        fg = fg_ref[...]
        for j in range(nsub):
            gate = _mod_rows(m_ref, 5, row0 + j * sb, ctx_start)
            for q in range(sb // 8):
                rows = slice(j * sb + 8 * q, j * sb + 8 * q + 8)
                x1r = jnp.concatenate([x3_ref[k * ncb + cb, rows, :] for cb in range(ncb)], axis=-1)
                out = x1r + gate * _unperm_rows(acc_ref, j * sb, sb, q)
                if final:
                    ms = jnp.mean(out * out, axis=-1, keepdims=True)
                    out = out * lax.rsqrt(ms + EPS) * fg
                o_ref[base + j * sb + 8 * q:base + j * sb + 8 * q + 8, :] = out

    prepare(0)
    for k in range(nin):
        if k + 1 < nin:
            prepare(k + 1)
        finish(k)


def mixer_out_conv_ffn(xs, t, parts, w_out, g, mod, w_up, conv_w, conv_b, w_down, final_g, tm, nin, sb,
                       seg_starts, ctx_start, final, name):
    bsz = xs.shape[0]
    f2 = 2 * D_FF
    bt = nin * tm
    r8 = bt // 8
    r16 = bt // 16
    kern = functools.partial(_ffn_kernel, nparts=len(parts), tm=tm, nin=nin, t=t, sb=sb, seg_starts=seg_starts,
                             ctx_start=ctx_start, final=final, ck=256)
    const = lambda shape: pl.BlockSpec(shape, lambda b, i: (0,) * len(shape), pipeline_mode=pl.Buffered(1))
    in_specs = [pl.BlockSpec((None, bt, D), lambda b, i: (b, i, 0)),
                pl.BlockSpec((None, 8, D), lambda b, i: (b, jnp.maximum(i * r8 - 1, 0), 0)),
                pl.BlockSpec((None, 8, D), lambda b, i: (b, jnp.minimum((i + 1) * r8, t // 8 - 1), 0))]
    args = [xs, xs, xs]
    for p in parts:
        kp = p.shape[-1]
        in_specs += [pl.BlockSpec((None, bt, kp), lambda b, i: (b, i, 0)),
                     pl.BlockSpec((None, 16, kp), lambda b, i: (b, jnp.maximum(i * r16 - 1, 0), 0)),
                     pl.BlockSpec((None, 16, kp), lambda b, i: (b, jnp.minimum((i + 1) * r16, t // 16 - 1), 0))]
        args += [p, p, p]
    in_specs += [const(w_out.shape), const((1, D)),
                 pl.BlockSpec((None, 2, 6, D), lambda b, i: (b, 0, 0, 0)),
                 const((D, f2)), const((3, f2)), const((1, f2)), const((D_FF, D)), const((1, D))]
    args += [w_out, g.reshape(1, D), mod, w_up, conv_w, conv_b.reshape(1, f2), w_down, final_g.reshape(1, D)]
    return pl.pallas_call(
        kern,
        grid=(bsz, t // bt),
        in_specs=in_specs,
        out_specs=pl.BlockSpec((None, bt, D), lambda b, i: (b, i, 0)),
        out_shape=jax.ShapeDtypeStruct((bsz, t, D), F32),
        scratch_shapes=[pltpu.VMEM((nin, tm + 16, D), BF16),
                        pltpu.VMEM((nin * (D // 128), tm, 128), F32), pltpu.VMEM((D // 128, tm, 128), F32)],
        compiler_params=_cp(("parallel", "parallel")),
        name=name,
    )(*args)


PAD = 8


def _fill_padded(xp_ref, x_ref, width):
    z = jnp.zeros((PAD, width), F32)
    xp_ref[0:PAD, :] = z
    xp_ref[PAD:PAD + S, :] = x_ref[0:S, :].astype(F32)
    xp_ref[PAD + S:2 * PAD + S, :] = z
    xp_ref[2 * PAD + S:2 * PAD + T, :] = x_ref[S:T, :].astype(F32)
    xp_ref[2 * PAD + T:3 * PAD + T, :] = z


def _conv_tile(xp_ref, w, bias, r0, rows, ktaps):
    base = PAD + r0 if r0 < S else 2 * PAD + r0
    left = ktaps // 2
    acc = bias
    for j in range(ktaps):
        acc = acc + xp_ref[base + j - left:base + j - left + rows, :] * w[j:j + 1, :]
    return acc


def _hconv_kernel(z_ref, w_ref, b_ref, o_ref, xp_ref):
    _fill_padded(xp_ref, z_ref, 512)
    w = w_ref[...]
    bias = b_ref[...]
    for r0 in range(0, T, 256):
        o_ref[r0:r0 + 256, :] = _conv_tile(xp_ref, w, bias, r0, 256, 3)


def hyena_short_conv(z, conv_w, conv_b):
    return pl.pallas_call(
        _hconv_kernel,
        grid=(B, 3),
        in_specs=[pl.BlockSpec((None, T, 512), lambda b, j: (b, 0, 2 + j)),
                  pl.BlockSpec((3, 512), lambda b, j: (0, j)),
                  pl.BlockSpec((1, 512), lambda b, j: (0, j))],
        out_specs=pl.BlockSpec((None, T, 512), lambda b, j: (b, 0, j)),
        out_shape=jax.ShapeDtypeStruct((B, T, 3 * HY_W), F32),
        scratch_shapes=[pltpu.VMEM((T + 3 * PAD, 512), F32)],
        compiler_params=_cp(("parallel", "parallel")),
        name="hyena_short_conv",
    )(z, conv_w, conv_b.reshape(1, 3 * HY_W))


RGH = 128
SEGS = ((S, C, C // 8), (0, S, S // 8))
POFF = {S: 0, 0: C + 24}


def _rg_kernel(x_ref, gt_ref, cw_ref, cb_ref, wbd_ref, bias_ref, sp_ref, o_ref,
               xp_ref, af_ref, bf_ref, ar_ref, br_ref, cf_ref, cr_ref):
    sub8 = lax.broadcasted_iota(jnp.int32, (8, 1), 0)
    cw = cw_ref[...]
    cb = cb_ref[...]
    nl2e8 = -RG_C * math.log2(math.e)
    spf = sp_ref[0:1, :] * nl2e8
    spr = sp_ref[1:2, :] * nl2e8

    for seg0, rows, nslot in SEGS:
        p0 = POFF[seg0]

        for r in range(nslot):
            xp_ref[p0 + 16 + 8 * r:p0 + 24 + 8 * r, :] = x_ref[pl.ds(seg0 + r, 8, stride=nslot), :]
        for slot, src, sh in ((0, nslot - 2, 1), (1, nslot - 1, 1), (nslot + 2, 0, 7)):
            v = pltpu.roll(xp_ref[p0 + 16 + 8 * src:p0 + 24 + 8 * src, :], sh, 0)
            xp_ref[p0 + 8 * slot:p0 + 8 * slot + 8, :] = jnp.where(sub8 == (0 if sh == 1 else 7), 0.0, v)

        for t0 in range(0, rows, 256):
            xc = cb
            for j in range(4):
                xc = xc + xp_ref[p0 + 8 * j + t0:p0 + 8 * j + t0 + 256, :] * cw[j:j + 1, :]
            g = _dot(xc.astype(BF16), wbd_ref[...])
            for d, (a_ref, b_ref, sp) in enumerate(((af_ref, bf_ref, spf), (ar_ref, br_ref, spr))):
                rg = _sigmoid(g[:, (2 * d) * RGH:(2 * d + 1) * RGH] + bias_ref[2 * d:2 * d + 1, :])
                ig = _sigmoid(g[:, (2 * d + 1) * RGH:(2 * d + 2) * RGH] + bias_ref[2 * d + 1:2 * d + 2, :])
                a = jnp.exp2(rg * sp)
                a_ref[seg0 + t0:seg0 + t0 + 256, :] = a
                om = 1.0 - a * a
                root = jnp.where(om > 0.0, om * lax.rsqrt(om), 0.0)
                b_ref[seg0 + t0:seg0 + t0 + 256, :] = root * (ig * xc)

    one = jnp.ones((8, RGH), F32)
    zero = jnp.zeros((8, RGH), F32)

    def step(a_ref, b_ref, t, p, h):
        t = pl.multiple_of(t, 8)
        a = a_ref[pl.ds(t, 8), :]
        h = a * h + b_ref[pl.ds(t, 8), :]
        p = a * p
        a_ref[pl.ds(t, 8), :] = p
        b_ref[pl.ds(t, 8), :] = h
        return p, h

    for seg0, rows, nslot in SEGS:
        half = nslot // 2

        def body(i, carry):
            fa, fb, ra, rb = carry
            fa = step(af_ref, bf_ref, seg0 + 8 * i, *fa)
            fb = step(af_ref, bf_ref, seg0 + 8 * (half + i), *fb)
            ra = step(ar_ref, br_ref, seg0 + 8 * (half - 1 - i), *ra)
            rb = step(ar_ref, br_ref, seg0 + 8 * (nslot - 1 - i), *rb)
            return fa, fb, ra, rb

        lax.fori_loop(0, half, body, ((one, zero),) * 4, unroll=4)

    def row(ref, t):
        return ref[t:t + 1, :]

    cf = jnp.zeros((1, RGH), F32)
    cr = jnp.zeros((1, RGH), F32)
    for k, (seg0, rows, nslot) in enumerate(SEGS):
        mid = seg0 + 8 * (nslot // 2)
        last = seg0 + 8 * (nslot - 1)
        for s in range(8):
            cf_ref[16 * k + s:16 * k + s + 1, :] = cf
            cf = row(bf_ref, mid - 8 + s) + row(af_ref, mid - 8 + s) * cf
            cf_ref[16 * k + 8 + s:16 * k + 9 + s, :] = cf
            cf = row(bf_ref, last + s) + row(af_ref, last + s) * cf
        for s in range(7, -1, -1):
            cr_ref[16 * k + 8 + s:16 * k + 9 + s, :] = cr
            cr = row(br_ref, mid + s) + row(ar_ref, mid + s) * cr
            cr_ref[16 * k + s:16 * k + s + 1, :] = cr
            cr = row(br_ref, seg0 + s) + row(ar_ref, seg0 + s) * cr

    for k, (seg0, rows, nslot) in enumerate(SEGS):
        hrows = rows // 2
        for t0 in range(0, rows, 256):
            sel = [16 * k + (0 if t0 + 8 * j < hrows else 8) for j in range(32)]
            cft = jnp.concatenate([cf_ref[o:o + 8, :] for o in sel], axis=0)
            crt = jnp.concatenate([cr_ref[o:o + 8, :] for o in sel], axis=0)
            sl = slice(seg0 + t0, seg0 + t0 + 256)
            bf_ref[sl, :] = (bf_ref[sl, :] + af_ref[sl, :] * cft) + (br_ref[sl, :] + ar_ref[sl, :] * crt)

    for seg0, rows, nslot in SEGS:
        per = nslot // 8
        for q in range(rows // 8):
            s, r0 = divmod(q, per)
            af_ref[seg0 + 8 * q:seg0 + 8 * q + 8, :] = bf_ref[pl.ds(seg0 + 64 * r0 + s, 8, stride=8), :]

    c0 = math.sqrt(2.0 / math.pi)
    for r0 in range(0, T, 256):
        gt = gt_ref[r0:r0 + 256, :].astype(F32)
        gelu = 0.5 * gt * (1.0 + jnp.tanh(c0 * (gt + 0.044715 * (gt * gt * gt))))
        o_ref[r0:r0 + 256, :] = (af_ref[r0:r0 + 256, :] * gelu).astype(o_ref.dtype)


def rglru(z, conv_w, conv_b, wbd, bias4, softplus_neg_lam):
    nh = RG_W // RGH
    return pl.pallas_call(
        _rg_kernel,
        grid=(B, nh),
        in_specs=[pl.BlockSpec((None, T, RGH), lambda b, j: (b, 0, j)),
                  pl.BlockSpec((None, T, RGH), lambda b, j: (b, 0, nh + j)),
                  pl.BlockSpec((4, RGH), lambda b, j: (0, j)),
                  pl.BlockSpec((1, RGH), lambda b, j: (0, j)),
                  pl.BlockSpec((None, RGH, 4 * RGH), lambda b, j: (j, 0, 0)),
                  pl.BlockSpec((4, RGH), lambda b, j: (0, j)),
                  pl.BlockSpec((2, RGH), lambda b, j: (0, j))],
        out_specs=pl.BlockSpec((None, T, RGH), lambda b, j: (b, 0, j)),
        out_shape=jax.ShapeDtypeStruct((B, T, RG_W), BF16),
        scratch_shapes=[pltpu.VMEM((T + 48, RGH), F32)] + [pltpu.VMEM((T, RGH), F32)] * 4
                       + [pltpu.VMEM((32, RGH), F32)] * 2,
        compiler_params=_cp(("parallel", "parallel")),
        name="rglru",
    )(z, z, conv_w, conv_b.reshape(1, RG_W), wbd, bias4, softplus_neg_lam)


def _softplus_kernel(x_ref, o_ref):
    y = -x_ref[...]
    o_ref[...] = jnp.maximum(y, 0.0) + jnp.log(1.0 + jnp.exp(-jnp.abs(y)))


def softplus_neg(lam):
    return pl.pallas_call(
        _softplus_kernel,
        out_shape=jax.ShapeDtypeStruct(lam.shape, F32),
        name="softplus_neg",
    )(lam)


def _hfilt_kernel(z_ref, w1_ref, b1_ref, fr_ref, w2_ref, b2_ref, w3_ref, dec_ref, x1_ref, x2_ref, nyq_ref,
                  s1_ref, s2_ref, *, n, tr):
    fr = fr_ref[...]
    w1h, w1l = _split(w1_ref[...])
    w2h, w2l = _split(w2_ref[...])
    w3h, w3l = _split(w3_ref[...])
    tot = jnp.zeros((1, HY_W), F32)
    alt = jnp.zeros((1, HY_W), F32)
    sign = jnp.where(lax.broadcasted_iota(jnp.int32, (tr, 1), 0) % 2 == 0, 1.0, -1.0)
    for r0 in range(0, n, tr):
        zh, zl = _split(z_ref[r0:r0 + tr, :])
        h = jnp.sin(fr * (_dot3(zh, zl, w1h, w1l) + b1_ref[...]))
        hh, hl = _split(h)
        h = jnp.sin(fr * (_dot3(hh, hl, w2h, w2l) + b2_ref[...]))
        hh, hl = _split(h)
        h = _dot3(hh, hl, w3h, w3l)
        dec = dec_ref[r0:r0 + tr, :]
        h0 = h[:, :HY_W] * dec
        h1 = h[:, HY_W:] * dec
        if r0 == 0:
            h1 = jnp.where(lax.broadcasted_iota(jnp.int32, (tr, 1), 0) == 0, 0.0, h1)
        tot = tot + jnp.sum(jnp.abs(h0) + jnp.abs(h1), axis=0, keepdims=True)
        alt = alt + jnp.sum((h0 + h1) * sign, axis=0, keepdims=True)
        s1_ref[r0:r0 + tr, :] = h0 + h1
        s2_ref[r0:r0 + tr, :] = h0 - h1
    inv = 1.0 / tot
    for r0 in range(0, n, tr):
        x1_ref[r0:r0 + tr, :] = (s1_ref[r0:r0 + tr, :] * inv).astype(BF16)
        x2_ref[r0:r0 + tr, :] = (s2_ref[r0:r0 + tr, :] * inv).astype(BF16)
    nyq_ref[...] = jnp.broadcast_to(alt * inv, (8, HY_W))


def hyena_filter_halves(n, zemb, w1p, b1, freq, w2, b2, w3, decay):
    kern = functools.partial(_hfilt_kernel, n=n, tr=256)
    full = lambda shape: pl.BlockSpec(shape, lambda o: (0,) * len(shape))
    return pl.pallas_call(
        kern,
        grid=(2,),
        in_specs=[full((n, 128)), full((128, HY_HID)), full((1, HY_HID)), full((1, HY_HID)),
                  full((HY_HID, HY_HID)), full((1, HY_HID)),
                  pl.BlockSpec((HY_HID, 2 * HY_W), lambda o: (0, o)),
                  full((n, HY_W))],
        out_specs=[pl.BlockSpec((n, HY_W), lambda o: (0, o)),
                   pl.BlockSpec((n, HY_W), lambda o: (0, o)),
                   pl.BlockSpec((8, HY_W), lambda o: (0, o))],
        out_shape=[jax.ShapeDtypeStruct((n, 2 * HY_W), BF16),
                   jax.ShapeDtypeStruct((n, 2 * HY_W), BF16),
                   jax.ShapeDtypeStruct((8, 2 * HY_W), F32)],
        scratch_shapes=[pltpu.VMEM((n, HY_W), F32), pltpu.VMEM((n, HY_W), F32)],
        compiler_params=_cp(("parallel",)),
        name=f"hyena_filter_{n}",
    )(zemb, w1p, b1.reshape(1, HY_HID), freq.reshape(1, HY_HID), w2, b2.reshape(1, HY_HID), w3, decay)


def _kspec_kernel(at_ref, ab_ref, x1_ref, x2_ref, kt_ref, kb_ref):
    kt_ref[...] = _dot(at_ref[...], x1_ref[...])
    kb_ref[...] = _dot(ab_ref[...], x2_ref[...])


def filter_spectrum(fwd, x1, x2, n, tr):
    nt = n // tr
    return pl.pallas_call(
        _kspec_kernel,
        grid=(nt,),
        in_specs=[pl.BlockSpec((tr, n), lambda i: (i, 0)),
                  pl.BlockSpec((tr, n), lambda i: (nt + i, 0)),
                  pl.BlockSpec((n, 2 * HY_W), lambda i: (0, 0)),
                  pl.BlockSpec((n, 2 * HY_W), lambda i: (0, 0))],
        out_specs=[pl.BlockSpec((tr, 2 * HY_W), lambda i: (i, 0)),
                   pl.BlockSpec((tr, 2 * HY_W), lambda i: (i, 0))],
        out_shape=[jax.ShapeDtypeStruct((n, 2 * HY_W), F32)] * 2,
        compiler_params=_cp(("parallel",)),
        name=f"filter_spectrum_{n}",
    )(fwd, fwd, x1, x2)


def _hfwd_kernel(at_ref, ab_ref, x_ref, kt_ref, kb_ref, kn_ref, yr_ref, yi_ref, xb_ref, *, tr):
    i = pl.program_id(1)

    @pl.when(i == 0)
    def _():
        xb_ref[...] = x_ref[...].astype(BF16)

    ts = min(tr, 512)
    hw = HY_W // 2
    pieces = []
    for r0 in range(0, tr, ts):
        tab = jnp.concatenate([at_ref[r0:r0 + ts, :], ab_ref[r0:r0 + ts, :]], axis=0)
        for c0 in (0, hw):
            pieces.append((r0, c0, _dot(tab, xb_ref[:, c0:c0 + hw])))
    first = (lax.broadcasted_iota(jnp.int32, (ts, 1), 0) == 0) & (i == 0)
    for r0, c0, u in pieces:
        ur = u[:ts]
        ui = u[ts:]
        kr = kt_ref[r0:r0 + ts, c0:c0 + hw]
        ki = kb_ref[r0:r0 + ts, c0:c0 + hw]
        if r0 == 0:
            yr = ur * kr - jnp.where(first, 0.0, ui * ki)
            yi = jnp.where(first, ui * kn_ref[0:1, c0:c0 + hw], ur * ki + ui * kr)
        else:
            yr = ur * kr - ui * ki
            yi = ur * ki + ui * kr
        yr_ref[r0:r0 + ts, c0:c0 + hw] = yr.astype(BF16)
        yi_ref[r0:r0 + ts, c0:c0 + hw] = yi.astype(BF16)


def hyena_forward(fwd, x, x_rb, kt, kb, nyq, order, n, tr, name):
    nt = n // tr
    kern = functools.partial(_hfwd_kernel, tr=tr)
    return pl.pallas_call(
        kern,
        grid=(B, nt),
        in_specs=[pl.BlockSpec((tr, n), lambda b, i: (i, 0)),
                  pl.BlockSpec((tr, n), lambda b, i: (nt + i, 0)),
                  pl.BlockSpec((None, n, HY_W), lambda b, i: (b, x_rb, 0)),
                  pl.BlockSpec((tr, HY_W), lambda b, i: (i, order)),
                  pl.BlockSpec((tr, HY_W), lambda b, i: (i, order)),
                  pl.BlockSpec((8, HY_W), lambda b, i: (0, order))],
        out_specs=[pl.BlockSpec((None, tr, HY_W), lambda b, i: (b, i, 0)),
                   pl.BlockSpec((None, tr, HY_W), lambda b, i: (b, i, 0))],
        out_shape=[jax.ShapeDtypeStruct((B, n, HY_W), BF16)] * 2,
        scratch_shapes=[pltpu.VMEM((n, HY_W), BF16)],
        compiler_params=_cp(("parallel", "arbitrary")),
        name=name,
    )(fwd, fwd, x, kt, kb, nyq)


def _hinv_kernel(al_ref, ar_ref, yr_ref, yi_ref, up_ref, gt_ref, bias_ref, o_ref):
    y = _dot(al_ref[...], yr_ref[...]) + _dot(ar_ref[...], yi_ref[...])
    o_ref[...] = (gt_ref[...] * (y + up_ref[...].astype(F32) * bias_ref[...])).astype(o_ref.dtype)


def hyena_inverse(inv, yr, yi, uprev, uprev_rb, gate, gate_rb, gate_cb, bias, n, tmi, out_dtype, name):
    return pl.pallas_call(
        _hinv_kernel,
        grid=(B, n // tmi),
        in_specs=[pl.BlockSpec((tmi, n), lambda b, i: (i, 0)),
                  pl.BlockSpec((tmi, n), lambda b, i: (i, 1)),
                  pl.BlockSpec((None, n, HY_W), lambda b, i: (b, 0, 0)),
                  pl.BlockSpec((None, n, HY_W), lambda b, i: (b, 0, 0)),
                  pl.BlockSpec((None, tmi, HY_W), lambda b, i: (b, uprev_rb + i, 0)),
                  pl.BlockSpec((None, tmi, HY_W), lambda b, i: (b, gate_rb + i, gate_cb)),
                  pl.BlockSpec((1, HY_W), lambda b, i: (0, 0))],
        out_specs=pl.BlockSpec((None, tmi, HY_W), lambda b, i: (b, i, 0)),
        out_shape=jax.ShapeDtypeStruct((B, n, HY_W), out_dtype),
        compiler_params=_cp(("parallel", "parallel")),
        name=name,
    )(inv, inv, yr, yi, uprev, gate, bias.reshape(1, HY_W))


def _rope128(r, c, sa, sb):
    return r * c + pltpu.roll(r, 32, 1) * sa + pltpu.roll(r, 96, 1) * sb


def _qkv_kernel(x_ref, g_ref, m_ref, win_ref, qg_ref, kvg_ref, wq_ref, wk_ref, wv_ref, c_ref, sa_ref, sb_ref,
                q_ref, k_ref, v_ref, *, tm):
    x = x_ref[...]
    y = x * lax.rsqrt(jnp.mean(x * x, axis=-1, keepdims=True) + EPS) * g_ref[...]
    row = pl.program_id(1) * tm + lax.broadcasted_iota(jnp.int32, (tm, 1), 0)
    h = (y * (1.0 + _mod_rows(m_ref, 1, row, S)) + _mod_rows(m_ref, 0, row, S)).astype(BF16)
    z = _dot(h, win_ref[...])
    c = c_ref[...]
    sa = sa_ref[...]
    sb = sb_ref[...]
    zq = z[:, :Q_LORA]
    qn = (zq * lax.rsqrt(jnp.mean(zq * zq, axis=-1, keepdims=True) + EPS) * qg_ref[...]).astype(BF16)
    zkv = z[:, Q_LORA:Q_LORA + KV_LORA]
    ckv = (zkv * lax.rsqrt(jnp.mean(zkv * zkv, axis=-1, keepdims=True) + EPS) * kvg_ref[...]).astype(BF16)
    kr = _rope128(z[:, Q_LORA + KV_LORA:], c, sa, sb).astype(BF16)
    for h in range(HEADS):
        q = _dot(qn, wq_ref[:, h * HQ:(h + 1) * HQ])
        q_ref[:, h * HQ:h * HQ + NOPE] = q[:, :NOPE].astype(BF16)
        q_ref[:, h * HQ + NOPE:(h + 1) * HQ] = _rope128(q[:, NOPE:], c, sa, sb).astype(BF16)
        k_ref[:, h * HQ:h * HQ + NOPE] = _dot(ckv, wk_ref[:, h * NOPE:(h + 1) * NOPE]).astype(BF16)
        k_ref[:, h * HQ + NOPE:(h + 1) * HQ] = kr
    v_ref[...] = _dot(ckv, wv_ref[...]).astype(BF16)


def mla_qkv(xs, g, mod, w_in, q_g, kv_g, wq, wk, wv, ctab, satab, sbtab, tm):
    full = lambda shape: pl.BlockSpec(shape, lambda b, i: (0,) * len(shape))
    tab = pl.BlockSpec((tm, 128), lambda b, i: (i, 0))
    return pl.pallas_call(
        functools.partial(_qkv_kernel, tm=tm),
        grid=(B, T // tm),
        in_specs=[pl.BlockSpec((None, tm, D), lambda b, i: (b, i, 0)),
                  full((1, D)),
                  pl.BlockSpec((None, 2, 6, D), lambda b, i: (b, 0, 0, 0)),
                  full((D, ODD_IN_P)),
                  full((1, Q_LORA)), full((1, KV_LORA)),
                  full((Q_LORA, HEADS * HQ)), full((KV_LORA, HEADS * NOPE)), full((KV_LORA, HEADS * VH)),
                  tab, tab, tab],
        out_specs=[pl.BlockSpec((None, tm, HEADS * HQ), lambda b, i: (b, i, 0)),
                   pl.BlockSpec((None, tm, HEADS * HQ), lambda b, i: (b, i, 0)),
                   pl.BlockSpec((None, tm, HEADS * VH), lambda b, i: (b, i, 0))],
        out_shape=[jax.ShapeDtypeStruct((B, T, HEADS * HQ), BF16),
                   jax.ShapeDtypeStruct((B, T, HEADS * HQ), BF16),
                   jax.ShapeDtypeStruct((B, T, HEADS * VH), BF16)],
        compiler_params=_cp(("parallel", "parallel")),
        name="mla_qkv",
    )(xs, g.reshape(1, D), mod, w_in, q_g.reshape(1, Q_LORA), kv_g.reshape(1, KV_LORA), wq, wk, wv,
      ctab, satab, sbtab)


def _attn_kernel(q_ref, k_ref, v_ref, o_ref, *, scale, ts):
    c = scale * math.log2(math.e)
    kc = 256
    tq = q_ref.shape[0]
    nh = q_ref.shape[1] // HQ
    tiles = [(h, r0) for h in range(nh) for r0 in range(0, tq, ts)]

    def scores(h, r0):
        return lax.dot_general(q_ref[r0:r0 + ts, h * HQ:(h + 1) * HQ], k_ref[:, h * HQ:(h + 1) * HQ],
                               (((1,), (1,)), ((), ())), preferred_element_type=F32)

    ahead = scores(*tiles[0])
    for n, (h, r0) in enumerate(tiles):
        s = ahead
        if n + 1 < len(tiles):
            ahead = scores(*tiles[n + 1])
        m = jnp.max(s, axis=-1, keepdims=True)
        acc = None
        lv = None
        for c0 in range(0, T, kc):
            p = jnp.exp2((s[:, c0:c0 + kc] - m) * c)
            pl_ = p[:, :128]
            for l0 in range(128, kc, 128):
                pl_ = pl_ + p[:, l0:l0 + 128]
            lv = pl_ if lv is None else lv + pl_
            d = _dot(p.astype(BF16), v_ref[c0:c0 + kc, h * VH:(h + 1) * VH])
            acc = d if acc is None else acc + d
        l = jnp.sum(lv, axis=-1, keepdims=True)
        o_ref[r0:r0 + ts, h * VH:(h + 1) * VH] = (acc * (1.0 / l)).astype(o_ref.dtype)


def mla_attention(q, k, v, tq, hp):
    kern = functools.partial(_attn_kernel, scale=(NOPE + ROPE) ** -0.5, ts=512)
    return pl.pallas_call(
        kern,
        grid=(B, HEADS // hp, S // tq),
        in_specs=[pl.BlockSpec((None, tq, hp * HQ), lambda b, h, i: (b, i, h)),
                  pl.BlockSpec((None, T, hp * HQ), lambda b, h, i: (b, 0, h)),
                  pl.BlockSpec((None, T, hp * VH), lambda b, h, i: (b, 0, h))],
        out_specs=pl.BlockSpec((None, tq, hp * VH), lambda b, h, i: (b, i, h)),
        out_shape=jax.ShapeDtypeStruct((B, S, HEADS * VH), BF16),
        compiler_params=_cp(("parallel", "parallel", "parallel")),
        name="mla_attention",
    )(q, k, v)


@functools.lru_cache(maxsize=None)
def _dft_tables(n):
    m = 2 * n
    j = np.arange(n, dtype=np.int64)
    kk = np.arange(n, dtype=np.int64)
    ang = 2.0 * np.pi * ((kk[:, None] * j[None, :]) % m).astype(np.float64) / m
    cosm = np.cos(ang)
    sinm = np.sin(ang)
    fwd = np.concatenate([cosm, -sinm], axis=0)
    fwd[n, :] = np.where(j % 2 == 0, 1.0, -1.0)
    inv = np.concatenate([2.0 * cosm.T, -2.0 * sinm.T], axis=1) / m
    inv[:, 0] = 1.0 / m
    inv[:, n] = np.where(j % 2 == 0, 1.0, -1.0) / m
    return fwd.astype(np.float32).astype(BF16), inv.astype(np.float32).astype(BF16)


def _filter_consts(n):
    pos = jnp.arange(n, dtype=F32)
    t = jnp.linspace(0.0, 1.0, n, dtype=F32)[:, None]
    bands = (HY_EMB - 1) // 2
    w = 2.0 * math.pi * pos / n
    f = jnp.linspace(1e-4, bands - 1, bands, dtype=F32)
    ang = w[:, None] * f[None, :]
    z = jnp.concatenate([t, jnp.cos(ang), -jnp.sin(ang)], axis=-1)
    z = jnp.pad(z, ((0, 0), (0, 128 - HY_EMB)))
    max_decay = math.log(1e-2) / 0.3
    min_decay = math.log(1e-2) / 1.5
    deltas = jnp.linspace(min_decay, max_decay, HY_W, dtype=F32)
    decay = jnp.exp(-t * jnp.abs(deltas)[None, :])
    return z, decay


def _rope_tables():
    rows = S // 64
    row = jnp.repeat(jnp.arange(rows, dtype=F32), 64)
    col = jnp.tile(jnp.arange(64, dtype=F32), rows)
    n_freq = ROPE // 4
    inv = 10000.0 ** (-jnp.arange(n_freq, dtype=F32) / n_freq)
    ang = jnp.concatenate([row[:, None] * inv[None, :], col[:, None] * inv[None, :]], axis=-1)
    cos = jnp.concatenate([jnp.cos(ang), jnp.ones((C, 32), F32)], axis=0)
    sin = jnp.concatenate([jnp.sin(ang), jnp.zeros((C, 32), F32)], axis=0)
    z32 = jnp.zeros((T, 32), F32)
    z64 = jnp.zeros((T, 64), F32)
    ctab = jnp.concatenate([cos, cos, z64], axis=-1)
    satab = jnp.concatenate([z32, sin, z64], axis=-1)
    sbtab = jnp.concatenate([-sin, z32, z64], axis=-1)
    return ctab, satab, sbtab


def _even_layer(x, ctx, mod, norm1_g, w_in, rg_conv_w, rg_conv_b, rg_wa, rg_ba, rg_wx, rg_bx, rg_lambda,
                hy_conv_w, hy_conv_b, f_w1, f_b1, f_freq, f_w2, f_b2, f_w3, hy_bias, w_out):
    z, xs = even_in_proj(x, ctx, norm1_g, mod, w_in.astype(BF16))

    def blockdiag(w):
        per = RGH // RG_HD
        out = jnp.zeros((RG_W // RGH, RGH, RGH), F32)
        for h in range(RG_HEADS):
            g, l = divmod(h, per)
            out = out.at[g, l * RG_HD:(l + 1) * RG_HD, l * RG_HD:(l + 1) * RG_HD].set(w[h])
        return out

    wbd = jnp.concatenate([blockdiag(rg_wa[0]), blockdiag(rg_wx[0]), blockdiag(rg_wa[1]), blockdiag(rg_wx[1])],
                          axis=-1).astype(BF16)
    bias4 = jnp.stack([rg_ba[0], rg_bx[0], rg_ba[1], rg_bx[1]], axis=0)
    a = rglru(z, rg_conv_w, rg_conv_b, wbd, bias4, softplus_neg(rg_lambda))

    vc = hyena_short_conv(z, hy_conv_w, hy_conv_b)
    w1p = jnp.pad(f_w1, ((0, 128 - HY_EMB), (0, 0)))
    outs = []
    for n, rb in ((S, 0), (C, S // C)):
        fwd, inv = _dft_tables(n)
        zemb, decay = _filter_consts(n)
        x1, x2, nyq = hyena_filter_halves(n, zemb, w1p, f_b1, f_freq, f_w2, f_b2, f_w3, decay)
        tr = min(n, 512)
        tmi = min(n, 1024)
        kt, kb = filter_spectrum(fwd, x1, x2, n, tr)
        grb = rb * (n // tmi)
        yr, yi = hyena_forward(fwd, vc, rb, kt, kb, nyq, 0, n, tmi, f"hyena_fwd1_{n}")
        u1 = hyena_inverse(inv, yr, yi, vc, grb, vc, grb, 1, hy_bias[0], n, tmi, F32, f"hyena_inv1_{n}")
        yr, yi = hyena_forward(fwd, u1, 0, kt, kb, nyq, 1, n, tmi, f"hyena_fwd2_{n}")
        outs.append(hyena_inverse(inv, yr, yi, u1, 0, vc, grb, 2, hy_bias[1], n, tmi, BF16,
                                  f"hyena_inv2_{n}"))
    return ([a, jnp.concatenate(outs, axis=1)], w_out.astype(BF16)), xs


def _odd_layer(xs, mod, norm1_g, w_in, q_g, kv_g, w_uq, w_ukv, w_o):
    perm = np.concatenate([np.arange(0, ROPE, 2), np.arange(1, ROPE, 2)])
    w_rope = jnp.pad(w_in[:, Q_LORA + KV_LORA:][:, perm], ((0, 0), (0, 128 - ROPE)))
    w_in_p = jnp.concatenate([w_in[:, :Q_LORA + KV_LORA], w_rope], axis=-1).astype(BF16)
    wq = w_uq.reshape(Q_LORA, HEADS, NOPE + ROPE)
    wq = jnp.concatenate([wq[..., :NOPE], wq[..., NOPE:][..., perm],
                          jnp.zeros((Q_LORA, HEADS, HQ - NOPE - ROPE), F32)], axis=-1)
    wq = wq.reshape(Q_LORA, HEADS * HQ).astype(BF16)
    wkv = w_ukv.reshape(KV_LORA, HEADS, NOPE + VH)
    wk = wkv[..., :NOPE].reshape(KV_LORA, HEADS * NOPE).astype(BF16)
    wv = wkv[..., NOPE:].reshape(KV_LORA, HEADS * VH).astype(BF16)
    ctab, satab, sbtab = _rope_tables()
    q, k, v = mla_qkv(xs, norm1_g, mod, w_in_p, q_g, kv_g, wq, wk, wv, ctab, satab, sbtab, 768)
    return [mla_attention(q, k, v, 2048, 2)], w_o.astype(BF16)


def _ffn(xs, t, mix, mod, norm2_g, w_up, conv_w, conv_b, w_down, final_g, tm, nin, sb, seg_starts, ctx_start,
         final, tag):
    parts, w_out = mix
    return mixer_out_conv_ffn(xs, t, parts, w_out, norm2_g, mod, w_up.astype(BF16), conv_w, conv_b,
                              w_down.astype(BF16), final_g, tm, nin, sb, seg_starts, ctx_start, final,
                              f"mix_out_conv_ffn_{tag}")


def kernel(x, c, ctx, c_ctx, ada_w, ada_b, norm1_g, norm2_g, ev_w_in, ev_rg_conv_w, ev_rg_conv_b, ev_rg_wa, ev_rg_ba, ev_rg_wx, ev_rg_bx, ev_rg_lambda, ev_hy_conv_w, ev_hy_conv_b, ev_hy_f_w1, ev_hy_f_b1, ev_hy_f_freq, ev_hy_f_w2, ev_hy_f_b2, ev_hy_f_w3, ev_hy_bias, ev_w_out, od_w_in, od_q_norm_g, od_kv_norm_g, od_w_uq, od_w_ukv, od_w_o, ffn_w_up, ffn_conv_w, ffn_conv_b, ffn_w_down, final_g):
    cvec = jnp.concatenate([c, c_ctx[None], jnp.zeros((16 - B - 1, D), F32)], axis=0)
    mod = modulation(cvec, ada_w, ada_b)

    def mod_sel(layer):
        lat = mod[layer, :B].reshape(B, 1, 6, D)
        cx = jnp.broadcast_to(mod[layer, B].reshape(1, 1, 6, D), (B, 1, 6, D))
        return jnp.concatenate([lat, cx], axis=1)

    m0 = mod_sel(0)
    mix, xs = _even_layer(x, ctx, m0, norm1_g[0], ev_w_in[0], ev_rg_conv_w[0], ev_rg_conv_b[0], ev_rg_wa[0], ev_rg_ba[0],
                      ev_rg_wx[0], ev_rg_bx[0], ev_rg_lambda[0], ev_hy_conv_w[0], ev_hy_conv_b[0], ev_hy_f_w1[0],
                      ev_hy_f_b1[0], ev_hy_f_freq[0], ev_hy_f_w2[0], ev_hy_f_b2[0], ev_hy_f_w3[0], ev_hy_bias[0],
                      ev_w_out[0])
    xs = _ffn(xs, T, mix, m0, norm2_g[0], ffn_w_up[0], ffn_conv_w[0], ffn_conv_b[0], ffn_w_down[0], final_g,
              384, 2, 128, (0, S), S, False, "0")
    m1 = mod_sel(1)
    mix = _odd_layer(xs, m1, norm1_g[1], od_w_in[0], od_q_norm_g[0], od_kv_norm_g[0], od_w_uq[0], od_w_ukv[0],
                     od_w_o[0])
    return _ffn(xs, S, mix, m1, norm2_g[1], ffn_w_up[1], ffn_conv_w[1], ffn_conv_b[1], ffn_w_down[1], final_g,
                256, 2, 128, (0,), None, True, "1")
```

```python
import functools
import math

import numpy as np
import jax
import jax.numpy as jnp
from jax import lax
from jax.experimental import pallas as pl
from jax.experimental.pallas import tpu as pltpu

F32 = jnp.float32
BF16 = jnp.bfloat16

D = 1024
B = 8
S = 2048
C = 256
T = S + C
EPS = 1e-6

RG_W = 512
RG_HEADS = 8
RG_HD = 64
RG_C = 8.0
HY_W = 512
HY_EMB = 33
HY_HID = 64
EVEN_IN = 2560

HEADS = 8
Q_LORA = 512
KV_LORA = 256
NOPE = 128
ROPE = 64
VH = 128
HQ = 256
ODD_IN_P = 896
D_FF = 2816

VMEM_LIMIT = 52 * 1024 * 1024


def _cp(sem):
    return pltpu.CompilerParams(dimension_semantics=sem, vmem_limit_bytes=VMEM_LIMIT)


def _dot(a, b):
    return jnp.dot(a, b, preferred_element_type=F32)


def _split(x):
    hi = x.astype(BF16)
    lo = (x - hi.astype(F32)).astype(BF16)
    return hi, lo


def _dot3(ah, al, bh, bl):
    return _dot(ah, bh) + (_dot(ah, bl) + _dot(al, bh))


def _sigmoid(x):
    return 1.0 / (1.0 + jnp.exp2(x * (-math.log2(math.e))))


def _mod_rows(m_ref, k, row, ctx_start):
    lat = m_ref[0, k:k + 1, :]
    if ctx_start is None:
        return lat
    return jnp.where(row >= ctx_start, m_ref[1, k:k + 1, :], lat)


def _strided_rows(ref3, start, stride, cb0=0):
    return jnp.concatenate([ref3[cb0 + cb, pl.ds(start, 8, stride=stride), :] for cb in range(D // 128)],
                           axis=-1)


def _perm_rows(ref3, base, nrows, r, cb0=0):
    return _strided_rows(ref3, base + r, nrows // 8, cb0)


def _unperm_rows(ref3, base, nrows, q):
    s, r0 = divmod(8 * q, nrows // 8)
    return _strided_rows(ref3, base + 8 * r0 + s, 8)


def _mod_kernel(c_ref, w_ref, b_ref, o_ref):
    c = c_ref[...]
    s = (c * _sigmoid(c)).astype(BF16)
    o_ref[...] = _dot(s, w_ref[...].astype(BF16)) + b_ref[...]


def modulation(cvec, ada_w, ada_b):
    depth = ada_w.shape[0]
    tn = 1024
    return pl.pallas_call(
        _mod_kernel,
        grid=(depth, 6 * D // tn),
        in_specs=[pl.BlockSpec((16, D), lambda l, j: (0, 0)),
                  pl.BlockSpec((None, D, tn), lambda l, j: (l, 0, j)),
                  pl.BlockSpec((None, 1, tn), lambda l, j: (l, 0, j))],
        out_specs=pl.BlockSpec((None, 16, tn), lambda l, j: (l, 0, j)),
        out_shape=jax.ShapeDtypeStruct((depth, 16, 6 * D), F32),
        compiler_params=_cp(("parallel", "parallel")),
        name="modulation",
    )(cvec, ada_w, ada_b.reshape(depth, 1, 6 * D))


def _even_in_kernel(xa_ref, xb_ref, xc_ref, ctx_ref, g_ref, m_ref, w_ref, z_ref, xs_ref, h_ref, *, tm, nchunk):
    i = pl.program_id(1)
    third = jnp.where(i == T // tm - 1, ctx_ref[...], xc_ref[...])
    x = jnp.concatenate([xa_ref[...], xb_ref[...], third], axis=0)
    xs_ref[...] = x
    ms = jnp.mean(x * x, axis=-1, keepdims=True)
    y = x * lax.rsqrt(ms + EPS) * g_ref[...]
    row = i * tm + lax.broadcasted_iota(jnp.int32, (tm, 1), 0)
    h_ref[...] = (y * (1.0 + _mod_rows(m_ref, 1, row, S)) + _mod_rows(m_ref, 0, row, S)).astype(BF16)
    n = z_ref.shape[-1]
    for n0 in range(0, n, nchunk):
        z_ref[:, n0:n0 + nchunk] = _dot(h_ref[...], w_ref[:, n0:n0 + nchunk])


def even_in_proj(x, ctx, g, mod, w):
    tm = 3 * C
    n = w.shape[1]
    last = S // C - 1
    xblk = lambda k: pl.BlockSpec((None, C, D), lambda b, i: (b, jnp.minimum(3 * i + k, last), 0))
    return pl.pallas_call(
        functools.partial(_even_in_kernel, tm=tm, nchunk=512),
        grid=(B, T // tm),
        in_specs=[xblk(0), xblk(1), xblk(2),
                  pl.BlockSpec((None, C, D), lambda b, i: (b, 0, 0)),
                  pl.BlockSpec((1, D), lambda b, i: (0, 0)),
                  pl.BlockSpec((None, 2, 6, D), lambda b, i: (b, 0, 0, 0)),
                  pl.BlockSpec((D, n), lambda b, i: (0, 0))],
        out_specs=[pl.BlockSpec((None, tm, n), lambda b, i: (b, i, 0)),
                   pl.BlockSpec((None, tm, D), lambda b, i: (b, i, 0))],
        out_shape=[jax.ShapeDtypeStruct((B, T, n), F32), jax.ShapeDtypeStruct((B, T, D), F32)],
        scratch_shapes=[pltpu.VMEM((tm, D), BF16)],
        compiler_params=_cp(("parallel", "parallel")),
        name="even_in_proj",
    )(x, x, x, ctx, g.reshape(1, D), mod, w)


def _ffn_kernel(*refs, nparts, tm, nin, t, sb, seg_starts, ctx_start, final, ck):
    x_ref, xp_ref, xn_ref = refs[:3]
    parts = [refs[3 + 3 * p:6 + 3 * p] for p in range(nparts)]
    (wo_ref, g_ref, m_ref, wu_ref, cw_ref, cb_ref, wd_ref, fg_ref, o_ref,
     h_ref, x3_ref, acc_ref) = refs[3 + 3 * nparts:]
    i = pl.program_id(1)
    nsub = tm // sb
    ncb = D // 128
    sub8 = lax.broadcasted_iota(jnp.int32, (8, 1), 0)
    g = g_ref[...]

    def normmod(x, row):
        ms = jnp.mean(x * x, axis=-1, keepdims=True)
        y = x * lax.rsqrt(ms + EPS) * g
        return y * (1.0 + _mod_rows(m_ref, 4, row, ctx_start)) + _mod_rows(m_ref, 3, row, ctx_start)

    def prepare(k):
        base = k * tm
        row0 = i * nin * tm + base
        mix = None
        k0 = 0
        for p_ref, pp_ref, pn_ref in parts:
            kp = p_ref.shape[-1]
            before = pp_ref[...] if k == 0 else p_ref[base - 16:base, :]
            after = pn_ref[...] if k == nin - 1 else p_ref[base + tm:base + tm + 16, :]
            d = _dot(jnp.concatenate([p_ref[base:base + tm, :], before, after], axis=0), wo_ref[k0:k0 + kp, :])
            mix = d if mix is None else mix + d
            k0 += kp
        xp = xp_ref[...] if k == 0 else x_ref[base - 8:base, :]
        xn = xn_ref[...] if k == nin - 1 else x_ref[base + tm:base + tm + 8, :]
        row = row0 + lax.broadcasted_iota(jnp.int32, (tm, 1), 0)
        x1 = x_ref[base:base + tm, :] + _mod_rows(m_ref, 2, row, ctx_start) * mix[:tm]
        x1p = xp + _mod_rows(m_ref, 2, row0 - 1, ctx_start) * mix[tm + 8:tm + 16]
        x1n = xn + _mod_rows(m_ref, 2, row0 + tm, ctx_start) * mix[tm + 16:tm + 24]
        for cb in range(ncb):
            x3_ref[k * ncb + cb] = x1[:, cb * 128:(cb + 1) * 128]
        for j in range(nsub):
            for r in range(0, sb // 8, 2):
                x = jnp.concatenate([_perm_rows(x3_ref, j * sb, sb, r, k * ncb),
                                     _perm_rows(x3_ref, j * sb, sb, r + 1, k * ncb)], axis=0)
                h_ref[k, j * sb + 8 * r:j * sb + 8 * r + 16, :] = normmod(x, row0 + j * sb).astype(BF16)
        h_ref[k, tm:tm + 16, :] = jnp.concatenate([normmod(x1p, row0 - 1), normmod(x1n, row0 + tm)],
                                                  axis=0).astype(BF16)

    def finish(k):
        base = k * tm
        row0 = i * nin * tm + base
        keep_prev = []
        keep_next = []
        for j in range(nsub):
            r0 = row0 + j * sb
            kp = r0 != seg_starts[0]
            kn = r0 + sb != t
            for s0 in seg_starts[1:]:
                kp = kp & (r0 != s0)
                kn = kn & (r0 + sb != s0)
            keep_prev.append(jnp.where(kp, 1.0, 0.0))
            keep_next.append(jnp.where(kn, 1.0, 0.0))

        def conv(u, j, c0):
            r0 = j * sb
            cur = u[r0:r0 + sb]
            hp = (u[tm + 7:tm + 8] if j == 0 else u[r0 - 1:r0]) * keep_prev[j]
            hn = (u[tm + 8:tm + 9] if j == nsub - 1 else u[r0 + sb:r0 + sb + 1]) * keep_next[j]
            p0 = jnp.where(sub8 == 0, hp, pltpu.roll(cur[sb - 8:sb], 1, 0))
            nl = jnp.where(sub8 == 7, hn, pltpu.roll(cur[0:8], 7, 0))
            prev = jnp.concatenate([p0, cur[:sb - 8]], axis=0)
            nxt = jnp.concatenate([cur[8:], nl], axis=0)
            w = cw_ref[:, c0:c0 + ck]
            return prev * w[0:1] + cur * w[1:2] + nxt * w[2:3] + cb_ref[:, c0:c0 + ck]

        h = h_ref[k]

        def up(c0):
            return _dot(h, wu_ref[:, c0:c0 + ck]), _dot(h, wu_ref[:, D_FF + c0:D_FF + c0 + ck])

        ahead = up(0)
        for n, c0 in enumerate(range(0, D_FF, ck)):
            ug, uv = ahead
            if c0 + ck < D_FF:
                ahead = up(c0 + ck)
            pieces = []
            for j in range(nsub):
                gt = conv(ug, j, c0)
                vl = conv(uv, j, D_FF + c0)
                pieces.append((gt * _sigmoid(gt) * vl).astype(BF16))
            a = pieces[0] if nsub == 1 else jnp.concatenate(pieces, axis=0)
            d = _dot(a, wd_ref[c0:c0 + ck, :])
            for cb in range(ncb):
                if n == 0:
                    acc_ref[cb] = d[:, cb * 128:(cb + 1) * 128]
                else:
                    acc_ref[cb] += d[:, cb * 128:(cb + 1) * 128]
        fg = fg_ref[...]
        for j in range(nsub):
            gate = _mod_rows(m_ref, 5, row0 + j * sb, ctx_start)
            for q in range(sb // 8):
                rows = slice(j * sb + 8 * q, j * sb + 8 * q + 8)
                x1r = jnp.concatenate([x3_ref[k * ncb + cb, rows, :] for cb in range(ncb)], axis=-1)
                out = x1r + gate * _unperm_rows(acc_ref, j * sb, sb, q)
                if final:
                    ms = jnp.mean(out * out, axis=-1, keepdims=True)
                    out = out * lax.rsqrt(ms + EPS) * fg
                o_ref[base + j * sb + 8 * q:base + j * sb + 8 * q + 8, :] = out

    prepare(0)
    for k in range(nin):
        if k + 1 < nin:
            prepare(k + 1)
        finish(k)


def mixer_out_conv_ffn(xs, t, parts, w_out, g, mod, layer, w_up, conv_w, conv_b, w_down, final_g, tm, nin, sb,
                       seg_starts, ctx_start, final, name):
    bsz = xs.shape[0]
    f2 = 2 * D_FF
    bt = nin * tm
    r8 = bt // 8
    r16 = bt // 16
    kern = functools.partial(_ffn_kernel, nparts=len(parts), tm=tm, nin=nin, t=t, sb=sb, seg_starts=seg_starts,
                             ctx_start=ctx_start, final=final, ck=256)
    const = lambda shape: pl.BlockSpec(shape, lambda b, i: (0,) * len(shape), pipeline_mode=pl.Buffered(1))
    layered = lambda shape: pl.BlockSpec((None,) + shape, lambda b, i: (layer, 0, 0),
                                         pipeline_mode=pl.Buffered(1))
    in_specs = [pl.BlockSpec((None, bt, D), lambda b, i: (b, i, 0)),
                pl.BlockSpec((None, 8, D), lambda b, i: (b, jnp.maximum(i * r8 - 1, 0), 0)),
                pl.BlockSpec((None, 8, D), lambda b, i: (b, jnp.minimum((i + 1) * r8, t // 8 - 1), 0))]
    args = [xs, xs, xs]
    for p in parts:
        kp = p.shape[-1]
        in_specs += [pl.BlockSpec((None, bt, kp), lambda b, i: (b, i, 0)),
                     pl.BlockSpec((None, 16, kp), lambda b, i: (b, jnp.maximum(i * r16 - 1, 0), 0)),
                     pl.BlockSpec((None, 16, kp), lambda b, i: (b, jnp.minimum((i + 1) * r16, t // 16 - 1), 0))]
        args += [p, p, p]
    in_specs += [const(w_out.shape), const((1, D)),
                 pl.BlockSpec((None, 2, 6, D), lambda b, i: (b, 0, 0, 0)),
                 layered((D, f2)), const((3, f2)), const((1, f2)), layered((D_FF, D)), const((1, D))]
    args += [w_out, g.reshape(1, D), mod, w_up, conv_w, conv_b.reshape(1, f2), w_down, final_g.reshape(1, D)]
    return pl.pallas_call(
        kern,
        grid=(bsz, t // bt),
        in_specs=in_specs,
        out_specs=pl.BlockSpec((None, bt, D), lambda b, i: (b, i, 0)),
        out_shape=jax.ShapeDtypeStruct((bsz, t, D), F32),
        scratch_shapes=[pltpu.VMEM((nin, tm + 16, D), BF16),
                        pltpu.VMEM((nin * (D // 128), tm, 128), F32), pltpu.VMEM((D // 128, tm, 128), F32)],
        compiler_params=_cp(("parallel", "parallel")),
        name=name,
    )(*args)


PAD = 8


def _fill_padded(xp_ref, x_ref, width):
    z = jnp.zeros((PAD, width), F32)
    xp_ref[0:PAD, :] = z
    xp_ref[PAD:PAD + S, :] = x_ref[0:S, :].astype(F32)
    xp_ref[PAD + S:2 * PAD + S, :] = z
    xp_ref[2 * PAD + S:2 * PAD + T, :] = x_ref[S:T, :].astype(F32)
    xp_ref[2 * PAD + T:3 * PAD + T, :] = z


def _conv_tile(xp_ref, w, bias, r0, rows, ktaps):
    base = PAD + r0 if r0 < S else 2 * PAD + r0
    left = ktaps // 2
    acc = bias
    for j in range(ktaps):
        acc = acc + xp_ref[base + j - left:base + j - left + rows, :] * w[j:j + 1, :]
    return acc


def _hconv_kernel(z_ref, w_ref, b_ref, o_ref, xp_ref):
    _fill_padded(xp_ref, z_ref, 512)
    w = w_ref[...]
    bias = b_ref[...]
    for r0 in range(0, T, 256):
        o_ref[r0:r0 + 256, :] = _conv_tile(xp_ref, w, bias, r0, 256, 3)


def hyena_short_conv(z, conv_w, conv_b):
    return pl.pallas_call(
        _hconv_kernel,
        grid=(B, 3),
        in_specs=[pl.BlockSpec((None, T, 512), lambda b, j: (b, 0, 2 + j)),
                  pl.BlockSpec((3, 512), lambda b, j: (0, j)),
                  pl.BlockSpec((1, 512), lambda b, j: (0, j))],
        out_specs=pl.BlockSpec((None, T, 512), lambda b, j: (b, 0, j)),
        out_shape=jax.ShapeDtypeStruct((B, T, 3 * HY_W), F32),
        scratch_shapes=[pltpu.VMEM((T + 3 * PAD, 512), F32)],
        compiler_params=_cp(("parallel", "parallel")),
        name="hyena_short_conv",
    )(z, conv_w, conv_b.reshape(1, 3 * HY_W))


RGH = 128
SEGS = ((S, C, C // 8), (0, S, S // 8))
POFF = {S: 0, 0: C + 24}


def _rg_kernel(x_ref, gt_ref, cw_ref, cb_ref, wbd_ref, bias_ref, sp_ref, o_ref,
               xp_ref, af_ref, bf_ref, ar_ref, br_ref, cf_ref, cr_ref):
    sub8 = lax.broadcasted_iota(jnp.int32, (8, 1), 0)
    cw = cw_ref[...]
    cb = cb_ref[...]
    nl2e8 = -RG_C * math.log2(math.e)
    spf = sp_ref[0:1, :] * nl2e8
    spr = sp_ref[1:2, :] * nl2e8

    for seg0, rows, nslot in SEGS:
        p0 = POFF[seg0]

        for r in range(nslot):
            xp_ref[p0 + 16 + 8 * r:p0 + 24 + 8 * r, :] = x_ref[pl.ds(seg0 + r, 8, stride=nslot), :]
        for slot, src, sh in ((0, nslot - 2, 1), (1, nslot - 1, 1), (nslot + 2, 0, 7)):
            v = pltpu.roll(xp_ref[p0 + 16 + 8 * src:p0 + 24 + 8 * src, :], sh, 0)
            xp_ref[p0 + 8 * slot:p0 + 8 * slot + 8, :] = jnp.where(sub8 == (0 if sh == 1 else 7), 0.0, v)

        for t0 in range(0, rows, 256):
            xc = cb
            for j in range(4):
                xc = xc + xp_ref[p0 + 8 * j + t0:p0 + 8 * j + t0 + 256, :] * cw[j:j + 1, :]
            g = _dot(xc.astype(BF16), wbd_ref[...])
            for d, (a_ref, b_ref, sp) in enumerate(((af_ref, bf_ref, spf), (ar_ref, br_ref, spr))):
                rg = _sigmoid(g[:, (2 * d) * RGH:(2 * d + 1) * RGH] + bias_ref[2 * d:2 * d + 1, :])
                ig = _sigmoid(g[:, (2 * d + 1) * RGH:(2 * d + 2) * RGH] + bias_ref[2 * d + 1:2 * d + 2, :])
                a = jnp.exp2(rg * sp)
                a_ref[seg0 + t0:seg0 + t0 + 256, :] = a
                om = 1.0 - a * a
                root = jnp.where(om > 0.0, om * lax.rsqrt(om), 0.0)
                b_ref[seg0 + t0:seg0 + t0 + 256, :] = root * (ig * xc)

    one = jnp.ones((8, RGH), F32)
    zero = jnp.zeros((8, RGH), F32)

    def step(a_ref, b_ref, t, p, h):
        t = pl.multiple_of(t, 8)
        a = a_ref[pl.ds(t, 8), :]
        h = a * h + b_ref[pl.ds(t, 8), :]
        p = a * p
        a_ref[pl.ds(t, 8), :] = p
        b_ref[pl.ds(t, 8), :] = h
        return p, h

    for seg0, rows, nslot in SEGS:
        half = nslot // 2

        def body(i, carry):
            fa, fb, ra, rb = carry
            fa = step(af_ref, bf_ref, seg0 + 8 * i, *fa)
            fb = step(af_ref, bf_ref, seg0 + 8 * (half + i), *fb)
            ra = step(ar_ref, br_ref, seg0 + 8 * (half - 1 - i), *ra)
            rb = step(ar_ref, br_ref, seg0 + 8 * (nslot - 1 - i), *rb)
            return fa, fb, ra, rb

        lax.fori_loop(0, half, body, ((one, zero),) * 4, unroll=4)

    def row(ref, t):
        return ref[t:t + 1, :]

    cf = jnp.zeros((1, RGH), F32)
    cr = jnp.zeros((1, RGH), F32)
    for k, (seg0, rows, nslot) in enumerate(SEGS):
        mid = seg0 + 8 * (nslot // 2)
        last = seg0 + 8 * (nslot - 1)
        for s in range(8):
            cf_ref[16 * k + s:16 * k + s + 1, :] = cf
            cf = row(bf_ref, mid - 8 + s) + row(af_ref, mid - 8 + s) * cf
            cf_ref[16 * k + 8 + s:16 * k + 9 + s, :] = cf
            cf = row(bf_ref, last + s) + row(af_ref, last + s) * cf
        for s in range(7, -1, -1):
            cr_ref[16 * k + 8 + s:16 * k + 9 + s, :] = cr
            cr = row(br_ref, mid + s) + row(ar_ref, mid + s) * cr
            cr_ref[16 * k + s:16 * k + s + 1, :] = cr
            cr = row(br_ref, seg0 + s) + row(ar_ref, seg0 + s) * cr

    for k, (seg0, rows, nslot) in enumerate(SEGS):
        hrows = rows // 2
        for t0 in range(0, rows, 256):
            sel = [16 * k + (0 if t0 + 8 * j < hrows else 8) for j in range(32)]
            cft = jnp.concatenate([cf_ref[o:o + 8, :] for o in sel], axis=0)
            crt = jnp.concatenate([cr_ref[o:o + 8, :] for o in sel], axis=0)
            sl = slice(seg0 + t0, seg0 + t0 + 256)
            bf_ref[sl, :] = (bf_ref[sl, :] + af_ref[sl, :] * cft) + (br_ref[sl, :] + ar_ref[sl, :] * crt)

    for seg0, rows, nslot in SEGS:
        per = nslot // 8
        for q in range(rows // 8):
            s, r0 = divmod(q, per)
            af_ref[seg0 + 8 * q:seg0 + 8 * q + 8, :] = bf_ref[pl.ds(seg0 + 64 * r0 + s, 8, stride=8), :]

    c0 = math.sqrt(2.0 / math.pi)
    for r0 in range(0, T, 256):
        gt = gt_ref[r0:r0 + 256, :].astype(F32)
        gelu = 0.5 * gt * (1.0 + jnp.tanh(c0 * (gt + 0.044715 * (gt * gt * gt))))
        o_ref[r0:r0 + 256, :] = (af_ref[r0:r0 + 256, :] * gelu).astype(o_ref.dtype)


def rglru(z, conv_w, conv_b, wbd, bias4, softplus_neg_lam):
    nh = RG_W // RGH
    return pl.pallas_call(
        _rg_kernel,
        grid=(B, nh),
        in_specs=[pl.BlockSpec((None, T, RGH), lambda b, j: (b, 0, j)),
                  pl.BlockSpec((None, T, RGH), lambda b, j: (b, 0, nh + j)),
                  pl.BlockSpec((4, RGH), lambda b, j: (0, j)),
                  pl.BlockSpec((1, RGH), lambda b, j: (0, j)),
                  pl.BlockSpec((None, RGH, 4 * RGH), lambda b, j: (j, 0, 0)),
                  pl.BlockSpec((4, RGH), lambda b, j: (0, j)),
                  pl.BlockSpec((2, RGH), lambda b, j: (0, j))],
        out_specs=pl.BlockSpec((None, T, RGH), lambda b, j: (b, 0, j)),
        out_shape=jax.ShapeDtypeStruct((B, T, RG_W), BF16),
        scratch_shapes=[pltpu.VMEM((T + 48, RGH), F32)] + [pltpu.VMEM((T, RGH), F32)] * 4
                       + [pltpu.VMEM((32, RGH), F32)] * 2,
        compiler_params=_cp(("parallel", "parallel")),
        name="rglru",
    )(z, z, conv_w, conv_b.reshape(1, RG_W), wbd, bias4, softplus_neg_lam)


def _softplus_kernel(x_ref, o_ref):
    y = -x_ref[...]
    o_ref[...] = jnp.maximum(y, 0.0) + jnp.log(1.0 + jnp.exp(-jnp.abs(y)))


def softplus_neg(lam):
    return pl.pallas_call(
        _softplus_kernel,
        out_shape=jax.ShapeDtypeStruct(lam.shape, F32),
        name="softplus_neg",
    )(lam)


def _hfilt_kernel(z_ref, w1_ref, b1_ref, fr_ref, w2_ref, b2_ref, w3_ref, dec_ref, x1_ref, x2_ref, nyq_ref,
                  s1_ref, s2_ref, *, n, tr):
    fr = fr_ref[...]
    w1h, w1l = _split(w1_ref[...])
    w2h, w2l = _split(w2_ref[...])
    w3h, w3l = _split(w3_ref[...])
    tot = jnp.zeros((1, HY_W), F32)
    alt = jnp.zeros((1, HY_W), F32)
    sign = jnp.where(lax.broadcasted_iota(jnp.int32, (tr, 1), 0) % 2 == 0, 1.0, -1.0)
    for r0 in range(0, n, tr):
        zh, zl = _split(z_ref[r0:r0 + tr, :])
        h = jnp.sin(fr * (_dot3(zh, zl, w1h, w1l) + b1_ref[...]))
        hh, hl = _split(h)
        h = jnp.sin(fr * (_dot3(hh, hl, w2h, w2l) + b2_ref[...]))
        hh, hl = _split(h)
        h = _dot3(hh, hl, w3h, w3l)
        dec = dec_ref[r0:r0 + tr, :]
        h0 = h[:, :HY_W] * dec
        h1 = h[:, HY_W:] * dec
        if r0 == 0:
            h1 = jnp.where(lax.broadcasted_iota(jnp.int32, (tr, 1), 0) == 0, 0.0, h1)
        tot = tot + jnp.sum(jnp.abs(h0) + jnp.abs(h1), axis=0, keepdims=True)
        alt = alt + jnp.sum((h0 + h1) * sign, axis=0, keepdims=True)
        s1_ref[r0:r0 + tr, :] = h0 + h1
        s2_ref[r0:r0 + tr, :] = h0 - h1
    inv = 1.0 / tot
    for r0 in range(0, n, tr):
        x1_ref[r0:r0 + tr, :] = (s1_ref[r0:r0 + tr, :] * inv).astype(BF16)
        x2_ref[r0:r0 + tr, :] = (s2_ref[r0:r0 + tr, :] * inv).astype(BF16)
    nyq_ref[...] = jnp.broadcast_to(alt * inv, (8, HY_W))


def hyena_filter_halves(n, zemb, w1p, b1, freq, w2, b2, w3, decay):
    kern = functools.partial(_hfilt_kernel, n=n, tr=256)
    full = lambda shape: pl.BlockSpec(shape, lambda o: (0,) * len(shape))
    return pl.pallas_call(
        kern,
        grid=(2,),
        in_specs=[full((n, 128)), full((128, HY_HID)), full((1, HY_HID)), full((1, HY_HID)),
                  full((HY_HID, HY_HID)), full((1, HY_HID)),
                  pl.BlockSpec((HY_HID, 2 * HY_W), lambda o: (0, o)),
                  full((n, HY_W))],
        out_specs=[pl.BlockSpec((n, HY_W), lambda o: (0, o)),
                   pl.BlockSpec((n, HY_W), lambda o: (0, o)),
                   pl.BlockSpec((8, HY_W), lambda o: (0, o))],
        out_shape=[jax.ShapeDtypeStruct((n, 2 * HY_W), BF16),
                   jax.ShapeDtypeStruct((n, 2 * HY_W), BF16),
                   jax.ShapeDtypeStruct((8, 2 * HY_W), F32)],
        scratch_shapes=[pltpu.VMEM((n, HY_W), F32), pltpu.VMEM((n, HY_W), F32)],
        compiler_params=_cp(("parallel",)),
        name=f"hyena_filter_{n}",
    )(zemb, w1p, b1.reshape(1, HY_HID), freq.reshape(1, HY_HID), w2, b2.reshape(1, HY_HID), w3, decay)


def _kspec_kernel(at_ref, ab_ref, x1_ref, x2_ref, kt_ref, kb_ref):
    kt_ref[...] = _dot(at_ref[...], x1_ref[...])
    kb_ref[...] = _dot(ab_ref[...], x2_ref[...])


def filter_spectrum(fwd, x1, x2, n, tr):
    nt = n // tr
    return pl.pallas_call(
        _kspec_kernel,
        grid=(nt,),
        in_specs=[pl.BlockSpec((tr, n), lambda i: (i, 0)),
                  pl.BlockSpec((tr, n), lambda i: (nt + i, 0)),
                  pl.BlockSpec((n, 2 * HY_W), lambda i: (0, 0)),
                  pl.BlockSpec((n, 2 * HY_W), lambda i: (0, 0))],
        out_specs=[pl.BlockSpec((tr, 2 * HY_W), lambda i: (i, 0)),
                   pl.BlockSpec((tr, 2 * HY_W), lambda i: (i, 0))],
        out_shape=[jax.ShapeDtypeStruct((n, 2 * HY_W), F32)] * 2,
        compiler_params=_cp(("parallel",)),
        name=f"filter_spectrum_{n}",
    )(fwd, fwd, x1, x2)


def _hfwd_kernel(f_ref, x_ref, kt_ref, kb_ref, kn_ref, yr_ref, yi_ref, xb_ref, *, n, tr):
    i = pl.program_id(1)

    @pl.when(i == 0)
    def _():
        xb_ref[...] = x_ref[...].astype(BF16)

    ts = min(tr, 512)
    hw = HY_W // 2
    pieces = []
    for r0 in range(0, tr, ts):
        top = pl.multiple_of(i * tr + r0, ts)
        tab = jnp.concatenate([f_ref[pl.ds(top, ts), :], f_ref[pl.ds(n + top, ts), :]], axis=0)
        for c0 in (0, hw):
            pieces.append((r0, c0, _dot(tab, xb_ref[:, c0:c0 + hw])))
    first = (lax.broadcasted_iota(jnp.int32, (ts, 1), 0) == 0) & (i == 0)
    for r0, c0, u in pieces:
        ur = u[:ts]
        ui = u[ts:]
        kr = kt_ref[r0:r0 + ts, c0:c0 + hw]
        ki = kb_ref[r0:r0 + ts, c0:c0 + hw]
        if r0 == 0:
            yr = ur * kr - jnp.where(first, 0.0, ui * ki)
            yi = jnp.where(first, ui * kn_ref[0:1, c0:c0 + hw], ur * ki + ui * kr)
        else:
            yr = ur * kr - ui * ki
            yi = ur * ki + ui * kr
        yr_ref[r0:r0 + ts, c0:c0 + hw] = yr.astype(BF16)
        yi_ref[r0:r0 + ts, c0:c0 + hw] = yi.astype(BF16)


def hyena_forward(fwd, x, x_rb, kt, kb, nyq, order, n, tr, name):
    nt = n // tr
    kern = functools.partial(_hfwd_kernel, n=n, tr=tr)
    return pl.pallas_call(
        kern,
        grid=(B, nt),
        in_specs=[pl.BlockSpec((2 * n, n), lambda b, i: (0, 0), pipeline_mode=pl.Buffered(1)),
                  pl.BlockSpec((None, n, HY_W), lambda b, i: (b, x_rb, 0)),
                  pl.BlockSpec((tr, HY_W), lambda b, i: (i, order)),
                  pl.BlockSpec((tr, HY_W), lambda b, i: (i, order)),
                  pl.BlockSpec((8, HY_W), lambda b, i: (0, order))],
        out_specs=[pl.BlockSpec((None, tr, HY_W), lambda b, i: (b, i, 0)),
                   pl.BlockSpec((None, tr, HY_W), lambda b, i: (b, i, 0))],
        out_shape=[jax.ShapeDtypeStruct((B, n, HY_W), BF16)] * 2,
        scratch_shapes=[pltpu.VMEM((n, HY_W), BF16)],
        compiler_params=_cp(("parallel", "arbitrary")),
        name=name,
    )(fwd, x, kt, kb, nyq)


def _hinv_kernel(inv_ref, yr_ref, yi_ref, up_ref, gt_ref, bias_ref, o_ref, *, n, tmi):
    rows = pl.ds(pl.multiple_of(pl.program_id(1) * tmi, tmi), tmi)
    y = _dot(inv_ref[rows, 0:n], yr_ref[...]) + _dot(inv_ref[rows, n:2 * n], yi_ref[...])
    o_ref[...] = (gt_ref[...] * (y + up_ref[...].astype(F32) * bias_ref[...])).astype(o_ref.dtype)


def hyena_inverse(inv, yr, yi, uprev, uprev_rb, gate, gate_rb, gate_cb, bias, n, tmi, out_dtype, name):
    return pl.pallas_call(
        functools.partial(_hinv_kernel, n=n, tmi=tmi),
        grid=(B, n // tmi),
        in_specs=[pl.BlockSpec((n, 2 * n), lambda b, i: (0, 0), pipeline_mode=pl.Buffered(1)),
                  pl.BlockSpec((None, n, HY_W), lambda b, i: (b, 0, 0)),
                  pl.BlockSpec((None, n, HY_W), lambda b, i: (b, 0, 0)),
                  pl.BlockSpec((None, tmi, HY_W), lambda b, i: (b, uprev_rb + i, 0)),
                  pl.BlockSpec((None, tmi, HY_W), lambda b, i: (b, gate_rb + i, gate_cb)),
                  pl.BlockSpec((1, HY_W), lambda b, i: (0, 0))],
        out_specs=pl.BlockSpec((None, tmi, HY_W), lambda b, i: (b, i, 0)),
        out_shape=jax.ShapeDtypeStruct((B, n, HY_W), out_dtype),
        compiler_params=_cp(("parallel", "parallel")),
        name=name,
    )(inv, yr, yi, uprev, gate, bias.reshape(1, HY_W))


def _rope128(r, c, sa, sb):
    return r * c + pltpu.roll(r, 32, 1) * sa + pltpu.roll(r, 96, 1) * sb


def _qkv_kernel(x_ref, g_ref, m_ref, win_ref, qg_ref, kvg_ref, wq_ref, wk_ref, wv_ref, c_ref, sa_ref, sb_ref,
                q_ref, k_ref, v_ref, *, tm):
    x = x_ref[...]
    y = x * lax.rsqrt(jnp.mean(x * x, axis=-1, keepdims=True) + EPS) * g_ref[...]
    row = pl.program_id(1) * tm + lax.broadcasted_iota(jnp.int32, (tm, 1), 0)
    h = (y * (1.0 + _mod_rows(m_ref, 1, row, S)) + _mod_rows(m_ref, 0, row, S)).astype(BF16)
    z = _dot(h, win_ref[...])
    c = c_ref[...]
    sa = sa_ref[...]
    sb = sb_ref[...]
    zq = z[:, :Q_LORA]
    qn = (zq * lax.rsqrt(jnp.mean(zq * zq, axis=-1, keepdims=True) + EPS) * qg_ref[...]).astype(BF16)
    zkv = z[:, Q_LORA:Q_LORA + KV_LORA]
    ckv = (zkv * lax.rsqrt(jnp.mean(zkv * zkv, axis=-1, keepdims=True) + EPS) * kvg_ref[...]).astype(BF16)
    kr = _rope128(z[:, Q_LORA + KV_LORA:], c, sa, sb).astype(BF16)
    for h in range(HEADS):
        q = _dot(qn, wq_ref[:, h * HQ:(h + 1) * HQ])
        q_ref[:, h * HQ:h * HQ + NOPE] = q[:, :NOPE].astype(BF16)
        q_ref[:, h * HQ + NOPE:(h + 1) * HQ] = _rope128(q[:, NOPE:], c, sa, sb).astype(BF16)
        k_ref[:, h * HQ:h * HQ + NOPE] = _dot(ckv, wk_ref[:, h * NOPE:(h + 1) * NOPE]).astype(BF16)
        k_ref[:, h * HQ + NOPE:(h + 1) * HQ] = kr
    v_ref[...] = _dot(ckv, wv_ref[...]).astype(BF16)


def mla_qkv(xs, g, mod, w_in, q_g, kv_g, wq, wk, wv, ctab, satab, sbtab, tm):
    full = lambda shape: pl.BlockSpec(shape, lambda b, i: (0,) * len(shape))
    tab = pl.BlockSpec((tm, 128), lambda b, i: (i, 0))
    return pl.pallas_call(
        functools.partial(_qkv_kernel, tm=tm),
        grid=(B, T // tm),
        in_specs=[pl.BlockSpec((None, tm, D), lambda b, i: (b, i, 0)),
                  full((1, D)),
                  pl.BlockSpec((None, 2, 6, D), lambda b, i: (b, 0, 0, 0)),
                  full((D, ODD_IN_P)),
                  full((1, Q_LORA)), full((1, KV_LORA)),
                  full((Q_LORA, HEADS * HQ)), full((KV_LORA, HEADS * NOPE)), full((KV_LORA, HEADS * VH)),
                  tab, tab, tab],
        out_specs=[pl.BlockSpec((None, tm, HEADS * HQ), lambda b, i: (b, i, 0)),
                   pl.BlockSpec((None, tm, HEADS * HQ), lambda b, i: (b, i, 0)),
                   pl.BlockSpec((None, tm, HEADS * VH), lambda b, i: (b, i, 0))],
        out_shape=[jax.ShapeDtypeStruct((B, T, HEADS * HQ), BF16),
                   jax.ShapeDtypeStruct((B, T, HEADS * HQ), BF16),
                   jax.ShapeDtypeStruct((B, T, HEADS * VH), BF16)],
        compiler_params=_cp(("parallel", "parallel")),
        name="mla_qkv",
    )(xs, g.reshape(1, D), mod, w_in, q_g.reshape(1, Q_LORA), kv_g.reshape(1, KV_LORA), wq, wk, wv,
      ctab, satab, sbtab)


def _attn_kernel(q_ref, k_ref, v_ref, o_ref, *, scale, ts):
    c = scale * math.log2(math.e)
    kc = 256
    tq = q_ref.shape[0]
    nh = q_ref.shape[1] // HQ
    tiles = [(h, r0) for h in range(nh) for r0 in range(0, tq, ts)]

    def scores(h, r0):
        return lax.dot_general(q_ref[r0:r0 + ts, h * HQ:(h + 1) * HQ], k_ref[:, h * HQ:(h + 1) * HQ],
                               (((1,), (1,)), ((), ())), preferred_element_type=F32)

    ahead = scores(*tiles[0])
    for n, (h, r0) in enumerate(tiles):
        s = ahead
        if n + 1 < len(tiles):
            ahead = scores(*tiles[n + 1])
        m = jnp.max(s, axis=-1, keepdims=True)
        acc = None
        lv = None
        for c0 in range(0, T, kc):
            p = jnp.exp2((s[:, c0:c0 + kc] - m) * c)
            pl_ = p[:, :128]
            for l0 in range(128, kc, 128):
                pl_ = pl_ + p[:, l0:l0 + 128]
            lv = pl_ if lv is None else lv + pl_
            d = _dot(p.astype(BF16), v_ref[c0:c0 + kc, h * VH:(h + 1) * VH])
            acc = d if acc is None else acc + d
        l = jnp.sum(lv, axis=-1, keepdims=True)
        o_ref[r0:r0 + ts, h * VH:(h + 1) * VH] = (acc * (1.0 / l)).astype(o_ref.dtype)


def mla_attention(q, k, v, tq, hp):
    kern = functools.partial(_attn_kernel, scale=(NOPE + ROPE) ** -0.5, ts=512)
    return pl.pallas_call(
        kern,
        grid=(B, HEADS // hp, S // tq),
        in_specs=[pl.BlockSpec((None, tq, hp * HQ), lambda b, h, i: (b, i, h)),
                  pl.BlockSpec((None, T, hp * HQ), lambda b, h, i: (b, 0, h)),
                  pl.BlockSpec((None, T, hp * VH), lambda b, h, i: (b, 0, h))],
        out_specs=pl.BlockSpec((None, tq, hp * VH), lambda b, h, i: (b, i, h)),
        out_shape=jax.ShapeDtypeStruct((B, S, HEADS * VH), BF16),
        compiler_params=_cp(("parallel", "parallel", "parallel")),
        name="mla_attention",
    )(q, k, v)


@functools.lru_cache(maxsize=None)
def _dft_tables(n):
    m = 2 * n
    j = np.arange(n, dtype=np.int64)
    kk = np.arange(n, dtype=np.int64)
    ang = 2.0 * np.pi * ((kk[:, None] * j[None, :]) % m).astype(np.float64) / m
    cosm = np.cos(ang)
    sinm = np.sin(ang)
    fwd = np.concatenate([cosm, -sinm], axis=0)
    fwd[n, :] = np.where(j % 2 == 0, 1.0, -1.0)
    inv = np.concatenate([2.0 * cosm.T, -2.0 * sinm.T], axis=1) / m
    inv[:, 0] = 1.0 / m
    inv[:, n] = np.where(j % 2 == 0, 1.0, -1.0) / m
    return fwd.astype(np.float32).astype(BF16), inv.astype(np.float32).astype(BF16)


def _filter_consts(n):
    pos = jnp.arange(n, dtype=F32)
    t = jnp.linspace(0.0, 1.0, n, dtype=F32)[:, None]
    bands = (HY_EMB - 1) // 2
    w = 2.0 * math.pi * pos / n
    f = jnp.linspace(1e-4, bands - 1, bands, dtype=F32)
    ang = w[:, None] * f[None, :]
    z = jnp.concatenate([t, jnp.cos(ang), -jnp.sin(ang)], axis=-1)
    z = jnp.pad(z, ((0, 0), (0, 128 - HY_EMB)))
    max_decay = math.log(1e-2) / 0.3
    min_decay = math.log(1e-2) / 1.5
    deltas = jnp.linspace(min_decay, max_decay, HY_W, dtype=F32)
    decay = jnp.exp(-t * jnp.abs(deltas)[None, :])
    return z, decay


def _rope_tables():
    rows = S // 64
    row = jnp.repeat(jnp.arange(rows, dtype=F32), 64)
    col = jnp.tile(jnp.arange(64, dtype=F32), rows)
    n_freq = ROPE // 4
    inv = 10000.0 ** (-jnp.arange(n_freq, dtype=F32) / n_freq)
    ang = jnp.concatenate([row[:, None] * inv[None, :], col[:, None] * inv[None, :]], axis=-1)
    cos = jnp.concatenate([jnp.cos(ang), jnp.ones((C, 32), F32)], axis=0)
    sin = jnp.concatenate([jnp.sin(ang), jnp.zeros((C, 32), F32)], axis=0)
    z32 = jnp.zeros((T, 32), F32)
    z64 = jnp.zeros((T, 64), F32)
    ctab = jnp.concatenate([cos, cos, z64], axis=-1)
    satab = jnp.concatenate([z32, sin, z64], axis=-1)
    sbtab = jnp.concatenate([-sin, z32, z64], axis=-1)
    return ctab, satab, sbtab


def _even_layer(x, ctx, mod, norm1_g, w_in, rg_conv_w, rg_conv_b, rg_wa, rg_ba, rg_wx, rg_bx, rg_lambda,
                hy_conv_w, hy_conv_b, f_w1, f_b1, f_freq, f_w2, f_b2, f_w3, hy_bias, w_out):
    z, xs = even_in_proj(x, ctx, norm1_g, mod, w_in.astype(BF16))

    def blockdiag(w):
        per = RGH // RG_HD
        out = jnp.zeros((RG_W // RGH, RGH, RGH), F32)
        for h in range(RG_HEADS):
            g, l = divmod(h, per)
            out = out.at[g, l * RG_HD:(l + 1) * RG_HD, l * RG_HD:(l + 1) * RG_HD].set(w[h])
        return out

    wbd = jnp.concatenate([blockdiag(rg_wa[0]), blockdiag(rg_wx[0]), blockdiag(rg_wa[1]), blockdiag(rg_wx[1])],
                          axis=-1).astype(BF16)
    bias4 = jnp.stack([rg_ba[0], rg_bx[0], rg_ba[1], rg_bx[1]], axis=0)
    a = rglru(z, rg_conv_w, rg_conv_b, wbd, bias4, softplus_neg(rg_lambda))

    vc = hyena_short_conv(z, hy_conv_w, hy_conv_b)
    w1p = jnp.pad(f_w1, ((0, 128 - HY_EMB), (0, 0)))
    outs = []
    for n, rb in ((S, 0), (C, S // C)):
        fwd, inv = _dft_tables(n)
        zemb, decay = _filter_consts(n)
        x1, x2, nyq = hyena_filter_halves(n, zemb, w1p, f_b1, f_freq, f_w2, f_b2, f_w3, decay)
        tr = min(n, 512)
        tmi = min(n, 1024)
        kt, kb = filter_spectrum(fwd, x1, x2, n, tr)
        grb = rb * (n // tmi)
        yr, yi = hyena_forward(fwd, vc, rb, kt, kb, nyq, 0, n, tmi, f"hyena_fwd1_{n}")
        u1 = hyena_inverse(inv, yr, yi, vc, grb, vc, grb, 1, hy_bias[0], n, tmi, F32, f"hyena_inv1_{n}")
        yr, yi = hyena_forward(fwd, u1, 0, kt, kb, nyq, 1, n, tmi, f"hyena_fwd2_{n}")
        outs.append(hyena_inverse(inv, yr, yi, u1, 0, vc, grb, 2, hy_bias[1], n, tmi, BF16,
                                  f"hyena_inv2_{n}"))
    return ([a, jnp.concatenate(outs, axis=1)], w_out.astype(BF16)), xs


def _odd_layer(xs, mod, norm1_g, w_in, q_g, kv_g, w_uq, w_ukv, w_o):
    perm = np.concatenate([np.arange(0, ROPE, 2), np.arange(1, ROPE, 2)])
    w_rope = jnp.pad(w_in[:, Q_LORA + KV_LORA:][:, perm], ((0, 0), (0, 128 - ROPE)))
    w_in_p = jnp.concatenate([w_in[:, :Q_LORA + KV_LORA], w_rope], axis=-1).astype(BF16)
    wq = w_uq.reshape(Q_LORA, HEADS, NOPE + ROPE)
    wq = jnp.concatenate([wq[..., :NOPE], wq[..., NOPE:][..., perm],
                          jnp.zeros((Q_LORA, HEADS, HQ - NOPE - ROPE), F32)], axis=-1)
    wq = wq.reshape(Q_LORA, HEADS * HQ).astype(BF16)
    wkv = w_ukv.reshape(KV_LORA, HEADS, NOPE + VH)
    wk = wkv[..., :NOPE].reshape(KV_LORA, HEADS * NOPE).astype(BF16)
    wv = wkv[..., NOPE:].reshape(KV_LORA, HEADS * VH).astype(BF16)
    ctab, satab, sbtab = _rope_tables()
    q, k, v = mla_qkv(xs, norm1_g, mod, w_in_p, q_g, kv_g, wq, wk, wv, ctab, satab, sbtab, 768)
    return [mla_attention(q, k, v, 2048, 2)], w_o.astype(BF16)


def _ffn(xs, t, mix, mod, layer, norm2_g, w_up, conv_w, conv_b, w_down, final_g, tm, nin, sb, seg_starts,
         ctx_start, final):
    parts, w_out = mix
    return mixer_out_conv_ffn(xs, t, parts, w_out, norm2_g, mod, layer, w_up, conv_w, conv_b, w_down, final_g,
                              tm, nin, sb, seg_starts, ctx_start, final, f"mix_out_conv_ffn_{layer}")


def kernel(x, c, ctx, c_ctx, ada_w, ada_b, norm1_g, norm2_g, ev_w_in, ev_rg_conv_w, ev_rg_conv_b, ev_rg_wa, ev_rg_ba, ev_rg_wx, ev_rg_bx, ev_rg_lambda, ev_hy_conv_w, ev_hy_conv_b, ev_hy_f_w1, ev_hy_f_b1, ev_hy_f_freq, ev_hy_f_w2, ev_hy_f_b2, ev_hy_f_w3, ev_hy_bias, ev_w_out, od_w_in, od_q_norm_g, od_kv_norm_g, od_w_uq, od_w_ukv, od_w_o, ffn_w_up, ffn_conv_w, ffn_conv_b, ffn_w_down, final_g):
    cvec = jnp.concatenate([c, c_ctx[None], jnp.zeros((16 - B - 1, D), F32)], axis=0)
    mod = modulation(cvec, ada_w, ada_b)

    def mod_sel(layer):
        lat = mod[layer, :B].reshape(B, 1, 6, D)
        cx = jnp.broadcast_to(mod[layer, B].reshape(1, 1, 6, D), (B, 1, 6, D))
        return jnp.concatenate([lat, cx], axis=1)

    m0 = mod_sel(0)
    mix, xs = _even_layer(x, ctx, m0, norm1_g[0], ev_w_in[0], ev_rg_conv_w[0], ev_rg_conv_b[0], ev_rg_wa[0], ev_rg_ba[0],
                      ev_rg_wx[0], ev_rg_bx[0], ev_rg_lambda[0], ev_hy_conv_w[0], ev_hy_conv_b[0], ev_hy_f_w1[0],
                      ev_hy_f_b1[0], ev_hy_f_freq[0], ev_hy_f_w2[0], ev_hy_f_b2[0], ev_hy_f_w3[0], ev_hy_bias[0],
                      ev_w_out[0])
    w_up = ffn_w_up.astype(BF16)
    w_down = ffn_w_down.astype(BF16)
    xs = _ffn(xs, T, mix, m0, 0, norm2_g[0], w_up, ffn_conv_w[0], ffn_conv_b[0], w_down, final_g,
              384, 2, 128, (0, S), S, False)
    m1 = mod_sel(1)
    mix = _odd_layer(xs, m1, norm1_g[1], od_w_in[0], od_q_norm_g[0], od_kv_norm_g[0], od_w_uq[0], od_w_ukv[0],
                     od_w_o[0])
    return _ffn(xs, S, mix, m1, 1, norm2_g[1], w_up, ffn_conv_w[1], ffn_conv_b[1], w_down, final_g,
                256, 2, 128, (0,), None, True)
```

```python
import functools
import math

import numpy as np
import jax
import jax.numpy as jnp
from jax import lax
from jax.experimental import pallas as pl
from jax.experimental.pallas import tpu as pltpu

F32 = jnp.float32
BF16 = jnp.bfloat16

D = 1024
B = 8
S = 2048
C = 256
T = S + C
EPS = 1e-6

RG_W = 512
RG_HEADS = 8
RG_HD = 64
RG_C = 8.0
HY_W = 512
HY_EMB = 33
HY_HID = 64
EVEN_IN = 2560

HEADS = 8
Q_LORA = 512
KV_LORA = 256
NOPE = 128
ROPE = 64
VH = 128
HQ = 256
ODD_IN_P = 896
D_FF = 2816

VMEM_LIMIT = 52 * 1024 * 1024


def _cp(sem):
    return pltpu.CompilerParams(dimension_semantics=sem, vmem_limit_bytes=VMEM_LIMIT)


def _dot(a, b):
    return jnp.dot(a, b, preferred_element_type=F32)


def _split(x):
    hi = x.astype(BF16)
    lo = (x - hi.astype(F32)).astype(BF16)
    return hi, lo


def _dot3(ah, al, bh, bl):
    return _dot(ah, bh) + (_dot(ah, bl) + _dot(al, bh))


def _sigmoid(x):
    return 1.0 / (1.0 + jnp.exp2(x * (-math.log2(math.e))))


def _mod_rows(m_ref, k, row, ctx_start):
    lat = m_ref[0, k:k + 1, :]
    if ctx_start is None:
        return lat
    return jnp.where(row >= ctx_start, m_ref[1, k:k + 1, :], lat)


def _strided_rows(ref3, start, stride, cb0=0):
    return jnp.concatenate([ref3[cb0 + cb, pl.ds(start, 8, stride=stride), :] for cb in range(D // 128)],
                           axis=-1)


def _perm_rows(ref3, base, nrows, r, cb0=0):
    return _strided_rows(ref3, base + r, nrows // 8, cb0)


def _unperm_rows(ref3, base, nrows, q):
    s, r0 = divmod(8 * q, nrows // 8)
    return _strided_rows(ref3, base + 8 * r0 + s, 8)


def _mod_kernel(c_ref, w_ref, b_ref, o_ref):
    c = c_ref[...]
    s = (c * _sigmoid(c)).astype(BF16)
    o_ref[...] = _dot(s, w_ref[...].astype(BF16)) + b_ref[...]


def modulation(cvec, ada_w, ada_b):
    depth = ada_w.shape[0]
    tn = 1024
    return pl.pallas_call(
        _mod_kernel,
        grid=(depth, 6 * D // tn),
        in_specs=[pl.BlockSpec((16, D), lambda l, j: (0, 0)),
                  pl.BlockSpec((None, D, tn), lambda l, j: (l, 0, j)),
                  pl.BlockSpec((None, 1, tn), lambda l, j: (l, 0, j))],
        out_specs=pl.BlockSpec((None, 16, tn), lambda l, j: (l, 0, j)),
        out_shape=jax.ShapeDtypeStruct((depth, 16, 6 * D), F32),
        compiler_params=_cp(("parallel", "parallel")),
        name="modulation",
    )(cvec, ada_w, ada_b.reshape(depth, 1, 6 * D))


def _even_in_kernel(xa_ref, xb_ref, xc_ref, ctx_ref, g_ref, m_ref, w_ref, z_ref, xs_ref, h_ref, *, tm, nchunk):
    i = pl.program_id(1)
    third = jnp.where(i == T // tm - 1, ctx_ref[...], xc_ref[...])
    x = jnp.concatenate([xa_ref[...], xb_ref[...], third], axis=0)
    xs_ref[...] = x
    ms = jnp.mean(x * x, axis=-1, keepdims=True)
    y = x * lax.rsqrt(ms + EPS) * g_ref[...]
    row = i * tm + lax.broadcasted_iota(jnp.int32, (tm, 1), 0)
    h_ref[...] = (y * (1.0 + _mod_rows(m_ref, 1, row, S)) + _mod_rows(m_ref, 0, row, S)).astype(BF16)
    n = z_ref.shape[-1]
    for n0 in range(0, n, nchunk):
        z_ref[:, n0:n0 + nchunk] = _dot(h_ref[...], w_ref[:, n0:n0 + nchunk])


def even_in_proj(x, ctx, g, mod, w):
    tm = 3 * C
    n = w.shape[1]
    last = S // C - 1
    xblk = lambda k: pl.BlockSpec((None, C, D), lambda b, i: (b, jnp.minimum(3 * i + k, last), 0))
    return pl.pallas_call(
        functools.partial(_even_in_kernel, tm=tm, nchunk=512),
        grid=(B, T // tm),
        in_specs=[xblk(0), xblk(1), xblk(2),
                  pl.BlockSpec((None, C, D), lambda b, i: (b, 0, 0)),
                  pl.BlockSpec((1, D), lambda b, i: (0, 0)),
                  pl.BlockSpec((None, 2, 6, D), lambda b, i: (b, 0, 0, 0)),
                  pl.BlockSpec((D, n), lambda b, i: (0, 0))],
        out_specs=[pl.BlockSpec((None, tm, n), lambda b, i: (b, i, 0)),
                   pl.BlockSpec((None, tm, D), lambda b, i: (b, i, 0))],
        out_shape=[jax.ShapeDtypeStruct((B, T, n), F32), jax.ShapeDtypeStruct((B, T, D), F32)],
        scratch_shapes=[pltpu.VMEM((tm, D), BF16)],
        compiler_params=_cp(("parallel", "parallel")),
        name="even_in_proj",
    )(x, x, x, ctx, g.reshape(1, D), mod, w)


def _ffn_kernel(*refs, nparts, tm, nin, t, sb, seg_starts, ctx_start, final, ck):
    x_ref, xp_ref, xn_ref = refs[:3]
    parts = [refs[3 + 3 * p:6 + 3 * p] for p in range(nparts)]
    (wo_ref, g_ref, m_ref, wu_ref, cw_ref, cb_ref, wd_ref, fg_ref, o_ref,
     h_ref, x3_ref, acc_ref) = refs[3 + 3 * nparts:]
    i = pl.program_id(1)
    nsub = tm // sb
    ncb = D // 128
    sub8 = lax.broadcasted_iota(jnp.int32, (8, 1), 0)
    g = g_ref[...]

    def normmod(x, row):
        ms = jnp.mean(x * x, axis=-1, keepdims=True)
        y = x * lax.rsqrt(ms + EPS) * g
        return y * (1.0 + _mod_rows(m_ref, 4, row, ctx_start)) + _mod_rows(m_ref, 3, row, ctx_start)

    def prepare(k):
        base = k * tm
        row0 = i * nin * tm + base
        mix = None
        k0 = 0
        for p_ref, pp_ref, pn_ref in parts:
            kp = p_ref.shape[-1]
            before = pp_ref[...] if k == 0 else p_ref[base - 16:base, :]
            after = pn_ref[...] if k == nin - 1 else p_ref[base + tm:base + tm + 16, :]
            d = _dot(jnp.concatenate([p_ref[base:base + tm, :], before, after], axis=0), wo_ref[k0:k0 + kp, :])
            mix = d if mix is None else mix + d
            k0 += kp
        xp = xp_ref[...] if k == 0 else x_ref[base - 8:base, :]
        xn = xn_ref[...] if k == nin - 1 else x_ref[base + tm:base + tm + 8, :]
        row = row0 + lax.broadcasted_iota(jnp.int32, (tm, 1), 0)
        x1 = x_ref[base:base + tm, :] + _mod_rows(m_ref, 2, row, ctx_start) * mix[:tm]
        x1p = xp + _mod_rows(m_ref, 2, row0 - 1, ctx_start) * mix[tm + 8:tm + 16]
        x1n = xn + _mod_rows(m_ref, 2, row0 + tm, ctx_start) * mix[tm + 16:tm + 24]
        for cb in range(ncb):
            x3_ref[k * ncb + cb] = x1[:, cb * 128:(cb + 1) * 128]
        for j in range(nsub):
            for r in range(0, sb // 8, 2):
                x = jnp.concatenate([_perm_rows(x3_ref, j * sb, sb, r, k * ncb),
                                     _perm_rows(x3_ref, j * sb, sb, r + 1, k * ncb)], axis=0)
                h_ref[k, j * sb + 8 * r:j * sb + 8 * r + 16, :] = normmod(x, row0 + j * sb).astype(BF16)
        h_ref[k, tm:tm + 16, :] = jnp.concatenate([normmod(x1p, row0 - 1), normmod(x1n, row0 + tm)],
                                                  axis=0).astype(BF16)

    def finish(k):
        base = k * tm
        row0 = i * nin * tm + base
        keep_prev = []
        keep_next = []
        for j in range(nsub):
            r0 = row0 + j * sb
            kp = r0 != seg_starts[0]
            kn = r0 + sb != t
            for s0 in seg_starts[1:]:
                kp = kp & (r0 != s0)
                kn = kn & (r0 + sb != s0)
            keep_prev.append(jnp.where(kp, 1.0, 0.0))
            keep_next.append(jnp.where(kn, 1.0, 0.0))

        def conv(u, j, c0):
            r0 = j * sb
            cur = u[r0:r0 + sb]
            hp = (u[tm + 7:tm + 8] if j == 0 else u[r0 - 1:r0]) * keep_prev[j]
            hn = (u[tm + 8:tm + 9] if j == nsub - 1 else u[r0 + sb:r0 + sb + 1]) * keep_next[j]
            p0 = jnp.where(sub8 == 0, hp, pltpu.roll(cur[sb - 8:sb], 1, 0))
            nl = jnp.where(sub8 == 7, hn, pltpu.roll(cur[0:8], 7, 0))
            prev = jnp.concatenate([p0, cur[:sb - 8]], axis=0)
            nxt = jnp.concatenate([cur[8:], nl], axis=0)
            w = cw_ref[:, c0:c0 + ck]
            return prev * w[0:1] + cur * w[1:2] + nxt * w[2:3] + cb_ref[:, c0:c0 + ck]

        h = h_ref[k]

        def up(c0):
            return _dot(h, wu_ref[:, c0:c0 + ck]), _dot(h, wu_ref[:, D_FF + c0:D_FF + c0 + ck])

        ahead = up(0)
        for n, c0 in enumerate(range(0, D_FF, ck)):
            ug, uv = ahead
            if c0 + ck < D_FF:
                ahead = up(c0 + ck)
            pieces = []
            for j in range(nsub):
                gt = conv(ug, j, c0)
                vl = conv(uv, j, D_FF + c0)
                pieces.append((gt * _sigmoid(gt) * vl).astype(BF16))
            a = pieces[0] if nsub == 1 else jnp.concatenate(pieces, axis=0)
            d = _dot(a, wd_ref[c0:c0 + ck, :])
            for cb in range(ncb):
                if n == 0:
                    acc_ref[cb] = d[:, cb * 128:(cb + 1) * 128]
                else:
                    acc_ref[cb] += d[:, cb * 128:(cb + 1) * 128]
        fg = fg_ref[...]
        for j in range(nsub):
            gate = _mod_rows(m_ref, 5, row0 + j * sb, ctx_start)
            for q in range(sb // 8):
                rows = slice(j * sb + 8 * q, j * sb + 8 * q + 8)
                x1r = jnp.concatenate([x3_ref[k * ncb + cb, rows, :] for cb in range(ncb)], axis=-1)
                out = x1r + gate * _unperm_rows(acc_ref, j * sb, sb, q)
                if final:
                    ms = jnp.mean(out * out, axis=-1, keepdims=True)
                    out = out * lax.rsqrt(ms + EPS) * fg
                o_ref[base + j * sb + 8 * q:base + j * sb + 8 * q + 8, :] = out

    prepare(0)
    for k in range(nin):
        if k + 1 < nin:
            prepare(k + 1)
        finish(k)


def mixer_out_conv_ffn(xs, t, parts, w_out, g, mod, layer, w_up, conv_w, conv_b, w_down, final_g, tm, nin, sb,
                       seg_starts, ctx_start, final, name):
    bsz = xs.shape[0]
    f2 = 2 * D_FF
    bt = nin * tm
    r8 = bt // 8
    r16 = bt // 16
    kern = functools.partial(_ffn_kernel, nparts=len(parts), tm=tm, nin=nin, t=t, sb=sb, seg_starts=seg_starts,
                             ctx_start=ctx_start, final=final, ck=256)
    const = lambda shape: pl.BlockSpec(shape, lambda b, i: (0,) * len(shape), pipeline_mode=pl.Buffered(1))
    layered = lambda shape: pl.BlockSpec((None,) + shape, lambda b, i: (layer, 0, 0),
                                         pipeline_mode=pl.Buffered(1))
    in_specs = [pl.BlockSpec((None, bt, D), lambda b, i: (b, i, 0)),
                pl.BlockSpec((None, 8, D), lambda b, i: (b, jnp.maximum(i * r8 - 1, 0), 0)),
                pl.BlockSpec((None, 8, D), lambda b, i: (b, jnp.minimum((i + 1) * r8, t // 8 - 1), 0))]
    args = [xs, xs, xs]
    for p in parts:
        kp = p.shape[-1]
        in_specs += [pl.BlockSpec((None, bt, kp), lambda b, i: (b, i, 0)),
                     pl.BlockSpec((None, 16, kp), lambda b, i: (b, jnp.maximum(i * r16 - 1, 0), 0)),
                     pl.BlockSpec((None, 16, kp), lambda b, i: (b, jnp.minimum((i + 1) * r16, t // 16 - 1), 0))]
        args += [p, p, p]
    in_specs += [const(w_out.shape), const((1, D)),
                 pl.BlockSpec((None, 2, 6, D), lambda b, i: (b, 0, 0, 0)),
                 layered((D, f2)), const((3, f2)), const((1, f2)), layered((D_FF, D)), const((1, D))]
    args += [w_out, g.reshape(1, D), mod, w_up, conv_w, conv_b.reshape(1, f2), w_down, final_g.reshape(1, D)]
    return pl.pallas_call(
        kern,
        grid=(bsz, t // bt),
        in_specs=in_specs,
        out_specs=pl.BlockSpec((None, bt, D), lambda b, i: (b, i, 0)),
        out_shape=jax.ShapeDtypeStruct((bsz, t, D), F32),
        scratch_shapes=[pltpu.VMEM((nin, tm + 16, D), BF16),
                        pltpu.VMEM((nin * (D // 128), tm, 128), F32), pltpu.VMEM((D // 128, tm, 128), F32)],
        compiler_params=_cp(("parallel", "parallel")),
        name=name,
    )(*args)


PAD = 8


def _fill_padded(xp_ref, x_ref, width):
    z = jnp.zeros((PAD, width), F32)
    xp_ref[0:PAD, :] = z
    xp_ref[PAD:PAD + S, :] = x_ref[0:S, :].astype(F32)
    xp_ref[PAD + S:2 * PAD + S, :] = z
    xp_ref[2 * PAD + S:2 * PAD + T, :] = x_ref[S:T, :].astype(F32)
    xp_ref[2 * PAD + T:3 * PAD + T, :] = z


def _conv_tile(xp_ref, w, bias, r0, rows, ktaps):
    base = PAD + r0 if r0 < S else 2 * PAD + r0
    left = ktaps // 2
    acc = bias
    for j in range(ktaps):
        acc = acc + xp_ref[base + j - left:base + j - left + rows, :] * w[j:j + 1, :]
    return acc


def _hconv_kernel(z_ref, w_ref, b_ref, o_ref, xp_ref):
    _fill_padded(xp_ref, z_ref, 512)
    w = w_ref[...]
    bias = b_ref[...]
    for r0 in range(0, T, 256):
        o_ref[r0:r0 + 256, :] = _conv_tile(xp_ref, w, bias, r0, 256, 3)


def hyena_short_conv(z, conv_w, conv_b):
    return pl.pallas_call(
        _hconv_kernel,
        grid=(B, 3),
        in_specs=[pl.BlockSpec((None, T, 512), lambda b, j: (b, 0, 2 + j)),
                  pl.BlockSpec((3, 512), lambda b, j: (0, j)),
                  pl.BlockSpec((1, 512), lambda b, j: (0, j))],
        out_specs=pl.BlockSpec((None, T, 512), lambda b, j: (b, 0, j)),
        out_shape=jax.ShapeDtypeStruct((B, T, 3 * HY_W), F32),
        scratch_shapes=[pltpu.VMEM((T + 3 * PAD, 512), F32)],
        compiler_params=_cp(("parallel", "parallel")),
        name="hyena_short_conv",
    )(z, conv_w, conv_b.reshape(1, 3 * HY_W))


RGH = 128
SEGS = ((S, C, C // 8), (0, S, S // 8))
POFF = {S: 0, 0: C + 24}


def _rg_kernel(x_ref, gt_ref, cw_ref, cb_ref, wbd_ref, bias_ref, sp_ref, o_ref,
               xp_ref, af_ref, bf_ref, ar_ref, br_ref, cf_ref, cr_ref):
    sub8 = lax.broadcasted_iota(jnp.int32, (8, 1), 0)
    cw = cw_ref[...]
    cb = cb_ref[...]
    nl2e8 = -RG_C * math.log2(math.e)
    spf = sp_ref[0:1, :] * nl2e8
    spr = sp_ref[1:2, :] * nl2e8

    for seg0, rows, nslot in SEGS:
        p0 = POFF[seg0]

        for r in range(nslot):
            xp_ref[p0 + 16 + 8 * r:p0 + 24 + 8 * r, :] = x_ref[pl.ds(seg0 + r, 8, stride=nslot), :]
        for slot, src, sh in ((0, nslot - 2, 1), (1, nslot - 1, 1), (nslot + 2, 0, 7)):
            v = pltpu.roll(xp_ref[p0 + 16 + 8 * src:p0 + 24 + 8 * src, :], sh, 0)
            xp_ref[p0 + 8 * slot:p0 + 8 * slot + 8, :] = jnp.where(sub8 == (0 if sh == 1 else 7), 0.0, v)

        for t0 in range(0, rows, 256):
            xc = cb
            for j in range(4):
                xc = xc + xp_ref[p0 + 8 * j + t0:p0 + 8 * j + t0 + 256, :] * cw[j:j + 1, :]
            g = _dot(xc.astype(BF16), wbd_ref[...])
            for d, (a_ref, b_ref, sp) in enumerate(((af_ref, bf_ref, spf), (ar_ref, br_ref, spr))):
                rg = _sigmoid(g[:, (2 * d) * RGH:(2 * d + 1) * RGH] + bias_ref[2 * d:2 * d + 1, :])
                ig = _sigmoid(g[:, (2 * d + 1) * RGH:(2 * d + 2) * RGH] + bias_ref[2 * d + 1:2 * d + 2, :])
                a = jnp.exp2(rg * sp)
                a_ref[seg0 + t0:seg0 + t0 + 256, :] = a
                om = 1.0 - a * a
                root = jnp.where(om > 0.0, om * lax.rsqrt(om), 0.0)
                b_ref[seg0 + t0:seg0 + t0 + 256, :] = root * (ig * xc)

    one = jnp.ones((8, RGH), F32)
    zero = jnp.zeros((8, RGH), F32)

    def step(a_ref, b_ref, t, p, h):
        t = pl.multiple_of(t, 8)
        a = a_ref[pl.ds(t, 8), :]
        h = a * h + b_ref[pl.ds(t, 8), :]
        p = a * p
        a_ref[pl.ds(t, 8), :] = p
        b_ref[pl.ds(t, 8), :] = h
        return p, h

    for seg0, rows, nslot in SEGS:
        def body(i, carry):
            fwd, rev = carry
            fwd = step(af_ref, bf_ref, seg0 + 8 * i, *fwd)
            rev = step(ar_ref, br_ref, seg0 + 8 * (nslot - 1 - i), *rev)
            return fwd, rev

        lax.fori_loop(0, nslot, body, ((one, zero),) * 2, unroll=8)

    def row(ref, t):
        return ref[t:t + 1, :]

    cf = jnp.zeros((1, RGH), F32)
    cr = jnp.zeros((1, RGH), F32)
    for k, (seg0, rows, nslot) in enumerate(SEGS):
        last = seg0 + 8 * (nslot - 1)
        for s in range(8):
            cf_ref[8 * k + s:8 * k + s + 1, :] = cf
            cf = row(bf_ref, last + s) + row(af_ref, last + s) * cf
        for s in range(7, -1, -1):
            cr_ref[8 * k + s:8 * k + s + 1, :] = cr
            cr = row(br_ref, seg0 + s) + row(ar_ref, seg0 + s) * cr

    for k, (seg0, rows, nslot) in enumerate(SEGS):
        cft = jnp.concatenate([cf_ref[8 * k:8 * k + 8, :]] * 32, axis=0)
        crt = jnp.concatenate([cr_ref[8 * k:8 * k + 8, :]] * 32, axis=0)
        for t0 in range(0, rows, 256):
            sl = slice(seg0 + t0, seg0 + t0 + 256)
            bf_ref[sl, :] = (bf_ref[sl, :] + af_ref[sl, :] * cft) + (br_ref[sl, :] + ar_ref[sl, :] * crt)

    for seg0, rows, nslot in SEGS:
        per = nslot // 8
        for q in range(rows // 8):
            s, r0 = divmod(q, per)
            af_ref[seg0 + 8 * q:seg0 + 8 * q + 8, :] = bf_ref[pl.ds(seg0 + 64 * r0 + s, 8, stride=8), :]

    c0 = math.sqrt(2.0 / math.pi)
    for r0 in range(0, T, 256):
        gt = gt_ref[r0:r0 + 256, :].astype(F32)
        gelu = 0.5 * gt * (1.0 + jnp.tanh(c0 * (gt + 0.044715 * (gt * gt * gt))))
        o_ref[r0:r0 + 256, :] = (af_ref[r0:r0 + 256, :] * gelu).astype(o_ref.dtype)


def rglru(z, conv_w, conv_b, wbd, bias4, softplus_neg_lam):
    nh = RG_W // RGH
    return pl.pallas_call(
        _rg_kernel,
        grid=(B, nh),
        in_specs=[pl.BlockSpec((None, T, RGH), lambda b, j: (b, 0, j)),
                  pl.BlockSpec((None, T, RGH), lambda b, j: (b, 0, nh + j)),
                  pl.BlockSpec((4, RGH), lambda b, j: (0, j)),
                  pl.BlockSpec((1, RGH), lambda b, j: (0, j)),
                  pl.BlockSpec((None, RGH, 4 * RGH), lambda b, j: (j, 0, 0)),
                  pl.BlockSpec((4, RGH), lambda b, j: (0, j)),
                  pl.BlockSpec((2, RGH), lambda b, j: (0, j))],
        out_specs=pl.BlockSpec((None, T, RGH), lambda b, j: (b, 0, j)),
        out_shape=jax.ShapeDtypeStruct((B, T, RG_W), BF16),
        scratch_shapes=[pltpu.VMEM((T + 48, RGH), F32)] + [pltpu.VMEM((T, RGH), F32)] * 4
                       + [pltpu.VMEM((16, RGH), F32)] * 2,
        compiler_params=_cp(("parallel", "parallel")),
        name="rglru",
    )(z, z, conv_w, conv_b.reshape(1, RG_W), wbd, bias4, softplus_neg_lam)


def _softplus_kernel(x_ref, o_ref):
    y = -x_ref[...]
    o_ref[...] = jnp.maximum(y, 0.0) + jnp.log(1.0 + jnp.exp(-jnp.abs(y)))


def softplus_neg(lam):
    return pl.pallas_call(
        _softplus_kernel,
        out_shape=jax.ShapeDtypeStruct(lam.shape, F32),
        name="softplus_neg",
    )(lam)


def _hfilt_kernel(z_ref, w1_ref, b1_ref, fr_ref, w2_ref, b2_ref, w3_ref, dec_ref, x1_ref, x2_ref, nyq_ref,
                  s1_ref, s2_ref, *, n, tr):
    fr = fr_ref[...]
    w1h, w1l = _split(w1_ref[...])
    w2h, w2l = _split(w2_ref[...])
    w3h, w3l = _split(w3_ref[...])
    tot = jnp.zeros((1, HY_W), F32)
    alt = jnp.zeros((1, HY_W), F32)
    sign = jnp.where(lax.broadcasted_iota(jnp.int32, (tr, 1), 0) % 2 == 0, 1.0, -1.0)
    for r0 in range(0, n, tr):
        zh, zl = _split(z_ref[r0:r0 + tr, :])
        h = jnp.sin(fr * (_dot3(zh, zl, w1h, w1l) + b1_ref[...]))
        hh, hl = _split(h)
        h = jnp.sin(fr * (_dot3(hh, hl, w2h, w2l) + b2_ref[...]))
        hh, hl = _split(h)
        h = _dot3(hh, hl, w3h, w3l)
        dec = dec_ref[r0:r0 + tr, :]
        h0 = h[:, :HY_W] * dec
        h1 = h[:, HY_W:] * dec
        if r0 == 0:
            h1 = jnp.where(lax.broadcasted_iota(jnp.int32, (tr, 1), 0) == 0, 0.0, h1)
        tot = tot + jnp.sum(jnp.abs(h0) + jnp.abs(h1), axis=0, keepdims=True)
        alt = alt + jnp.sum((h0 + h1) * sign, axis=0, keepdims=True)
        s1_ref[r0:r0 + tr, :] = h0 + h1
        s2_ref[r0:r0 + tr, :] = h0 - h1
    inv = 1.0 / tot
    for r0 in range(0, n, tr):
        x1_ref[r0:r0 + tr, :] = (s1_ref[r0:r0 + tr, :] * inv).astype(BF16)
        x2_ref[r0:r0 + tr, :] = (s2_ref[r0:r0 + tr, :] * inv).astype(BF16)
    nyq_ref[...] = jnp.broadcast_to(alt * inv, (8, HY_W))


def hyena_filter_halves(n, zemb, w1p, b1, freq, w2, b2, w3, decay):
    kern = functools.partial(_hfilt_kernel, n=n, tr=256)
    full = lambda shape: pl.BlockSpec(shape, lambda o: (0,) * len(shape))
    return pl.pallas_call(
        kern,
        grid=(2,),
        in_specs=[full((n, 128)), full((128, HY_HID)), full((1, HY_HID)), full((1, HY_HID)),
                  full((HY_HID, HY_HID)), full((1, HY_HID)),
                  pl.BlockSpec((HY_HID, 2 * HY_W), lambda o: (0, o)),
                  full((n, HY_W))],
        out_specs=[pl.BlockSpec((n, HY_W), lambda o: (0, o)),
                   pl.BlockSpec((n, HY_W), lambda o: (0, o)),
                   pl.BlockSpec((8, HY_W), lambda o: (0, o))],
        out_shape=[jax.ShapeDtypeStruct((n, 2 * HY_W), BF16),
                   jax.ShapeDtypeStruct((n, 2 * HY_W), BF16),
                   jax.ShapeDtypeStruct((8, 2 * HY_W), F32)],
        scratch_shapes=[pltpu.VMEM((n, HY_W), F32), pltpu.VMEM((n, HY_W), F32)],
        compiler_params=_cp(("parallel",)),
        name=f"hyena_filter_{n}",
    )(zemb, w1p, b1.reshape(1, HY_HID), freq.reshape(1, HY_HID), w2, b2.reshape(1, HY_HID), w3, decay)


def _kspec_kernel(at_ref, ab_ref, x1_ref, x2_ref, kt_ref, kb_ref):
    kt_ref[...] = _dot(at_ref[...], x1_ref[...])
    kb_ref[...] = _dot(ab_ref[...], x2_ref[...])


def filter_spectrum(fwd, x1, x2, n, tr):
    nt = n // tr
    return pl.pallas_call(
        _kspec_kernel,
        grid=(nt,),
        in_specs=[pl.BlockSpec((tr, n), lambda i: (i, 0)),
                  pl.BlockSpec((tr, n), lambda i: (nt + i, 0)),
                  pl.BlockSpec((n, 2 * HY_W), lambda i: (0, 0)),
                  pl.BlockSpec((n, 2 * HY_W), lambda i: (0, 0))],
        out_specs=[pl.BlockSpec((tr, 2 * HY_W), lambda i: (i, 0)),
                   pl.BlockSpec((tr, 2 * HY_W), lambda i: (i, 0))],
        out_shape=[jax.ShapeDtypeStruct((n, 2 * HY_W), F32)] * 2,
        compiler_params=_cp(("parallel",)),
        name=f"filter_spectrum_{n}",
    )(fwd, fwd, x1, x2)


def _hfwd_kernel(f_ref, x_ref, kt_ref, kb_ref, kn_ref, yr_ref, yi_ref, xb_ref, *, n, tr):
    i = pl.program_id(1)

    @pl.when(i == 0)
    def _():
        xb_ref[...] = x_ref[...].astype(BF16)

    ts = min(tr, 512)
    hw = HY_W // 2
    pieces = []
    for r0 in range(0, tr, ts):
        top = pl.multiple_of(i * tr + r0, ts)
        tab = jnp.concatenate([f_ref[pl.ds(top, ts), :], f_ref[pl.ds(n + top, ts), :]], axis=0)
        for c0 in (0, hw):
            pieces.append((r0, c0, _dot(tab, xb_ref[:, c0:c0 + hw])))
    first = (lax.broadcasted_iota(jnp.int32, (ts, 1), 0) == 0) & (i == 0)
    for r0, c0, u in pieces:
        ur = u[:ts]
        ui = u[ts:]
        kr = kt_ref[r0:r0 + ts, c0:c0 + hw]
        ki = kb_ref[r0:r0 + ts, c0:c0 + hw]
        if r0 == 0:
            yr = ur * kr - jnp.where(first, 0.0, ui * ki)
            yi = jnp.where(first, ui * kn_ref[0:1, c0:c0 + hw], ur * ki + ui * kr)
        else:
            yr = ur * kr - ui * ki
            yi = ur * ki + ui * kr
        yr_ref[r0:r0 + ts, c0:c0 + hw] = yr.astype(BF16)
        yi_ref[r0:r0 + ts, c0:c0 + hw] = yi.astype(BF16)


def hyena_forward(fwd, x, x_rb, kt, kb, nyq, order, n, tr, name):
    nt = n // tr
    kern = functools.partial(_hfwd_kernel, n=n, tr=tr)
    return pl.pallas_call(
        kern,
        grid=(B, nt),
        in_specs=[pl.BlockSpec((2 * n, n), lambda b, i: (0, 0), pipeline_mode=pl.Buffered(1)),
                  pl.BlockSpec((None, n, HY_W), lambda b, i: (b, x_rb, 0)),
                  pl.BlockSpec((tr, HY_W), lambda b, i: (i, order)),
                  pl.BlockSpec((tr, HY_W), lambda b, i: (i, order)),
                  pl.BlockSpec((8, HY_W), lambda b, i: (0, order))],
        out_specs=[pl.BlockSpec((None, tr, HY_W), lambda b, i: (b, i, 0)),
                   pl.BlockSpec((None, tr, HY_W), lambda b, i: (b, i, 0))],
        out_shape=[jax.ShapeDtypeStruct((B, n, HY_W), BF16)] * 2,
        scratch_shapes=[pltpu.VMEM((n, HY_W), BF16)],
        compiler_params=_cp(("parallel", "arbitrary")),
        name=name,
    )(fwd, x, kt, kb, nyq)


def _hinv_kernel(inv_ref, yr_ref, yi_ref, up_ref, gt_ref, bias_ref, o_ref, *, n, tmi):
    rows = pl.ds(pl.multiple_of(pl.program_id(1) * tmi, tmi), tmi)
    y = _dot(inv_ref[rows, 0:n], yr_ref[...]) + _dot(inv_ref[rows, n:2 * n], yi_ref[...])
    o_ref[...] = (gt_ref[...] * (y + up_ref[...].astype(F32) * bias_ref[...])).astype(o_ref.dtype)


def hyena_inverse(inv, yr, yi, uprev, uprev_rb, gate, gate_rb, gate_cb, bias, n, tmi, out_dtype, name):
    return pl.pallas_call(
        functools.partial(_hinv_kernel, n=n, tmi=tmi),
        grid=(B, n // tmi),
        in_specs=[pl.BlockSpec((n, 2 * n), lambda b, i: (0, 0), pipeline_mode=pl.Buffered(1)),
                  pl.BlockSpec((None, n, HY_W), lambda b, i: (b, 0, 0)),
                  pl.BlockSpec((None, n, HY_W), lambda b, i: (b, 0, 0)),
                  pl.BlockSpec((None, tmi, HY_W), lambda b, i: (b, uprev_rb + i, 0)),
                  pl.BlockSpec((None, tmi, HY_W), lambda b, i: (b, gate_rb + i, gate_cb)),
                  pl.BlockSpec((1, HY_W), lambda b, i: (0, 0))],
        out_specs=pl.BlockSpec((None, tmi, HY_W), lambda b, i: (b, i, 0)),
        out_shape=jax.ShapeDtypeStruct((B, n, HY_W), out_dtype),
        compiler_params=_cp(("parallel", "parallel")),
        name=name,
    )(inv, yr, yi, uprev, gate, bias.reshape(1, HY_W))


def _rope128(r, c, sa, sb):
    return r * c + pltpu.roll(r, 32, 1) * sa + pltpu.roll(r, 96, 1) * sb


def _qkv_kernel(x_ref, g_ref, m_ref, win_ref, qg_ref, kvg_ref, wq_ref, wk_ref, wv_ref, c_ref, sa_ref, sb_ref,
                q_ref, k_ref, v_ref, *, tm):
    x = x_ref[...]
    y = x * lax.rsqrt(jnp.mean(x * x, axis=-1, keepdims=True) + EPS) * g_ref[...]
    row = pl.program_id(1) * tm + lax.broadcasted_iota(jnp.int32, (tm, 1), 0)
    h = (y * (1.0 + _mod_rows(m_ref, 1, row, S)) + _mod_rows(m_ref, 0, row, S)).astype(BF16)
    z = _dot(h, win_ref[...])
    c = c_ref[...]
    sa = sa_ref[...]
    sb = sb_ref[...]
    zq = z[:, :Q_LORA]
    qn = (zq * lax.rsqrt(jnp.mean(zq * zq, axis=-1, keepdims=True) + EPS) * qg_ref[...]).astype(BF16)
    zkv = z[:, Q_LORA:Q_LORA + KV_LORA]
    ckv = (zkv * lax.rsqrt(jnp.mean(zkv * zkv, axis=-1, keepdims=True) + EPS) * kvg_ref[...]).astype(BF16)
    kr = _rope128(z[:, Q_LORA + KV_LORA:], c, sa, sb).astype(BF16)
    for h in range(HEADS):
        q = _dot(qn, wq_ref[:, h * HQ:(h + 1) * HQ])
        q_ref[:, h * HQ:h * HQ + NOPE] = q[:, :NOPE].astype(BF16)
        q_ref[:, h * HQ + NOPE:(h + 1) * HQ] = _rope128(q[:, NOPE:], c, sa, sb).astype(BF16)
        k_ref[:, h * HQ:h * HQ + NOPE] = _dot(ckv, wk_ref[:, h * NOPE:(h + 1) * NOPE]).astype(BF16)
        k_ref[:, h * HQ + NOPE:(h + 1) * HQ] = kr
    v_ref[...] = _dot(ckv, wv_ref[...]).astype(BF16)


def mla_qkv(xs, g, mod, w_in, q_g, kv_g, wq, wk, wv, ctab, satab, sbtab, tm):
    full = lambda shape: pl.BlockSpec(shape, lambda b, i: (0,) * len(shape))
    tab = pl.BlockSpec((tm, 128), lambda b, i: (i, 0))
    return pl.pallas_call(
        functools.partial(_qkv_kernel, tm=tm),
        grid=(B, T // tm),
        in_specs=[pl.BlockSpec((None, tm, D), lambda b, i: (b, i, 0)),
                  full((1, D)),
                  pl.BlockSpec((None, 2, 6, D), lambda b, i: (b, 0, 0, 0)),
                  full((D, ODD_IN_P)),
                  full((1, Q_LORA)), full((1, KV_LORA)),
                  full((Q_LORA, HEADS * HQ)), full((KV_LORA, HEADS * NOPE)), full((KV_LORA, HEADS * VH)),
                  tab, tab, tab],
        out_specs=[pl.BlockSpec((None, tm, HEADS * HQ), lambda b, i: (b, i, 0)),
                   pl.BlockSpec((None, tm, HEADS * HQ), lambda b, i: (b, i, 0)),
                   pl.BlockSpec((None, tm, HEADS * VH), lambda b, i: (b, i, 0))],
        out_shape=[jax.ShapeDtypeStruct((B, T, HEADS * HQ), BF16),
                   jax.ShapeDtypeStruct((B, T, HEADS * HQ), BF16),
                   jax.ShapeDtypeStruct((B, T, HEADS * VH), BF16)],
        compiler_params=_cp(("parallel", "parallel")),
        name="mla_qkv",
    )(xs, g.reshape(1, D), mod, w_in, q_g.reshape(1, Q_LORA), kv_g.reshape(1, KV_LORA), wq, wk, wv,
      ctab, satab, sbtab)


def _attn_kernel(q_ref, k_ref, v_ref, o_ref, *, scale, ts):
    c = scale * math.log2(math.e)
    kc = 256
    tq = q_ref.shape[0]
    nh = q_ref.shape[1] // HQ
    tiles = [(h, r0) for h in range(nh) for r0 in range(0, tq, ts)]

    def scores(h, r0):
        return lax.dot_general(q_ref[r0:r0 + ts, h * HQ:(h + 1) * HQ], k_ref[:, h * HQ:(h + 1) * HQ],
                               (((1,), (1,)), ((), ())), preferred_element_type=F32)

    ahead = scores(*tiles[0])
    for n, (h, r0) in enumerate(tiles):
        s = ahead
        if n + 1 < len(tiles):
            ahead = scores(*tiles[n + 1])
        m = jnp.max(s, axis=-1, keepdims=True)
        acc = None
        lv = None
        for c0 in range(0, T, kc):
            p = jnp.exp2((s[:, c0:c0 + kc] - m) * c)
            pl_ = p[:, :128]
            for l0 in range(128, kc, 128):
                pl_ = pl_ + p[:, l0:l0 + 128]
            lv = pl_ if lv is None else lv + pl_
            d = _dot(p.astype(BF16), v_ref[c0:c0 + kc, h * VH:(h + 1) * VH])
            acc = d if acc is None else acc + d
        l = jnp.sum(lv, axis=-1, keepdims=True)
        o_ref[r0:r0 + ts, h * VH:(h + 1) * VH] = (acc * (1.0 / l)).astype(o_ref.dtype)


def mla_attention(q, k, v, tq, hp):
    kern = functools.partial(_attn_kernel, scale=(NOPE + ROPE) ** -0.5, ts=512)
    return pl.pallas_call(
        kern,
        grid=(B, HEADS // hp, S // tq),
        in_specs=[pl.BlockSpec((None, tq, hp * HQ), lambda b, h, i: (b, i, h)),
                  pl.BlockSpec((None, T, hp * HQ), lambda b, h, i: (b, 0, h)),
                  pl.BlockSpec((None, T, hp * VH), lambda b, h, i: (b, 0, h))],
        out_specs=pl.BlockSpec((None, tq, hp * VH), lambda b, h, i: (b, i, h)),
        out_shape=jax.ShapeDtypeStruct((B, S, HEADS * VH), BF16),
        compiler_params=_cp(("parallel", "parallel", "parallel")),
        name="mla_attention",
    )(q, k, v)


@functools.lru_cache(maxsize=None)
def _dft_tables(n):
    m = 2 * n
    j = np.arange(n, dtype=np.int64)
    kk = np.arange(n, dtype=np.int64)
    ang = 2.0 * np.pi * ((kk[:, None] * j[None, :]) % m).astype(np.float64) / m
    cosm = np.cos(ang)
    sinm = np.sin(ang)
    fwd = np.concatenate([cosm, -sinm], axis=0)
    fwd[n, :] = np.where(j % 2 == 0, 1.0, -1.0)
    inv = np.concatenate([2.0 * cosm.T, -2.0 * sinm.T], axis=1) / m
    inv[:, 0] = 1.0 / m
    inv[:, n] = np.where(j % 2 == 0, 1.0, -1.0) / m
    return fwd.astype(np.float32).astype(BF16), inv.astype(np.float32).astype(BF16)


def _filter_consts(n):
    pos = jnp.arange(n, dtype=F32)
    t = jnp.linspace(0.0, 1.0, n, dtype=F32)[:, None]
    bands = (HY_EMB - 1) // 2
    w = 2.0 * math.pi * pos / n
    f = jnp.linspace(1e-4, bands - 1, bands, dtype=F32)
    ang = w[:, None] * f[None, :]
    z = jnp.concatenate([t, jnp.cos(ang), -jnp.sin(ang)], axis=-1)
    z = jnp.pad(z, ((0, 0), (0, 128 - HY_EMB)))
    max_decay = math.log(1e-2) / 0.3
    min_decay = math.log(1e-2) / 1.5
    deltas = jnp.linspace(min_decay, max_decay, HY_W, dtype=F32)
    decay = jnp.exp(-t * jnp.abs(deltas)[None, :])
    return z, decay


def _rope_tables():
    rows = S // 64
    row = jnp.repeat(jnp.arange(rows, dtype=F32), 64)
    col = jnp.tile(jnp.arange(64, dtype=F32), rows)
    n_freq = ROPE // 4
    inv = 10000.0 ** (-jnp.arange(n_freq, dtype=F32) / n_freq)
    ang = jnp.concatenate([row[:, None] * inv[None, :], col[:, None] * inv[None, :]], axis=-1)
    cos = jnp.concatenate([jnp.cos(ang), jnp.ones((C, 32), F32)], axis=0)
    sin = jnp.concatenate([jnp.sin(ang), jnp.zeros((C, 32), F32)], axis=0)
    z32 = jnp.zeros((T, 32), F32)
    z64 = jnp.zeros((T, 64), F32)
    ctab = jnp.concatenate([cos, cos, z64], axis=-1)
    satab = jnp.concatenate([z32, sin, z64], axis=-1)
    sbtab = jnp.concatenate([-sin, z32, z64], axis=-1)
    return ctab, satab, sbtab


def _even_layer(x, ctx, mod, norm1_g, w_in, rg_conv_w, rg_conv_b, rg_wa, rg_ba, rg_wx, rg_bx, rg_lambda,
                hy_conv_w, hy_conv_b, f_w1, f_b1, f_freq, f_w2, f_b2, f_w3, hy_bias, w_out):
    z, xs = even_in_proj(x, ctx, norm1_g, mod, w_in.astype(BF16))

    def blockdiag(w):
        per = RGH // RG_HD
        out = jnp.zeros((RG_W // RGH, RGH, RGH), F32)
        for h in range(RG_HEADS):
            g, l = divmod(h, per)
            out = out.at[g, l * RG_HD:(l + 1) * RG_HD, l * RG_HD:(l + 1) * RG_HD].set(w[h])
        return out

    wbd = jnp.concatenate([blockdiag(rg_wa[0]), blockdiag(rg_wx[0]), blockdiag(rg_wa[1]), blockdiag(rg_wx[1])],
                          axis=-1).astype(BF16)
    bias4 = jnp.stack([rg_ba[0], rg_bx[0], rg_ba[1], rg_bx[1]], axis=0)
    a = rglru(z, rg_conv_w, rg_conv_b, wbd, bias4, softplus_neg(rg_lambda))

    vc = hyena_short_conv(z, hy_conv_w, hy_conv_b)
    w1p = jnp.pad(f_w1, ((0, 128 - HY_EMB), (0, 0)))
    outs = []
    for n, rb in ((S, 0), (C, S // C)):
        fwd, inv = _dft_tables(n)
        zemb, decay = _filter_consts(n)
        x1, x2, nyq = hyena_filter_halves(n, zemb, w1p, f_b1, f_freq, f_w2, f_b2, f_w3, decay)
        tr = min(n, 512)
        tmi = min(n, 1024)
        kt, kb = filter_spectrum(fwd, x1, x2, n, tr)
        grb = rb * (n // tmi)
        yr, yi = hyena_forward(fwd, vc, rb, kt, kb, nyq, 0, n, tmi, f"hyena_fwd1_{n}")
        u1 = hyena_inverse(inv, yr, yi, vc, grb, vc, grb, 1, hy_bias[0], n, tmi, F32, f"hyena_inv1_{n}")
        yr, yi = hyena_forward(fwd, u1, 0, kt, kb, nyq, 1, n, tmi, f"hyena_fwd2_{n}")
        outs.append(hyena_inverse(inv, yr, yi, u1, 0, vc, grb, 2, hy_bias[1], n, tmi, BF16,
                                  f"hyena_inv2_{n}"))
    return ([a, jnp.concatenate(outs, axis=1)], w_out.astype(BF16)), xs


def _odd_layer(xs, mod, norm1_g, w_in, q_g, kv_g, w_uq, w_ukv, w_o):
    perm = np.concatenate([np.arange(0, ROPE, 2), np.arange(1, ROPE, 2)])
    w_rope = jnp.pad(w_in[:, Q_LORA + KV_LORA:][:, perm], ((0, 0), (0, 128 - ROPE)))
    w_in_p = jnp.concatenate([w_in[:, :Q_LORA + KV_LORA], w_rope], axis=-1).astype(BF16)
    wq = w_uq.reshape(Q_LORA, HEADS, NOPE + ROPE)
    wq = jnp.concatenate([wq[..., :NOPE], wq[..., NOPE:][..., perm],
                          jnp.zeros((Q_LORA, HEADS, HQ - NOPE - ROPE), F32)], axis=-1)
    wq = wq.reshape(Q_LORA, HEADS * HQ).astype(BF16)
    wkv = w_ukv.reshape(KV_LORA, HEADS, NOPE + VH)
    wk = wkv[..., :NOPE].reshape(KV_LORA, HEADS * NOPE).astype(BF16)
    wv = wkv[..., NOPE:].reshape(KV_LORA, HEADS * VH).astype(BF16)
    ctab, satab, sbtab = _rope_tables()
    q, k, v = mla_qkv(xs, norm1_g, mod, w_in_p, q_g, kv_g, wq, wk, wv, ctab, satab, sbtab, 768)
    return [mla_attention(q, k, v, 2048, 2)], w_o.astype(BF16)


def _ffn(xs, t, mix, mod, layer, norm2_g, w_up, conv_w, conv_b, w_down, final_g, tm, nin, sb, seg_starts,
         ctx_start, final):
    parts, w_out = mix
    return mixer_out_conv_ffn(xs, t, parts, w_out, norm2_g, mod, layer, w_up, conv_w, conv_b, w_down, final_g,
                              tm, nin, sb, seg_starts, ctx_start, final, f"mix_out_conv_ffn_{layer}")


def kernel(x, c, ctx, c_ctx, ada_w, ada_b, norm1_g, norm2_g, ev_w_in, ev_rg_conv_w, ev_rg_conv_b, ev_rg_wa, ev_rg_ba, ev_rg_wx, ev_rg_bx, ev_rg_lambda, ev_hy_conv_w, ev_hy_conv_b, ev_hy_f_w1, ev_hy_f_b1, ev_hy_f_freq, ev_hy_f_w2, ev_hy_f_b2, ev_hy_f_w3, ev_hy_bias, ev_w_out, od_w_in, od_q_norm_g, od_kv_norm_g, od_w_uq, od_w_ukv, od_w_o, ffn_w_up, ffn_conv_w, ffn_conv_b, ffn_w_down, final_g):
    cvec = jnp.concatenate([c, c_ctx[None], jnp.zeros((16 - B - 1, D), F32)], axis=0)
    mod = modulation(cvec, ada_w, ada_b)

    def mod_sel(layer):
        lat = mod[layer, :B].reshape(B, 1, 6, D)
        cx = jnp.broadcast_to(mod[layer, B].reshape(1, 1, 6, D), (B, 1, 6, D))
        return jnp.concatenate([lat, cx], axis=1)

    m0 = mod_sel(0)
    mix, xs = _even_layer(x, ctx, m0, norm1_g[0], ev_w_in[0], ev_rg_conv_w[0], ev_rg_conv_b[0], ev_rg_wa[0], ev_rg_ba[0],
                      ev_rg_wx[0], ev_rg_bx[0], ev_rg_lambda[0], ev_hy_conv_w[0], ev_hy_conv_b[0], ev_hy_f_w1[0],
                      ev_hy_f_b1[0], ev_hy_f_freq[0], ev_hy_f_w2[0], ev_hy_f_b2[0], ev_hy_f_w3[0], ev_hy_bias[0],
                      ev_w_out[0])
    w_up = ffn_w_up.astype(BF16)
    w_down = ffn_w_down.astype(BF16)
    xs = _ffn(xs, T, mix, m0, 0, norm2_g[0], w_up, ffn_conv_w[0], ffn_conv_b[0], w_down, final_g,
              384, 2, 128, (0, S), S, False)
    m1 = mod_sel(1)
    mix = _odd_layer(xs, m1, norm1_g[1], od_w_in[0], od_q_norm_g[0], od_kv_norm_g[0], od_w_uq[0], od_w_ukv[0],
                     od_w_o[0])
    return _ffn(xs, S, mix, m1, 1, norm2_g[1], w_up, ffn_conv_w[1], ffn_conv_b[1], w_down, final_g,
                256, 2, 128, (0,), None, True)
```
